```python
import math
import jax, jax.numpy as jnp
from jax import lax
import numpy as np

D_MODEL = 4096
BATCH = 4
SEQ = 2048
DEPTH = 2
DEC_BATCH = 8
DEC_SEQ = 8
PAST_LEN = 16384
PAGE_SIZE = 128

HEAD_DIM = 128
H_A = 8
KV_A = 2
H_IDX = 16
D_IDX = 128
TOPK_A = 256
H_B = 8
KV_B = 2
CMP_LEN = 32
CMP_STRIDE = 16
SLC_LEN = 64
N_SLC = 16
WINDOW = 512
H_C = 8
MOBA_BLOCK = 256
MOBA_TOPK = 3
N_BUCKETS = 32
MAX_DISTANCE = 128
H_ALL = H_A + H_B + H_C
N_BRANCH = 3
D_FF = 11008
D_PLE = 256
QBLK = 32
EPS = 1e-6
NEG = -1e30

_IN_SIZES = (
    H_A * HEAD_DIM, KV_A * HEAD_DIM, KV_A * HEAD_DIM,
    H_IDX * D_IDX, H_IDX, D_IDX,
    H_B * HEAD_DIM, N_BRANCH * H_B,
    KV_B * HEAD_DIM, KV_B * HEAD_DIM,
    KV_B * HEAD_DIM, KV_B * HEAD_DIM,
    KV_B * HEAD_DIM, KV_B * HEAD_DIM,
    H_C * HEAD_DIM, H_C * HEAD_DIM, H_C * HEAD_DIM,
)
IN_COLS = sum(_IN_SIZES)
_IN_SPLITS = tuple(np.cumsum(_IN_SIZES)[:-1].tolist())

kernel_name = "hybrid_dsa_nsa_moba_macaron_step"


def _rms(x, g):
    xf = x.astype(jnp.float32)
    y = xf * lax.rsqrt(jnp.mean(xf * xf, axis=-1, keepdims=True) + EPS)
    return (y * g.astype(jnp.float32)).astype(x.dtype)


def _swiglu(x, wg, wu, wd):
    return (jax.nn.silu(x @ wg) * (x @ wu)) @ wd


def _t5_bucket(n):
    n = jnp.maximum(n, 0)
    exact = N_BUCKETS // 2
    nf = jnp.maximum(n, 1).astype(jnp.float32)
    big = exact + (jnp.log(nf / exact) * ((N_BUCKETS - exact) / math.log(MAX_DISTANCE / exact))).astype(jnp.int32)
    return jnp.where(n < exact, n, jnp.minimum(big, N_BUCKETS - 1))


def _masked_softmax(logits, mask):
    s = jnp.where(mask, logits.astype(jnp.float32), NEG)
    e = jnp.exp(s - jnp.max(s, axis=-1, keepdims=True)) * mask
    return e / jnp.maximum(jnp.sum(e, axis=-1, keepdims=True), 1e-30)


def _query_blocks(fn, qpos, *xs):
    T = qpos.shape[0]
    qb = QBLK if T % QBLK == 0 else T
    nb = T // qb
    blocked = tuple(jnp.swapaxes(x.reshape((x.shape[0], nb, qb) + x.shape[2:]), 0, 1) for x in xs)
    out = lax.map(lambda a: fn(a[0], a[1], *a[2:]), (jnp.arange(nb), qpos.reshape(nb, qb)) + blocked)
    out = jnp.swapaxes(out, 0, 1)
    return out.reshape((out.shape[0], T) + out.shape[3:])


def _mixer_a(q, k, v, q_idx, w_idx, k_idx, qpos, tab):
    b, L = k.shape[0], k.shape[1]
    topk = min(TOPK_A, L // 4)
    rep = H_A // KV_A
    kpos = jnp.arange(L)
    bi = jnp.arange(b)[:, None, None]

    def blk(i, pos, qq, qi, wi):
        qbn = pos.shape[0]
        s = jnp.einsum('bqhd,bsd->bqhs', qi, k_idx).astype(jnp.float32) * (D_IDX ** -0.5)
        score = jnp.einsum('bqhs,bqh->bqs', jax.nn.relu(s), wi.astype(jnp.float32)) * (H_IDX ** -0.5)
        causal = kpos[None, None, :] <= pos[None, :, None]
        _, idx = lax.top_k(jnp.where(causal, score, NEG), topk)
        valid = idx <= pos[None, :, None]
        kg = k[bi, idx]
        vg = v[bi, idx]
        qg = qq.reshape(b, qbn, KV_A, rep, HEAD_DIM)
        logits = jnp.einsum('bqgrd,bqkgd->bqgrk', qg, kg) * (HEAD_DIM ** -0.5)
        bias = jnp.moveaxis(tab[_t5_bucket(pos[None, :, None] - idx)], 2, -1).reshape(b, qbn, KV_A, rep, topk)
        p = _masked_softmax(logits + bias, valid[:, :, None, None, :])
        o = jnp.einsum('bqgrk,bqkgd->bqgrd', p.astype(vg.dtype), vg)
        return o.reshape(b, qbn, H_A * HEAD_DIM)

    return _query_blocks(blk, qpos, q, q_idx, w_idx)


def _compress(x, pos_emb, w1, w2):
    b, L, g, d = x.shape
    n_sub = L // CMP_STRIDE
    r = CMP_LEN // CMP_STRIDE
    n_cmp = n_sub - r + 1
    sub = x[:, :n_sub * CMP_STRIDE].reshape(b, n_sub, CMP_STRIDE, g, d)
    blocks = jnp.concatenate([sub[:, j:j + n_cmp] for j in range(r)], axis=2) + pos_emb[:, None, :]
    flat = jnp.moveaxis(blocks, 3, 2).reshape(b, n_cmp, g, CMP_LEN * d)
    return jax.nn.gelu(flat @ w1) @ w2


def _mixer_b(q, gates, kc, vc, ks, vs, kw, vw, kw_pos, qpos, cmp_pos, cmp_w1, cmp_w2, kn_cmp, tab):
    b, L = ks.shape[0], ks.shape[1]
    rep = H_B // KV_B
    scale = HEAD_DIM ** -0.5
    kcmp = _rms(_compress(kc, cmp_pos[0], cmp_w1[0], cmp_w2[0]), kn_cmp)
    vcmp = _compress(vc, cmp_pos[1], cmp_w1[1], cmp_w2[1])
    n_cmp = kcmp.shape[1]
    cend = jnp.arange(n_cmp) * CMP_STRIDE + CMP_LEN - 1
    cstart = cend - (CMP_LEN - 1)
    n_slc = -(-L // SLC_LEN)
    nsel = min(N_SLC, n_slc)
    sstart = jnp.arange(n_slc) * SLC_LEN
    overlap = ((cstart[:, None] < sstart[None, :] + SLC_LEN) & (cend[:, None] >= sstart[None, :])).astype(jnp.float32)
    pad = n_slc * SLC_LEN - L
    ks_blk = jnp.moveaxis(jnp.pad(ks, ((0, 0), (0, pad), (0, 0), (0, 0))).reshape(b, n_slc, SLC_LEN, KV_B, HEAD_DIM), 3, 1)
    vs_blk = jnp.moveaxis(jnp.pad(vs, ((0, 0), (0, pad), (0, 0), (0, 0))).reshape(b, n_slc, SLC_LEN, KV_B, HEAD_DIM), 3, 1)
    n_buf = kw.shape[1] - qpos.shape[0]
    kw_pad = jnp.pad(kw, ((0, 0), (WINDOW, 0), (0, 0), (0, 0)))
    vw_pad = jnp.pad(vw, ((0, 0), (WINDOW, 0), (0, 0), (0, 0)))
    kwp_pad = jnp.concatenate([jnp.full((WINDOW,), -WINDOW, jnp.int32), kw_pos.astype(jnp.int32)])
    tab3 = tab.reshape(N_BUCKETS, KV_B, rep)
    bi = jnp.arange(b)[:, None, None, None]
    gi = jnp.arange(KV_B)[None, None, :, None]
    jj = jnp.arange(n_slc)

    def head_bias(dist):
        bq = tab[_t5_bucket(dist)].reshape(dist.shape + (KV_B, rep))
        return jnp.transpose(bq, (0, 2, 3, 1))[None]

    def blk(i, pos, qq, gg):
        qbn = pos.shape[0]
        qg = qq.reshape(b, qbn, KV_B, rep, HEAD_DIM)
        lc = jnp.einsum('bqgrd,bngd->bqgrn', qg, kcmp) * scale + head_bias(pos[:, None] - cend[None, :])
        pc = _masked_softmax(lc, (cend[None, :] <= pos[:, None])[None, :, None, None, :])
        oc = jnp.einsum('bqgrn,bngd->bqgrd', pc.astype(vcmp.dtype), vcmp)
        imp = jnp.einsum('bqgrn,ns->bqgs', pc, overlap)
        cur = (pos // SLC_LEN)[None, :, None, None]
        forced = (jj == 0) | (jj == cur) | (jj == cur - 1)
        imp = jnp.where(jj > cur, NEG, jnp.where(forced, -NEG, imp))
        _, sidx = lax.top_k(imp, nsel)
        tok = (sidx[..., None] * SLC_LEN + jnp.arange(SLC_LEN)).reshape(b, qbn, KV_B, nsel * SLC_LEN)
        kg = ks_blk[bi, gi, sidx].reshape(b, qbn, KV_B, nsel * SLC_LEN, HEAD_DIM)
        vg = vs_blk[bi, gi, sidx].reshape(b, qbn, KV_B, nsel * SLC_LEN, HEAD_DIM)
        bs = jnp.moveaxis(tab3[_t5_bucket(pos[None, :, None, None] - tok), gi], -1, 3)
        ls = jnp.einsum('bqgrd,bqgkd->bqgrk', qg, kg) * scale + bs
        ps = _masked_softmax(ls, (tok <= pos[None, :, None, None])[:, :, :, None, :])
        o_s = jnp.einsum('bqgrk,bqgkd->bqgrd', ps.astype(vg.dtype), vg)
        start = n_buf + i * qbn
        kwb = lax.dynamic_slice_in_dim(kw_pad, start, WINDOW + qbn, axis=1)
        vwb = lax.dynamic_slice_in_dim(vw_pad, start, WINDOW + qbn, axis=1)
        dist = pos[:, None] - lax.dynamic_slice_in_dim(kwp_pad, start, WINDOW + qbn)[None, :]
        lw = jnp.einsum('bqgrd,bkgd->bqgrk', qg, kwb) * scale + head_bias(dist)
        pw = _masked_softmax(lw, ((dist >= 0) & (dist < WINDOW))[None, :, None, None, :])
        o_w = jnp.einsum('bqgrk,bkgd->bqgrd', pw.astype(vwb.dtype), vwb)
        g5 = gg.reshape(b, qbn, KV_B, rep, N_BRANCH)
        o = g5[..., 0:1] * oc + g5[..., 1:2] * o_s + g5[..., 2:3] * o_w
        return o.reshape(b, qbn, H_B * HEAD_DIM)

    return _query_blocks(blk, qpos, q, gates)


def _mixer_c(q, k, v, qpos, tab):
    b, L = k.shape[0], k.shape[1]
    nblk = L // MOBA_BLOCK
    ksel = min(MOBA_TOPK, nblk)
    scale = HEAD_DIM ** -0.5
    kb = jnp.moveaxis(k[:, :nblk * MOBA_BLOCK].reshape(b, nblk, MOBA_BLOCK, H_C, HEAD_DIM), 3, 1)
    vb = jnp.moveaxis(v[:, :nblk * MOBA_BLOCK].reshape(b, nblk, MOBA_BLOCK, H_C, HEAD_DIM), 3, 1)
    kmean = jnp.mean(kb.astype(jnp.float32), axis=3)
    k_pad = jnp.pad(k, ((0, 0), (0, MOBA_BLOCK), (0, 0), (0, 0)))
    v_pad = jnp.pad(v, ((0, 0), (0, MOBA_BLOCK), (0, 0), (0, 0)))
    bi = jnp.arange(b)[:, None, None, None]
    hi = jnp.arange(H_C)[None, None, :, None]

    def blk(i, pos, qq):
        qbn = pos.shape[0]
        cur = pos // MOBA_BLOCK
        own = (pos[0] // MOBA_BLOCK) * MOBA_BLOCK
        ko = lax.dynamic_slice_in_dim(k_pad, own, MOBA_BLOCK, axis=1)
        vo = lax.dynamic_slice_in_dim(v_pad, own, MOBA_BLOCK, axis=1)
        dist_o = pos[:, None] - (own + jnp.arange(MOBA_BLOCK))[None, :]
        lo = jnp.einsum('bqhd,bkhd->bqhk', qq, ko) * scale + jnp.transpose(tab[_t5_bucket(dist_o)], (0, 2, 1))[None]
        mo = jnp.broadcast_to((dist_o >= 0)[None, :, None, :], lo.shape)
        if ksel > 0:
            gs = jnp.einsum('bqhd,bhnd->bqhn', qq.astype(jnp.float32), kmean)
            past = jnp.arange(nblk)[None, :] < cur[:, None]
            _, bidx = lax.top_k(jnp.where(past[None, :, None, :], gs, NEG), ksel)
            nk = ksel * MOBA_BLOCK
            kg = kb[bi, hi, bidx].reshape(b, qbn, H_C, nk, HEAD_DIM)
            vg = vb[bi, hi, bidx].reshape(b, qbn, H_C, nk, HEAD_DIM)
            tok = (bidx[..., None] * MOBA_BLOCK + jnp.arange(MOBA_BLOCK)).reshape(b, qbn, H_C, nk)
            ms = jnp.broadcast_to((bidx < cur[None, :, None, None])[..., None], bidx.shape + (MOBA_BLOCK,)).reshape(b, qbn, H_C, nk)
            ls = jnp.einsum('bqhd,bqhkd->bqhk', qq, kg) * scale + tab[_t5_bucket(pos[None, :, None, None] - tok), hi]
            p = _masked_softmax(jnp.concatenate([ls, lo], axis=-1), jnp.concatenate([ms, mo], axis=-1))
            o = (jnp.einsum('bqhk,bqhkd->bqhd', p[..., :nk].astype(vg.dtype), vg)
                 + jnp.einsum('bqhk,bkhd->bqhd', p[..., nk:].astype(vo.dtype), vo))
        else:
            p = _masked_softmax(lo, mo)
            o = jnp.einsum('bqhk,bkhd->bqhd', p.astype(vo.dtype), vo)
        return o.reshape(b, qbn, H_C * HEAD_DIM)

    return _query_blocks(blk, qpos, q)


def _layer(x, p, past, past_len, lw, rel_bias):
    (f1n, f1g, f1u, f1d, mn, w_in, qn_a, kn_a, qn_b, kn_b, cmp_pos, cmp_w1, cmp_w2,
     qn_c, kn_c, wb_a, wb_b, wb_c, w_gate, b_gate, w_out, f2n, f2g, f2u, f2d, pn, pwg, pwp) = lw
    b, T, _ = x.shape
    h = x + 0.5 * _swiglu(_rms(x, f1n), f1g, f1u, f1d)
    u = _rms(h, mn)
    (qa, ka, va, qi, wi, ki, q_b, g_b, kbc, vbc, kbs, vbs, kbw, vbw, qc, kc, vc) = jnp.split(u @ w_in, _IN_SPLITS, axis=-1)

    def heads(t, n):
        return t.reshape(b, T, n, -1)

    qa = _rms(heads(qa, H_A), qn_a)
    ka = _rms(heads(ka, KV_A), kn_a)
    va = heads(va, KV_A)
    qi = heads(qi, H_IDX)
    q_b = _rms(heads(q_b, H_B), qn_b)
    g_b = jax.nn.sigmoid(heads(g_b, H_B))
    kbc = heads(kbc, KV_B)
    vbc = heads(vbc, KV_B)
    kbs = _rms(heads(kbs, KV_B), kn_b[1])
    vbs = heads(vbs, KV_B)
    kbw = _rms(heads(kbw, KV_B), kn_b[2])
    vbw = heads(vbw, KV_B)
    qc = _rms(heads(qc, H_C), qn_c)
    kc = _rms(heads(kc, H_C), kn_c)
    vc = heads(vc, H_C)
    win_new = jnp.stack([kbw, vbw], axis=2)
    if past is None:
        def full(j, k_new, v_new):
            return k_new, v_new
        ki_all = ki
        win_all = win_new
    else:
        def full(j, k_new, v_new):
            return (jnp.concatenate([past[j][:, :, 0], k_new], axis=1),
                    jnp.concatenate([past[j][:, :, 1], v_new], axis=1))
        ki_all = jnp.concatenate([past[1], ki], axis=1)
        win_all = jnp.concatenate([past[4], win_new], axis=1)
    qpos = past_len + jnp.arange(T)
    n_buf = win_all.shape[1] - T
    kw_pos = past_len - n_buf + jnp.arange(win_all.shape[1])
    tab_a = rel_bias[:, :H_A]
    tab_b = rel_bias[:, H_A:H_A + H_B]
    tab_c = rel_bias[:, H_A + H_B:]
    ka_all, va_all = full(0, ka, va)
    kbc_all, vbc_all = full(2, kbc, vbc)
    kbs_all, vbs_all = full(3, kbs, vbs)
    kc_all, vc_all = full(5, kc, vc)
    o_a = _mixer_a(qa, ka_all, va_all, qi, wi, ki_all, qpos, tab_a)
    o_b = _mixer_b(q_b, g_b, kbc_all, vbc_all, kbs_all, vbs_all, win_all[:, :, 0], win_all[:, :, 1],
                   kw_pos, qpos, cmp_pos, cmp_w1, cmp_w2, kn_b[0], tab_b)
    o_c = _mixer_c(qc, kc_all, vc_all, qpos, tab_c)
    gates = jax.nn.sigmoid(u @ w_gate + b_gate).reshape(b, T, N_BRANCH, D_MODEL)
    merged = gates[:, :, 0] * (o_a @ wb_a) + gates[:, :, 1] * (o_b @ wb_b) + gates[:, :, 2] * (o_c @ wb_c)
    h = h + merged @ w_out
    h = h + 0.5 * _swiglu(_rms(h, f2n), f2g, f2u, f2d)
    h = h + jax.nn.sigmoid(_rms(h, pn) @ pwg) * (p @ pwp)
    state = (jnp.stack([ka, va], axis=2), ki, jnp.stack([kbc, vbc], axis=2), jnp.stack([kbs, vbs], axis=2),
             win_all[:, -WINDOW:], jnp.stack([kc, vc], axis=2))
    return h, state


def setup_inputs(seed: int = 0) -> dict:
    key = jax.random.key(seed)
    keys = iter(jax.random.split(key, 64))

    def nrm(shape, scale=1.0):
        return jax.random.normal(next(keys), shape, jnp.float32) * scale

    def gain(shape):
        return 1.0 + 0.05 * nrm(shape)

    n_pages = PAST_LEN // PAGE_SIZE
    n_pool = (DEC_BATCH * n_pages * 5) // 4

    def pool(*row):
        return nrm((DEPTH, n_pool, PAGE_SIZE) + row)

    x_prompt = nrm((BATCH, SEQ, D_MODEL))
    x_sample = nrm((DEC_BATCH, DEC_SEQ, D_MODEL))
    cache_a_kv = pool(2, KV_A, HEAD_DIM)
    cache_a_kidx = pool(D_IDX)
    cache_b_cmp_kv = pool(2, KV_B, HEAD_DIM)
    cache_b_slc_kv = pool(2, KV_B, HEAD_DIM)
    state_b_win_kv = nrm((DEPTH, DEC_BATCH, min(WINDOW, PAST_LEN), 2, KV_B, HEAD_DIM))
    cache_c_kv = pool(2, H_C, HEAD_DIM)
    page_table = jax.random.permutation(next(keys), n_pool)[:DEC_BATCH * n_pages].reshape(DEC_BATCH, n_pages).astype(jnp.int32)
    p_prompt = nrm((DEPTH, BATCH, SEQ, D_PLE))
    p_sample = nrm((DEPTH, DEC_BATCH, DEC_SEQ, D_PLE))
    sd = D_MODEL ** -0.5
    return {
        "x_prompt": x_prompt, "x_sample": x_sample,
        "cache_a_kv": cache_a_kv, "cache_a_kidx": cache_a_kidx,
        "cache_b_cmp_kv": cache_b_cmp_kv, "cache_b_slc_kv": cache_b_slc_kv,
        "state_b_win_kv": state_b_win_kv, "cache_c_kv": cache_c_kv,
        "page_table": page_table, "p_prompt": p_prompt, "p_sample": p_sample,
        "rel_bias": nrm((N_BUCKETS, H_ALL), 0.5),
        "ffn1_norm": gain((DEPTH, D_MODEL)),
        "ffn1_wg": nrm((DEPTH, D_MODEL, D_FF), sd),
        "ffn1_wu": nrm((DEPTH, D_MODEL, D_FF), sd),
        "ffn1_wd": nrm((DEPTH, D_FF, D_MODEL), D_FF ** -0.5),
        "mix_norm": gain((DEPTH, D_MODEL)),
        "w_in": nrm((DEPTH, D_MODEL, IN_COLS), sd),
        "qn_a": gain((DEPTH, HEAD_DIM)), "kn_a": gain((DEPTH, HEAD_DIM)),
        "qn_b": gain((DEPTH, HEAD_DIM)), "kn_b": gain((DEPTH, N_BRANCH, HEAD_DIM)),
        "cmp_pos": nrm((DEPTH, 2, CMP_LEN, HEAD_DIM), 0.1),
        "cmp_w1": nrm((DEPTH, 2, CMP_LEN * HEAD_DIM, HEAD_DIM), (CMP_LEN * HEAD_DIM) ** -0.5),
        "cmp_w2": nrm((DEPTH, 2, HEAD_DIM, HEAD_DIM), HEAD_DIM ** -0.5),
        "qn_c": gain((DEPTH, HEAD_DIM)), "kn_c": gain((DEPTH, HEAD_DIM)),
        "wb_a": nrm((DEPTH, H_A * HEAD_DIM, D_MODEL), (H_A * HEAD_DIM) ** -0.5),
        "wb_b": nrm((DEPTH, H_B * HEAD_DIM, D_MODEL), (H_B * HEAD_DIM) ** -0.5),
        "wb_c": nrm((DEPTH, H_C * HEAD_DIM, D_MODEL), (H_C * HEAD_DIM) ** -0.5),
        "w_gate": nrm((DEPTH, D_MODEL, N_BRANCH * D_MODEL), sd),
        "b_gate": nrm((DEPTH, N_BRANCH * D_MODEL), 0.02),
        "w_out": nrm((DEPTH, D_MODEL, D_MODEL), sd),
        "ffn2_norm": gain((DEPTH, D_MODEL)),
        "ffn2_wg": nrm((DEPTH, D_MODEL, D_FF), sd),
        "ffn2_wu": nrm((DEPTH, D_MODEL, D_FF), sd),
        "ffn2_wd": nrm((DEPTH, D_FF, D_MODEL), D_FF ** -0.5),
        "ple_norm": gain((DEPTH, D_MODEL)),
        "ple_wg": nrm((DEPTH, D_MODEL, D_MODEL), sd),
        "ple_wp": nrm((DEPTH, D_PLE, D_MODEL), D_PLE ** -0.5),
    }


def reference(x_prompt, x_sample, cache_a_kv, cache_a_kidx, cache_b_cmp_kv, cache_b_slc_kv, state_b_win_kv,
              cache_c_kv, page_table, p_prompt, p_sample, rel_bias, ffn1_norm, ffn1_wg, ffn1_wu, ffn1_wd,
              mix_norm, w_in, qn_a, kn_a, qn_b, kn_b, cmp_pos, cmp_w1, cmp_w2, qn_c, kn_c, wb_a, wb_b, wb_c,
              w_gate, b_gate, w_out, ffn2_norm, ffn2_wg, ffn2_wu, ffn2_wd, ple_norm, ple_wg, ple_wp):
    past_len = page_table.shape[1] * PAGE_SIZE

    def paged(c):
        g = c[page_table]
        return g.reshape((g.shape[0], g.shape[1] * g.shape[2]) + g.shape[3:])

    y_prompt, y_sample = x_prompt, x_sample
    sp_list, ss_list = [], []
    for i in range(DEPTH):
        lw = (ffn1_norm[i], ffn1_wg[i], ffn1_wu[i], ffn1_wd[i], mix_norm[i], w_in[i], qn_a[i], kn_a[i],
              qn_b[i], kn_b[i], cmp_pos[i], cmp_w1[i], cmp_w2[i], qn_c[i], kn_c[i], wb_a[i], wb_b[i], wb_c[i],
              w_gate[i], b_gate[i], w_out[i], ffn2_norm[i], ffn2_wg[i], ffn2_wu[i], ffn2_wd[i],
              ple_norm[i], ple_wg[i], ple_wp[i])
        y_prompt, sp = _layer(y_prompt, p_prompt[i], None, 0, lw, rel_bias)
        past = (paged(cache_a_kv[i]), paged(cache_a_kidx[i]), paged(cache_b_cmp_kv[i]),
                paged(cache_b_slc_kv[i]), state_b_win_kv[i], paged(cache_c_kv[i]))
        y_sample, ss = _layer(y_sample, p_sample[i], past, past_len, lw, rel_bias)
        sp_list.append(sp)
        ss_list.append(ss)

    def stacked(lst, j):
        return jnp.stack([s[j] for s in lst])

    a_kv_p = stacked(sp_list, 0)
    a_kv_s = stacked(ss_list, 0)
    a_kidx_p = stacked(sp_list, 1)
    a_kidx_s = stacked(ss_list, 1)
    b_cmp_p = stacked(sp_list, 2)
    b_cmp_s = stacked(ss_list, 2)
    b_slc_p = stacked(sp_list, 3)
    b_slc_s = stacked(ss_list, 3)
    b_win_p = stacked(sp_list, 4)
    b_win_s = stacked(ss_list, 4)
    c_kv_p = stacked(sp_list, 5)
    c_kv_s = stacked(ss_list, 5)
    return (y_prompt, y_sample, a_kv_p, a_kv_s, a_kidx_p, a_kidx_s, b_cmp_p, b_cmp_s,
            b_slc_p, b_slc_s, b_win_p, b_win_s, c_kv_p, c_kv_s)
```

```python
import functools
import math

import numpy as np
import jax
import jax.numpy as jnp
from jax import lax
from jax.experimental import pallas as pl
from jax.experimental.pallas import tpu as pltpu

PAGE_SIZE = 128
HEAD_DIM = 128
H_A = 8
KV_A = 2
H_IDX = 16
D_IDX = 128
TOPK_A = 256
H_B = 8
KV_B = 2
CMP_LEN = 32
CMP_STRIDE = 16
SLC_LEN = 64
N_SLC = 16
WINDOW = 512
H_C = 8
MOBA_BLOCK = 256
MOBA_TOPK = 3
N_BUCKETS = 32
MAX_DISTANCE = 128
N_BRANCH = 3
QBLK = 32
EPS = 1e-6
NEG = -1e30

LANE = 128
BF16_SUBLANE = 16
V7X_VMEM_LIMIT = 56 * 1024 * 1024

F32 = jnp.float32
BF16 = jnp.bfloat16


def _tile(n, target, align):
    best = None
    for t in range(align, min(n, target) + 1, align):
        if n % t == 0:
            best = t
    return best if best is not None else n


def _round_up(n, m):
    return -(-n // m) * m


def _params(*sem):
    return pltpu.CompilerParams(dimension_semantics=sem, vmem_limit_bytes=V7X_VMEM_LIMIT)


def _rmsnorm_kernel(x_ref, g_ref, o_ref):
    x = x_ref[...]
    y = x * lax.rsqrt(jnp.mean(x * x, axis=-1, keepdims=True) + EPS)
    o_ref[...] = (y * g_ref[...]).astype(o_ref.dtype)


def rmsnorm_bf16(x, g):
    M, D = x.shape
    tm = _tile(M, 256, BF16_SUBLANE)
    return pl.pallas_call(
        _rmsnorm_kernel,
        grid=(M // tm,),
        in_specs=[pl.BlockSpec((tm, D), lambda i: (i, 0)), pl.BlockSpec((1, D), lambda i: (0, 0))],
        out_specs=pl.BlockSpec((tm, D), lambda i: (i, 0)),
        out_shape=jax.ShapeDtypeStruct((M, D), BF16),
        compiler_params=_params("parallel"),
    )(x, g.reshape(1, D))


def _ffn_up_kernel(a_ref, wg_ref, wu_ref, o_ref):
    a = a_ref[...]
    g = jnp.dot(a, wg_ref[...], preferred_element_type=F32)
    u = jnp.dot(a, wu_ref[...], preferred_element_type=F32)
    o_ref[...] = (g * jax.nn.sigmoid(g) * u).astype(o_ref.dtype)


def ffn_up(a, wg, wu):
    M, D = a.shape
    F = wg.shape[1]
    tm = _tile(M, 1024, BF16_SUBLANE)
    tn = _tile(F, 512, LANE)
    return pl.pallas_call(
        _ffn_up_kernel,
        grid=(M // tm, F // tn),
        in_specs=[pl.BlockSpec((tm, D), lambda i, j: (i, 0)),
                  pl.BlockSpec((D, tn), lambda i, j: (0, j)),
                  pl.BlockSpec((D, tn), lambda i, j: (0, j))],
        out_specs=pl.BlockSpec((tm, tn), lambda i, j: (i, j)),
        out_shape=jax.ShapeDtypeStruct((M, F), BF16),
        compiler_params=_params("parallel", "arbitrary"),
    )(a, wg, wu)


def _ffn_down_kernel(a_ref, w_ref, x_ref, o_ref, acc_ref):
    k = pl.program_id(2)

    @pl.when(k == 0)
    def _():
        acc_ref[...] = jnp.zeros_like(acc_ref)

    acc_ref[...] += jnp.dot(a_ref[...], w_ref[...], preferred_element_type=F32)

    @pl.when(k == pl.num_programs(2) - 1)
    def _():
        o_ref[...] = x_ref[...] + 0.5 * acc_ref[...]


def ffn_down(a, w, x):
    M, F = a.shape
    D = w.shape[1]
    tm = _tile(M, 1024, BF16_SUBLANE)
    tn = _tile(D, 1024, LANE)
    tk = _tile(F, 2816, LANE)
    return pl.pallas_call(
        _ffn_down_kernel,
        grid=(M // tm, D // tn, F // tk),
        in_specs=[pl.BlockSpec((tm, tk), lambda i, j, k: (i, k)),
                  pl.BlockSpec((tk, tn), lambda i, j, k: (k, j)),
                  pl.BlockSpec((tm, tn), lambda i, j, k: (i, j))],
        out_specs=pl.BlockSpec((tm, tn), lambda i, j, k: (i, j)),
        out_shape=jax.ShapeDtypeStruct((M, D), F32),
        scratch_shapes=[pltpu.VMEM((tm, tn), F32)],
        compiler_params=_params("parallel", "arbitrary", "arbitrary"),
    )(a, w, x)


def _w_in_kernel(flags_ref, a_ref, w_ref, g_ref, o_ref, *, heads_per_tile):
    j = pl.program_id(1)
    acc = jnp.dot(a_ref[...], w_ref[...], preferred_element_type=F32)
    for c in range(heads_per_tile):
        cols = slice(c * HEAD_DIM, (c + 1) * HEAD_DIM)
        y = acc[:, cols]
        yn = y * lax.rsqrt(jnp.mean(y * y, axis=-1, keepdims=True) + EPS) * g_ref[:, cols]
        o_ref[:, cols] = jnp.where(flags_ref[j * heads_per_tile + c] > 0, yn, y)


def in_proj(a, w, gains, flags):
    M, D = a.shape
    N = w.shape[1]
    tm = _tile(M, 1024, BF16_SUBLANE)
    tn = _tile(N, 512, LANE)
    grid_spec = pltpu.PrefetchScalarGridSpec(
        num_scalar_prefetch=1,
        grid=(M // tm, N // tn),
        in_specs=[pl.BlockSpec((tm, D), lambda i, j, f: (i, 0)),
                  pl.BlockSpec((D, tn), lambda i, j, f: (0, j)),
                  pl.BlockSpec((1, tn), lambda i, j, f: (0, j))],
        out_specs=pl.BlockSpec((tm, tn), lambda i, j, f: (i, j)),
    )
    return pl.pallas_call(
        functools.partial(_w_in_kernel, heads_per_tile=tn // HEAD_DIM),
        grid_spec=grid_spec,
        out_shape=jax.ShapeDtypeStruct((M, N), F32),
        compiler_params=_params("parallel", "arbitrary"),
    )(flags, a, w, gains)


def _gate_merge_kernel(u_ref, oa_ref, ob_ref, oc_ref, wg0_ref, wg1_ref, wg2_ref, bg0_ref, bg1_ref, bg2_ref,
                       wba_ref, wbb_ref, wbc_ref, o_ref):
    u = u_ref[...]
    out = None
    for o_r, wg_r, bg_r, wb_r in ((oa_ref, wg0_ref, bg0_ref, wba_ref),
                                  (ob_ref, wg1_ref, bg1_ref, wbb_ref),
                                  (oc_ref, wg2_ref, bg2_ref, wbc_ref)):
        gate = jax.nn.sigmoid(jnp.dot(u, wg_r[...], preferred_element_type=F32) + bg_r[...])
        branch = jnp.dot(o_r[...], wb_r[...], preferred_element_type=F32)
        out = gate * branch if out is None else out + gate * branch
    o_ref[...] = out.astype(o_ref.dtype)


def gate_merge(u, o_a, o_b, o_c, w_gate, b_gate, wb_a, wb_b, wb_c):
    M, D = u.shape
    Ho = o_a.shape[1]
    tm = _tile(M, 1024, BF16_SUBLANE)
    tn = _tile(D, 256, LANE)
    nj = D // tn
    a_spec = pl.BlockSpec((tm, D), lambda i, j: (i, 0))
    o_spec = pl.BlockSpec((tm, Ho), lambda i, j: (i, 0))
    wb_spec = pl.BlockSpec((Ho, tn), lambda i, j: (0, j))

    def branch_spec(rows, br):
        return pl.BlockSpec((rows, tn), lambda i, j: (0, br * nj + j))

    return pl.pallas_call(
        _gate_merge_kernel,
        grid=(M // tm, nj),
        in_specs=[a_spec, o_spec, o_spec, o_spec,
                  branch_spec(D, 0), branch_spec(D, 1), branch_spec(D, 2),
                  branch_spec(1, 0), branch_spec(1, 1), branch_spec(1, 2),
                  wb_spec, wb_spec, wb_spec],
        out_specs=pl.BlockSpec((tm, tn), lambda i, j: (i, j)),
        out_shape=jax.ShapeDtypeStruct((M, D), BF16),
        compiler_params=_params("parallel", "arbitrary"),
    )(u, o_a, o_b, o_c, w_gate, w_gate, w_gate, b_gate, b_gate, b_gate, wb_a, wb_b, wb_c)


def _out_proj_kernel(a_ref, w_ref, x_ref, o_ref):
    o_ref[...] = x_ref[...] + jnp.dot(a_ref[...], w_ref[...], preferred_element_type=F32)


def out_proj(a, w, x):
    M, K = a.shape
    N = w.shape[1]
    tm = _tile(M, 1024, BF16_SUBLANE)
    tn = _tile(N, 512, LANE)
    return pl.pallas_call(
        _out_proj_kernel,
        grid=(M // tm, N // tn),
        in_specs=[pl.BlockSpec((tm, K), lambda i, j: (i, 0)),
                  pl.BlockSpec((K, tn), lambda i, j: (0, j)),
                  pl.BlockSpec((tm, tn), lambda i, j: (i, j))],
        out_specs=pl.BlockSpec((tm, tn), lambda i, j: (i, j)),
        out_shape=jax.ShapeDtypeStruct((M, N), F32),
        compiler_params=_params("parallel", "arbitrary"),
    )(a, w, x)


def _ple_kernel(a_ref, wg_ref, p_ref, wp_ref, x_ref, o_ref):
    gate = jax.nn.sigmoid(jnp.dot(a_ref[...], wg_ref[...], preferred_element_type=F32))
    emb = jnp.dot(p_ref[...], wp_ref[...], preferred_element_type=F32)
    o_ref[...] = x_ref[...] + gate * emb


def ple(a, wg, p, wp, x):
    M, K = a.shape
    N = wg.shape[1]
    P = p.shape[1]
    tm = _tile(M, 1024, BF16_SUBLANE)
    tn = _tile(N, 512, LANE)
    return pl.pallas_call(
        _ple_kernel,
        grid=(M // tm, N // tn),
        in_specs=[pl.BlockSpec((tm, K), lambda i, j: (i, 0)),
                  pl.BlockSpec((K, tn), lambda i, j: (0, j)),
                  pl.BlockSpec((tm, P), lambda i, j: (i, 0)),
                  pl.BlockSpec((P, tn), lambda i, j: (0, j)),
                  pl.BlockSpec((tm, tn), lambda i, j: (i, j))],
        out_specs=pl.BlockSpec((tm, tn), lambda i, j: (i, j)),
        out_shape=jax.ShapeDtypeStruct((M, N), F32),
        compiler_params=_params("parallel", "arbitrary"),
    )(a, wg, p, wp, x)


def _rms(x, g):
    xf = x.astype(jnp.float32)
    y = xf * lax.rsqrt(jnp.mean(xf * xf, axis=-1, keepdims=True) + EPS)
    return (y * g.astype(jnp.float32)).astype(x.dtype)


def _t5_bucket(n):
    n = jnp.maximum(n, 0)
    exact = N_BUCKETS // 2
    nf = jnp.maximum(n, 1).astype(jnp.float32)
    big = exact + (jnp.log(nf / exact) * ((N_BUCKETS - exact) / math.log(MAX_DISTANCE / exact))).astype(jnp.int32)
    return jnp.where(n < exact, n, jnp.minimum(big, N_BUCKETS - 1))


def _masked_softmax(logits, mask):
    s = jnp.where(mask, logits.astype(jnp.float32), NEG)
    e = jnp.exp(s - jnp.max(s, axis=-1, keepdims=True)) * mask
    return e / jnp.maximum(jnp.sum(e, axis=-1, keepdims=True), 1e-30)


def _query_blocks(fn, qpos, *xs):
    T = qpos.shape[0]
    qb = QBLK if T % QBLK == 0 else T
    nb = T // qb
    blocked = tuple(jnp.swapaxes(x.reshape((x.shape[0], nb, qb) + x.shape[2:]), 0, 1) for x in xs)
    out = lax.map(lambda a: fn(a[0], a[1], *a[2:]), (jnp.arange(nb), qpos.reshape(nb, qb)) + blocked)
    out = jnp.swapaxes(out, 0, 1)
    return out.reshape((out.shape[0], T) + out.shape[3:])


def _mixer_a(q, k, v, q_idx, w_idx, k_idx, qpos, tab):
    b, L = k.shape[0], k.shape[1]
    topk = min(TOPK_A, L // 4)
    rep = H_A // KV_A
    kpos = jnp.arange(L)
    bi = jnp.arange(b)[:, None, None]

    def blk(i, pos, qq, qi, wi):
        qbn = pos.shape[0]
        s = jnp.einsum('bqhd,bsd->bqhs', qi, k_idx).astype(jnp.float32) * (D_IDX ** -0.5)
        score = jnp.einsum('bqhs,bqh->bqs', jax.nn.relu(s), wi.astype(jnp.float32)) * (H_IDX ** -0.5)
        causal = kpos[None, None, :] <= pos[None, :, None]
        _, idx = lax.top_k(jnp.where(causal, score, NEG), topk)
        valid = idx <= pos[None, :, None]
        kg = k[bi, idx]
        vg = v[bi, idx]
        qg = qq.reshape(b, qbn, KV_A, rep, HEAD_DIM)
        logits = jnp.einsum('bqgrd,bqkgd->bqgrk', qg, kg) * (HEAD_DIM ** -0.5)
        bias = jnp.moveaxis(tab[_t5_bucket(pos[None, :, None] - idx)], 2, -1).reshape(b, qbn, KV_A, rep, topk)
        p = _masked_softmax(logits + bias, valid[:, :, None, None, :])
        o = jnp.einsum('bqgrk,bqkgd->bqgrd', p.astype(vg.dtype), vg)
        return o.reshape(b, qbn, H_A * HEAD_DIM)

    return _query_blocks(blk, qpos, q, q_idx, w_idx)


def _compress(x, pos_emb, w1, w2):
    b, L, g, d = x.shape
    n_sub = L // CMP_STRIDE
    r = CMP_LEN // CMP_STRIDE
    n_cmp = n_sub - r + 1
    sub = x[:, :n_sub * CMP_STRIDE].reshape(b, n_sub, CMP_STRIDE, g, d)
    blocks = jnp.concatenate([sub[:, j:j + n_cmp] for j in range(r)], axis=2) + pos_emb[:, None, :]
    flat = jnp.moveaxis(blocks, 3, 2).reshape(b, n_cmp, g, CMP_LEN * d)
    return jax.nn.gelu(flat @ w1) @ w2


def _mixer_b(q, gates, kc, vc, ks, vs, kw, vw, kw_pos, qpos, cmp_pos, cmp_w1, cmp_w2, kn_cmp, tab):
    b, L = ks.shape[0], ks.shape[1]
    rep = H_B // KV_B
    scale = HEAD_DIM ** -0.5
    kcmp = _rms(_compress(kc, cmp_pos[0], cmp_w1[0], cmp_w2[0]), kn_cmp)
    vcmp = _compress(vc, cmp_pos[1], cmp_w1[1], cmp_w2[1])
    n_cmp = kcmp.shape[1]
    cend = jnp.arange(n_cmp) * CMP_STRIDE + CMP_LEN - 1
    cstart = cend - (CMP_LEN - 1)
    n_slc = -(-L // SLC_LEN)
    nsel = min(N_SLC, n_slc)
    sstart = jnp.arange(n_slc) * SLC_LEN
    overlap = ((cstart[:, None] < sstart[None, :] + SLC_LEN) & (cend[:, None] >= sstart[None, :])).astype(jnp.float32)
    pad = n_slc * SLC_LEN - L
    ks_blk = jnp.moveaxis(jnp.pad(ks, ((0, 0), (0, pad), (0, 0), (0, 0))).reshape(b, n_slc, SLC_LEN, KV_B, HEAD_DIM), 3, 1)
    vs_blk = jnp.moveaxis(jnp.pad(vs, ((0, 0), (0, pad), (0, 0), (0, 0))).reshape(b, n_slc, SLC_LEN, KV_B, HEAD_DIM), 3, 1)
    n_buf = kw.shape[1] - qpos.shape[0]
    kw_pad = jnp.pad(kw, ((0, 0), (WINDOW, 0), (0, 0), (0, 0)))
    vw_pad = jnp.pad(vw, ((0, 0), (WINDOW, 0), (0, 0), (0, 0)))
    kwp_pad = jnp.concatenate([jnp.full((WINDOW,), -WINDOW, jnp.int32), kw_pos.astype(jnp.int32)])
    tab3 = tab.reshape(N_BUCKETS, KV_B, rep)
    bi = jnp.arange(b)[:, None, None, None]
    gi = jnp.arange(KV_B)[None, None, :, None]
    jj = jnp.arange(n_slc)

    def head_bias(dist):
        bq = tab[_t5_bucket(dist)].reshape(dist.shape + (KV_B, rep))
        return jnp.transpose(bq, (0, 2, 3, 1))[None]

    def blk(i, pos, qq, gg):
        qbn = pos.shape[0]
        qg = qq.reshape(b, qbn, KV_B, rep, HEAD_DIM)
        lc = jnp.einsum('bqgrd,bngd->bqgrn', qg, kcmp) * scale + head_bias(pos[:, None] - cend[None, :])
        pc = _masked_softmax(lc, (cend[None, :] <= pos[:, None])[None, :, None, None, :])
        oc = jnp.einsum('bqgrn,bngd->bqgrd', pc.astype(vcmp.dtype), vcmp)
        imp = jnp.einsum('bqgrn,ns->bqgs', pc, overlap)
        cur = (pos // SLC_LEN)[None, :, None, None]
        forced = (jj == 0) | (jj == cur) | (jj == cur - 1)
        imp = jnp.where(jj > cur, NEG, jnp.where(forced, -NEG, imp))
        _, sidx = lax.top_k(imp, nsel)
        tok = (sidx[..., None] * SLC_LEN + jnp.arange(SLC_LEN)).reshape(b, qbn, KV_B, nsel * SLC_LEN)
        kg = ks_blk[bi, gi, sidx].reshape(b, qbn, KV_B, nsel * SLC_LEN, HEAD_DIM)
        vg = vs_blk[bi, gi, sidx].reshape(b, qbn, KV_B, nsel * SLC_LEN, HEAD_DIM)
        bs = jnp.moveaxis(tab3[_t5_bucket(pos[None, :, None, None] - tok), gi], -1, 3)
        ls = jnp.einsum('bqgrd,bqgkd->bqgrk', qg, kg) * scale + bs
        ps = _masked_softmax(ls, (tok <= pos[None, :, None, None])[:, :, :, None, :])
        o_s = jnp.einsum('bqgrk,bqgkd->bqgrd', ps.astype(vg.dtype), vg)
        start = n_buf + i * qbn
        kwb = lax.dynamic_slice_in_dim(kw_pad, start, WINDOW + qbn, axis=1)
        vwb = lax.dynamic_slice_in_dim(vw_pad, start, WINDOW + qbn, axis=1)
        dist = pos[:, None] - lax.dynamic_slice_in_dim(kwp_pad, start, WINDOW + qbn)[None, :]
        lw = jnp.einsum('bqgrd,bkgd->bqgrk', qg, kwb) * scale + head_bias(dist)
        pw = _masked_softmax(lw, ((dist >= 0) & (dist < WINDOW))[None, :, None, None, :])
        o_w = jnp.einsum('bqgrk,bkgd->bqgrd', pw.astype(vwb.dtype), vwb)
        g5 = gg.reshape(b, qbn, KV_B, rep, N_BRANCH)
        o = g5[..., 0:1] * oc + g5[..., 1:2] * o_s + g5[..., 2:3] * o_w
        return o.reshape(b, qbn, H_B * HEAD_DIM)

    return _query_blocks(blk, qpos, q, gates)


def _mixer_c(q, k, v, qpos, tab):
    b, L = k.shape[0], k.shape[1]
    nblk = L // MOBA_BLOCK
    ksel = min(MOBA_TOPK, nblk)
    scale = HEAD_DIM ** -0.5
    kb = jnp.moveaxis(k[:, :nblk * MOBA_BLOCK].reshape(b, nblk, MOBA_BLOCK, H_C, HEAD_DIM), 3, 1)
    vb = jnp.moveaxis(v[:, :nblk * MOBA_BLOCK].reshape(b, nblk, MOBA_BLOCK, H_C, HEAD_DIM), 3, 1)
    kmean = jnp.mean(kb.astype(jnp.float32), axis=3)
    k_pad = jnp.pad(k, ((0, 0), (0, MOBA_BLOCK), (0, 0), (0, 0)))
    v_pad = jnp.pad(v, ((0, 0), (0, MOBA_BLOCK), (0, 0), (0, 0)))
    bi = jnp.arange(b)[:, None, None, None]
    hi = jnp.arange(H_C)[None, None, :, None]

    def blk(i, pos, qq):
        qbn = pos.shape[0]
        cur = pos // MOBA_BLOCK
        own = (pos[0] // MOBA_BLOCK) * MOBA_BLOCK
        ko = lax.dynamic_slice_in_dim(k_pad, own, MOBA_BLOCK, axis=1)
        vo = lax.dynamic_slice_in_dim(v_pad, own, MOBA_BLOCK, axis=1)
        dist_o = pos[:, None] - (own + jnp.arange(MOBA_BLOCK))[None, :]
        lo = jnp.einsum('bqhd,bkhd->bqhk', qq, ko) * scale + jnp.transpose(tab[_t5_bucket(dist_o)], (0, 2, 1))[None]
        mo = jnp.broadcast_to((dist_o >= 0)[None, :, None, :], lo.shape)
        gs = jnp.einsum('bqhd,bhnd->bqhn', qq.astype(jnp.float32), kmean)
        past = jnp.arange(nblk)[None, :] < cur[:, None]
        _, bidx = lax.top_k(jnp.where(past[None, :, None, :], gs, NEG), ksel)
        nk = ksel * MOBA_BLOCK
        kg = kb[bi, hi, bidx].reshape(b, qbn, H_C, nk, HEAD_DIM)
        vg = vb[bi, hi, bidx].reshape(b, qbn, H_C, nk, HEAD_DIM)
        tok = (bidx[..., None] * MOBA_BLOCK + jnp.arange(MOBA_BLOCK)).reshape(b, qbn, H_C, nk)
        ms = jnp.broadcast_to((bidx < cur[None, :, None, None])[..., None], bidx.shape + (MOBA_BLOCK,)).reshape(b, qbn, H_C, nk)
        ls = jnp.einsum('bqhd,bqhkd->bqhk', qq, kg) * scale + tab[_t5_bucket(pos[None, :, None, None] - tok), hi]
        p = _masked_softmax(jnp.concatenate([ls, lo], axis=-1), jnp.concatenate([ms, mo], axis=-1))
        o = (jnp.einsum('bqhk,bqhkd->bqhd', p[..., :nk].astype(vg.dtype), vg)
             + jnp.einsum('bqhk,bkhd->bqhd', p[..., nk:].astype(vo.dtype), vo))
        return o.reshape(b, qbn, H_C * HEAD_DIM)

    return _query_blocks(blk, qpos, q)


_SEG = {}
_off = 0
for _name, _w in (("qa", H_A * HEAD_DIM), ("kva", 2 * KV_A * HEAD_DIM), ("qi", H_IDX * D_IDX), ("ki", D_IDX),
                  ("qb", H_B * HEAD_DIM), ("cmp", 2 * KV_B * HEAD_DIM), ("slc", 2 * KV_B * HEAD_DIM),
                  ("win", 2 * KV_B * HEAD_DIM), ("qc", H_C * HEAD_DIM), ("kvc", 2 * H_C * HEAD_DIM), ("misc", LANE)):
    _SEG[_name] = (_off, _off + _w)
    _off += _w
IN_COLS_ALIGNED = _off


def _prep_in_proj(w_in, qn_a, kn_a, qn_b, kn_b, qn_c, kn_c):
    sizes = (H_A * HEAD_DIM, KV_A * HEAD_DIM, KV_A * HEAD_DIM, H_IDX * D_IDX, H_IDX, D_IDX,
             H_B * HEAD_DIM, N_BRANCH * H_B) + (KV_B * HEAD_DIM,) * 6 + (H_C * HEAD_DIM,) * 3
    splits = np.cumsum(sizes)[:-1].tolist()
    (qa, ka, va, qi, wi, ki, q_b, g_b, kbc, vbc, kbs, vbs, kbw, vbw, qc, kc, vc) = jnp.split(w_in, splits, axis=-1)
    D = w_in.shape[0]
    misc = jnp.concatenate([wi, g_b, jnp.zeros((D, LANE - H_IDX - N_BRANCH * H_B), w_in.dtype)], axis=-1)
    n_pad = _round_up(IN_COLS_ALIGNED, 512) - IN_COLS_ALIGNED
    w = jnp.concatenate([qa, ka, va, qi, ki, q_b, kbc, vbc, kbs, vbs, kbw, vbw, qc, kc, vc, misc,
                         jnp.zeros((D, n_pad), w_in.dtype)], axis=-1).astype(BF16)
    one = jnp.ones((HEAD_DIM,), F32)

    def rep(g, n):
        return jnp.tile(g.astype(F32), n)

    gains = jnp.concatenate([
        rep(qn_a, H_A), rep(kn_a, KV_A), rep(one, KV_A), rep(one, H_IDX), one,
        rep(qn_b, H_B), rep(one, 2 * KV_B), rep(kn_b[1], KV_B), rep(one, KV_B), rep(kn_b[2], KV_B), rep(one, KV_B),
        rep(qn_c, H_C), rep(kn_c, H_C), rep(one, H_C), one, jnp.ones((n_pad,), F32)]).reshape(1, -1)
    flags = np.concatenate([
        np.ones(H_A), np.ones(KV_A), np.zeros(KV_A), np.zeros(H_IDX), np.zeros(1),
        np.ones(H_B), np.zeros(2 * KV_B), np.ones(KV_B), np.zeros(KV_B), np.ones(KV_B), np.zeros(KV_B),
        np.ones(H_C), np.ones(H_C), np.zeros(H_C), np.zeros(1), np.zeros(n_pad // LANE)]).astype(np.int32)
    return w, gains, jnp.asarray(flags)


def _prep_layer(i, ffn1_norm, ffn1_wg, ffn1_wu, ffn1_wd, mix_norm, w_in, qn_a, kn_a, qn_b, kn_b, cmp_pos, cmp_w1,
                cmp_w2, qn_c, kn_c, wb_a, wb_b, wb_c, w_gate, b_gate, w_out, ffn2_norm, ffn2_wg, ffn2_wu, ffn2_wd,
                ple_norm, ple_wg, ple_wp):
    F = ffn1_wg.shape[2]
    Fp = _round_up(F, 512) if F > 512 else F

    def up(w):
        return jnp.pad(w[i], ((0, 0), (0, Fp - F))).astype(BF16)

    def down(w):
        return jnp.pad(w[i], ((0, Fp - F), (0, 0))).astype(BF16)

    w_in_p, gains, flags = _prep_in_proj(w_in[i], qn_a[i], kn_a[i], qn_b[i], kn_b[i], qn_c[i], kn_c[i])
    return dict(
        f1n=ffn1_norm[i], f1g=up(ffn1_wg), f1u=up(ffn1_wu), f1d=down(ffn1_wd), mn=mix_norm[i],
        w_in=w_in_p, gains=gains, flags=flags, kn_cmp=kn_b[i, 0],
        cmp_pos=cmp_pos[i], cmp_w1=cmp_w1[i], cmp_w2=cmp_w2[i],
        wb_a=wb_a[i].astype(BF16), wb_b=wb_b[i].astype(BF16), wb_c=wb_c[i].astype(BF16),
        w_gate=w_gate[i].astype(BF16), b_gate=b_gate[i].reshape(1, -1), w_out=w_out[i].astype(BF16),
        f2n=ffn2_norm[i], f2g=up(ffn2_wg), f2u=up(ffn2_wu), f2d=down(ffn2_wd),
        pn=ple_norm[i], pwg=ple_wg[i].astype(BF16), pwp=ple_wp[i].astype(BF16))


def _layer(x, p, b, T, past, past_len, W, rel_bias):
    M = b * T
    h = ffn_down(ffn_up(rmsnorm_bf16(x, W["f1n"]), W["f1g"], W["f1u"]), W["f1d"], x)
    u = rmsnorm_bf16(h, W["mn"])
    proj = in_proj(u, W["w_in"], W["gains"], W["flags"])

    def seg(name, heads):
        lo, hi = _SEG[name]
        return proj[:, lo:hi].reshape(b, T, heads, -1)

    qa = seg("qa", H_A)
    kva = seg("kva", 2 * KV_A)
    ka, va = kva[:, :, :KV_A], kva[:, :, KV_A:]
    qi = seg("qi", H_IDX)
    ki = seg("ki", 1)[:, :, 0]
    q_b = seg("qb", H_B)
    cmp = seg("cmp", 2 * KV_B)
    slc = seg("slc", 2 * KV_B)
    win = seg("win", 2 * KV_B)
    qc = seg("qc", H_C)
    kvc = seg("kvc", 2 * H_C)
    kc, vc = kvc[:, :, :H_C], kvc[:, :, H_C:]
    m0 = _SEG["misc"][0]
    wi = proj[:, m0:m0 + H_IDX].reshape(b, T, H_IDX)
    g_b = jax.nn.sigmoid(proj[:, m0 + H_IDX:m0 + H_IDX + N_BRANCH * H_B].reshape(b, T, H_B, N_BRANCH))
    kbc, vbc = cmp[:, :, :KV_B], cmp[:, :, KV_B:]
    kbs, vbs = slc[:, :, :KV_B], slc[:, :, KV_B:]
    win_new = win.reshape(b, T, 2, KV_B, HEAD_DIM)

    if past is None:
        def full(j, k_new, v_new):
            return k_new, v_new
        ki_all = ki
        win_all = win_new
    else:
        def full(j, k_new, v_new):
            return (jnp.concatenate([past[j][:, :, 0], k_new], axis=1),
                    jnp.concatenate([past[j][:, :, 1], v_new], axis=1))
        ki_all = jnp.concatenate([past[1], ki], axis=1)
        win_all = jnp.concatenate([past[4], win_new], axis=1)
    qpos = past_len + jnp.arange(T)
    n_buf = win_all.shape[1] - T
    kw_pos = past_len - n_buf + jnp.arange(win_all.shape[1])
    tab_a = rel_bias[:, :H_A]
    tab_b = rel_bias[:, H_A:H_A + H_B]
    tab_c = rel_bias[:, H_A + H_B:]
    ka_all, va_all = full(0, ka, va)
    kbc_all, vbc_all = full(2, kbc, vbc)
    kbs_all, vbs_all = full(3, kbs, vbs)
    kc_all, vc_all = full(5, kc, vc)
    o_a = _mixer_a(qa, ka_all, va_all, qi, wi, ki_all, qpos, tab_a)
    o_b = _mixer_b(q_b, g_b, kbc_all, vbc_all, kbs_all, vbs_all, win_all[:, :, 0], win_all[:, :, 1],
                   kw_pos, qpos, W["cmp_pos"], W["cmp_w1"], W["cmp_w2"], W["kn_cmp"], tab_b)
    o_c = _mixer_c(qc, kc_all, vc_all, qpos, tab_c)

    merged = gate_merge(u, o_a.reshape(M, -1).astype(BF16), o_b.reshape(M, -1).astype(BF16),
                        o_c.reshape(M, -1).astype(BF16), W["w_gate"], W["b_gate"], W["wb_a"], W["wb_b"], W["wb_c"])
    h = out_proj(merged, W["w_out"], h)
    h = ffn_down(ffn_up(rmsnorm_bf16(h, W["f2n"]), W["f2g"], W["f2u"]), W["f2d"], h)
    h = ple(rmsnorm_bf16(h, W["pn"]), W["pwg"], p.astype(BF16), W["pwp"], h)
    state = (kva.reshape(b, T, 2, KV_A, HEAD_DIM), ki, cmp.reshape(b, T, 2, KV_B, HEAD_DIM),
             slc.reshape(b, T, 2, KV_B, HEAD_DIM), win_all[:, -WINDOW:], kvc.reshape(b, T, 2, H_C, HEAD_DIM))
    return h, state


def kernel(x_prompt, x_sample, cache_a_kv, cache_a_kidx, cache_b_cmp_kv, cache_b_slc_kv, state_b_win_kv, cache_c_kv, page_table, p_prompt, p_sample, rel_bias, ffn1_norm, ffn1_wg, ffn1_wu, ffn1_wd, mix_norm, w_in, qn_a, kn_a, qn_b, kn_b, cmp_pos, cmp_w1, cmp_w2, qn_c, kn_c, wb_a, wb_b, wb_c, w_gate, b_gate, w_out, ffn2_norm, ffn2_wg, ffn2_wu, ffn2_wd, ple_norm, ple_wg, ple_wp):
    depth = ffn1_norm.shape[0]
    bp, Tp, D = x_prompt.shape
    bs, Ts, _ = x_sample.shape
    past_len = page_table.shape[1] * PAGE_SIZE

    def paged(c):
        g = c[page_table]
        return g.reshape((g.shape[0], g.shape[1] * g.shape[2]) + g.shape[3:])

    y_p = x_prompt.reshape(bp * Tp, D)
    y_s = x_sample.reshape(bs * Ts, D)
    sp_list, ss_list = [], []
    for i in range(depth):
        W = _prep_layer(i, ffn1_norm, ffn1_wg, ffn1_wu, ffn1_wd, mix_norm, w_in, qn_a, kn_a, qn_b, kn_b, cmp_pos,
                        cmp_w1, cmp_w2, qn_c, kn_c, wb_a, wb_b, wb_c, w_gate, b_gate, w_out, ffn2_norm, ffn2_wg,
                        ffn2_wu, ffn2_wd, ple_norm, ple_wg, ple_wp)
        y_p, sp = _layer(y_p, p_prompt[i].reshape(bp * Tp, -1), bp, Tp, None, 0, W, rel_bias)
        past = (paged(cache_a_kv[i]), paged(cache_a_kidx[i]), paged(cache_b_cmp_kv[i]),
                paged(cache_b_slc_kv[i]), state_b_win_kv[i], paged(cache_c_kv[i]))
        y_s, ss = _layer(y_s, p_sample[i].reshape(bs * Ts, -1), bs, Ts, past, past_len, W, rel_bias)
        sp_list.append(sp)
        ss_list.append(ss)

    def stacked(lst, j):
        return jnp.stack([s[j] for s in lst])

    outs = [y_p.reshape(bp, Tp, D), y_s.reshape(bs, Ts, D)]
    for j in range(6):
        outs.append(stacked(sp_list, j))
        outs.append(stacked(ss_list, j))
    return tuple(outs)
```

```python
import functools
import math

import numpy as np
import jax
import jax.numpy as jnp
from jax import lax
from jax.experimental import pallas as pl
from jax.experimental.pallas import tpu as pltpu

PAGE_SIZE = 128
HEAD_DIM = 128
H_A = 8
KV_A = 2
H_IDX = 16
D_IDX = 128
TOPK_A = 256
H_B = 8
KV_B = 2
CMP_LEN = 32
CMP_STRIDE = 16
SLC_LEN = 64
N_SLC = 16
WINDOW = 512
H_C = 8
MOBA_BLOCK = 256
MOBA_TOPK = 3
N_BUCKETS = 32
MAX_DISTANCE = 128
N_BRANCH = 3
EPS = 1e-6
NEG = -1e30

LANE = 128
BF16_SUBLANE = 16
V7X_VMEM_LIMIT = 56 * 1024 * 1024

F32 = jnp.float32
BF16 = jnp.bfloat16


def _tile(n, target, align):
    best = None
    for t in range(align, min(n, target) + 1, align):
        if n % t == 0:
            best = t
    return best if best is not None else n


def _round_up(n, m):
    return -(-n // m) * m


def _params(*sem):
    return pltpu.CompilerParams(dimension_semantics=sem, vmem_limit_bytes=V7X_VMEM_LIMIT)


def _rmsnorm_kernel(x_ref, g_ref, o_ref):
    x = x_ref[...]
    y = x * lax.rsqrt(jnp.mean(x * x, axis=-1, keepdims=True) + EPS)
    o_ref[...] = (y * g_ref[...]).astype(o_ref.dtype)


def rmsnorm_bf16(x, g):
    M, D = x.shape
    tm = _tile(M, 256, BF16_SUBLANE)
    return pl.pallas_call(
        _rmsnorm_kernel,
        grid=(M // tm,),
        in_specs=[pl.BlockSpec((tm, D), lambda i: (i, 0)), pl.BlockSpec((1, D), lambda i: (0, 0))],
        out_specs=pl.BlockSpec((tm, D), lambda i: (i, 0)),
        out_shape=jax.ShapeDtypeStruct((M, D), BF16),
        compiler_params=_params("parallel"),
    )(x, g.reshape(1, D))


def _ffn_up_kernel(a_ref, wg_ref, wu_ref, o_ref):
    a = a_ref[...]
    g = jnp.dot(a, wg_ref[...], preferred_element_type=F32)
    u = jnp.dot(a, wu_ref[...], preferred_element_type=F32)
    o_ref[...] = (g * jax.nn.sigmoid(g) * u).astype(o_ref.dtype)


def ffn_up(a, wg, wu):
    M, D = a.shape
    F = wg.shape[1]
    tm = _tile(M, 1024, BF16_SUBLANE)
    tn = _tile(F, 512, LANE)
    return pl.pallas_call(
        _ffn_up_kernel,
        grid=(M // tm, F // tn),
        in_specs=[pl.BlockSpec((tm, D), lambda i, j: (i, 0)),
                  pl.BlockSpec((D, tn), lambda i, j: (0, j)),
                  pl.BlockSpec((D, tn), lambda i, j: (0, j))],
        out_specs=pl.BlockSpec((tm, tn), lambda i, j: (i, j)),
        out_shape=jax.ShapeDtypeStruct((M, F), BF16),
        compiler_params=_params("parallel", "arbitrary"),
    )(a, wg, wu)


def _ffn_down_kernel(a_ref, w_ref, x_ref, o_ref, acc_ref):
    k = pl.program_id(2)

    @pl.when(k == 0)
    def _():
        acc_ref[...] = jnp.zeros_like(acc_ref)

    acc_ref[...] += jnp.dot(a_ref[...], w_ref[...], preferred_element_type=F32)

    @pl.when(k == pl.num_programs(2) - 1)
    def _():
        o_ref[...] = x_ref[...] + 0.5 * acc_ref[...]


def ffn_down(a, w, x):
    M, F = a.shape
    D = w.shape[1]
    tm = _tile(M, 1024, BF16_SUBLANE)
    tn = _tile(D, 1024, LANE)
    tk = _tile(F, 2816, LANE)
    return pl.pallas_call(
        _ffn_down_kernel,
        grid=(M // tm, D // tn, F // tk),
        in_specs=[pl.BlockSpec((tm, tk), lambda i, j, k: (i, k)),
                  pl.BlockSpec((tk, tn), lambda i, j, k: (k, j)),
                  pl.BlockSpec((tm, tn), lambda i, j, k: (i, j))],
        out_specs=pl.BlockSpec((tm, tn), lambda i, j, k: (i, j)),
        out_shape=jax.ShapeDtypeStruct((M, D), F32),
        scratch_shapes=[pltpu.VMEM((tm, tn), F32)],
        compiler_params=_params("parallel", "arbitrary", "arbitrary"),
    )(a, w, x)


def _w_in_kernel(flags_ref, a_ref, w_ref, g_ref, o_ref, o16_ref, *, heads_per_tile):
    j = pl.program_id(1)
    acc = jnp.dot(a_ref[...], w_ref[...], preferred_element_type=F32)
    for c in range(heads_per_tile):
        cols = slice(c * HEAD_DIM, (c + 1) * HEAD_DIM)
        y = acc[:, cols]
        yn = y * lax.rsqrt(jnp.mean(y * y, axis=-1, keepdims=True) + EPS) * g_ref[:, cols]
        y = jnp.where(flags_ref[j * heads_per_tile + c] > 0, yn, y)
        o_ref[:, cols] = y
        o16_ref[:, cols] = y.astype(o16_ref.dtype)


def in_proj(a, w, gains, flags):
    M, D = a.shape
    N = w.shape[1]
    tm = _tile(M, 1024, BF16_SUBLANE)
    tn = _tile(N, 512, LANE)
    grid_spec = pltpu.PrefetchScalarGridSpec(
        num_scalar_prefetch=1,
        grid=(M // tm, N // tn),
        in_specs=[pl.BlockSpec((tm, D), lambda i, j, f: (i, 0)),
                  pl.BlockSpec((D, tn), lambda i, j, f: (0, j)),
                  pl.BlockSpec((1, tn), lambda i, j, f: (0, j))],
        out_specs=[pl.BlockSpec((tm, tn), lambda i, j, f: (i, j)), pl.BlockSpec((tm, tn), lambda i, j, f: (i, j))],
    )
    return pl.pallas_call(
        functools.partial(_w_in_kernel, heads_per_tile=tn // HEAD_DIM),
        grid_spec=grid_spec,
        out_shape=[jax.ShapeDtypeStruct((M, N), F32), jax.ShapeDtypeStruct((M, N), BF16)],
        compiler_params=_params("parallel", "arbitrary"),
    )(flags, a, w, gains)


def _gate_merge_kernel(u_ref, oa_ref, ob_ref, oc_ref, wg0_ref, wg1_ref, wg2_ref, bg0_ref, bg1_ref, bg2_ref,
                       wba_ref, wbb_ref, wbc_ref, o_ref):
    u = u_ref[...]
    out = None
    for o_r, wg_r, bg_r, wb_r in ((oa_ref, wg0_ref, bg0_ref, wba_ref),
                                  (ob_ref, wg1_ref, bg1_ref, wbb_ref),
                                  (oc_ref, wg2_ref, bg2_ref, wbc_ref)):
        gate = jax.nn.sigmoid(jnp.dot(u, wg_r[...], preferred_element_type=F32) + bg_r[...])
        branch = jnp.dot(o_r[...], wb_r[...], preferred_element_type=F32)
        out = gate * branch if out is None else out + gate * branch
    o_ref[...] = out.astype(o_ref.dtype)


def gate_merge(u, o_a, o_b, o_c, w_gate, b_gate, wb_a, wb_b, wb_c):
    M, D = u.shape
    Ho = o_a.shape[1]
    tm = _tile(M, 1024, BF16_SUBLANE)
    tn = _tile(D, 256, LANE)
    nj = D // tn
    a_spec = pl.BlockSpec((tm, D), lambda i, j: (i, 0))
    o_spec = pl.BlockSpec((tm, Ho), lambda i, j: (i, 0))
    wb_spec = pl.BlockSpec((Ho, tn), lambda i, j: (0, j))

    def branch_spec(rows, br):
        return pl.BlockSpec((rows, tn), lambda i, j: (0, br * nj + j))

    return pl.pallas_call(
        _gate_merge_kernel,
        grid=(M // tm, nj),
        in_specs=[a_spec, o_spec, o_spec, o_spec,
                  branch_spec(D, 0), branch_spec(D, 1), branch_spec(D, 2),
                  branch_spec(1, 0), branch_spec(1, 1), branch_spec(1, 2),
                  wb_spec, wb_spec, wb_spec],
        out_specs=pl.BlockSpec((tm, tn), lambda i, j: (i, j)),
        out_shape=jax.ShapeDtypeStruct((M, D), BF16),
        compiler_params=_params("parallel", "arbitrary"),
    )(u, o_a, o_b, o_c, w_gate, w_gate, w_gate, b_gate, b_gate, b_gate, wb_a, wb_b, wb_c)


def _out_proj_kernel(a_ref, w_ref, x_ref, o_ref):
    o_ref[...] = x_ref[...] + jnp.dot(a_ref[...], w_ref[...], preferred_element_type=F32)


def out_proj(a, w, x):
    M, K = a.shape
    N = w.shape[1]
    tm = _tile(M, 1024, BF16_SUBLANE)
    tn = _tile(N, 512, LANE)
    return pl.pallas_call(
        _out_proj_kernel,
        grid=(M // tm, N // tn),
        in_specs=[pl.BlockSpec((tm, K), lambda i, j: (i, 0)),
                  pl.BlockSpec((K, tn), lambda i, j: (0, j)),
                  pl.BlockSpec((tm, tn), lambda i, j: (i, j))],
        out_specs=pl.BlockSpec((tm, tn), lambda i, j: (i, j)),
        out_shape=jax.ShapeDtypeStruct((M, N), F32),
        compiler_params=_params("parallel", "arbitrary"),
    )(a, w, x)


def _ple_kernel(a_ref, wg_ref, p_ref, wp_ref, x_ref, o_ref):
    gate = jax.nn.sigmoid(jnp.dot(a_ref[...], wg_ref[...], preferred_element_type=F32))
    emb = jnp.dot(p_ref[...], wp_ref[...], preferred_element_type=F32)
    o_ref[...] = x_ref[...] + gate * emb


def ple(a, wg, p, wp, x):
    M, K = a.shape
    N = wg.shape[1]
    P = p.shape[1]
    tm = _tile(M, 1024, BF16_SUBLANE)
    tn = _tile(N, 512, LANE)
    return pl.pallas_call(
        _ple_kernel,
        grid=(M // tm, N // tn),
        in_specs=[pl.BlockSpec((tm, K), lambda i, j: (i, 0)),
                  pl.BlockSpec((K, tn), lambda i, j: (0, j)),
                  pl.BlockSpec((tm, P), lambda i, j: (i, 0)),
                  pl.BlockSpec((P, tn), lambda i, j: (0, j)),
                  pl.BlockSpec((tm, tn), lambda i, j: (i, j))],
        out_specs=pl.BlockSpec((tm, tn), lambda i, j: (i, j)),
        out_shape=jax.ShapeDtypeStruct((M, N), F32),
        compiler_params=_params("parallel", "arbitrary"),
    )(a, wg, p, wp, x)


SCALE = HEAD_DIM ** -0.5
KEY_TILE = 256
QUERY_TILE = 256
ASSEMBLE_PAGES = 8
INT32_MIN = -2 ** 31

_NT = (((1,), (1,)), ((), ()))


def _qk(q, k):
    return lax.dot_general(q, k, _NT, preferred_element_type=F32)


def _softmax_init(m_ref, l_ref, acc_ref):
    m_ref[...] = jnp.full(m_ref.shape, NEG, F32)
    l_ref[...] = jnp.zeros(l_ref.shape, F32)
    acc_ref[...] = jnp.zeros(acc_ref.shape, F32)


def _softmax_step(s, mask, v, m_ref, l_ref, acc_ref, r):
    s = jnp.where(mask, s, NEG)
    m_prev = m_ref[r]
    m_new = jnp.maximum(m_prev, jnp.max(s, axis=-1, keepdims=True))
    p = jnp.where(mask, jnp.exp(s - m_new), 0.0)
    alpha = jnp.exp(m_prev - m_new)
    l_ref[r] = alpha * l_ref[r] + jnp.sum(p, axis=-1, keepdims=True)
    acc_ref[r] = alpha * acc_ref[r] + jnp.dot(p.astype(BF16), v, preferred_element_type=F32)
    m_ref[r] = m_new


def _softmax_out(l_ref, acc_ref, r):
    return acc_ref[r] / jnp.maximum(l_ref[r], 1e-30)


def _positions(qpos0, kpos0, tq, tk):
    qpos = qpos0 + lax.broadcasted_iota(jnp.int32, (tq, tk), 0)
    kpos = kpos0 + lax.broadcasted_iota(jnp.int32, (tq, tk), 1)
    return qpos, kpos


def _lane_column(x, idx):
    lane = lax.broadcasted_iota(jnp.int32, x.shape, 1)
    return jnp.sum(jnp.where(lane == idx, x, 0.0), axis=-1, keepdims=True)


def _t5_bucket(n):
    n = jnp.maximum(n, 0)
    exact = N_BUCKETS // 2
    nf = jnp.maximum(n, 1).astype(jnp.float32)
    big = exact + (jnp.log(nf / exact) * ((N_BUCKETS - exact) / math.log(MAX_DISTANCE / exact))).astype(jnp.int32)
    return jnp.where(n < exact, n, jnp.minimum(big, N_BUCKETS - 1))


def _toeplitz_bias(tab):
    i = jnp.arange(KEY_TILE)[:, None]
    j = jnp.arange(KEY_TILE)[None, :]
    d = jnp.stack([i - j, KEY_TILE + i - j])
    bt = tab[_t5_bucket(d)] - tab[N_BUCKETS - 1]
    return jnp.moveaxis(bt, -1, 0).astype(F32)


class _Geom:
    def __init__(self, b, T, past_len):
        self.b, self.T, self.past_len = b, T, past_len
        self.tq = min(QUERY_TILE, T)
        self.tk = KEY_TILE
        assert T % self.tq == 0 and self.tk % self.tq == 0 and past_len % self.tk == 0
        assert self.tq == self.tk or T == self.tq, "a query tile must not straddle key tiles"
        assert MAX_DISTANCE <= self.tk, "bias must be constant two key tiles behind the query tile"
        self.nqt = T // self.tq
        self.L = past_len + T
        self.nkt = (self.L - 1) // self.tk + 1

    def q_tile(self, qt):
        return (self.past_len + qt * self.tq) // self.tk


def _bias_spec(geom, rep):
    return pl.BlockSpec((rep, 1, geom.tq, geom.tk),
                        lambda i, g, q, k: (g, jnp.clip(geom.q_tile(q) - k, 0, 1), 0, 0))


def _assemble_kernel(pt_ref, *refs, n_in, T):
    ins, new_ref, o_ref = refs[:n_in], refs[n_in], refs[n_in + 1]
    s = pl.program_id(1)
    last = pl.num_programs(1) - 1

    @pl.when(s < last)
    def _():
        for p in range(n_in):
            o_ref[0, p * PAGE_SIZE:(p + 1) * PAGE_SIZE, :] = ins[p][0, 0].astype(o_ref.dtype)

    @pl.when(s == last)
    def _():
        o_ref[0] = jnp.zeros(o_ref.shape[1:], o_ref.dtype)
        o_ref[0, :T, :] = new_ref[0]


def assemble_cache(cache, layer, page_table, new):
    b, n_pages = page_table.shape
    T, C = new.shape[1], new.shape[2]
    cache = cache.reshape(cache.shape[0], cache.shape[1], PAGE_SIZE, C)
    P = _tile(n_pages, ASSEMBLE_PAGES, 1)
    steps = n_pages // P

    def page_spec(p):
        return pl.BlockSpec((1, 1, PAGE_SIZE, C),
                            lambda i, s, pt: (layer, pt[i, jnp.minimum(s * P + p, n_pages - 1)], 0, 0))

    grid_spec = pltpu.PrefetchScalarGridSpec(
        num_scalar_prefetch=1,
        grid=(b, steps + 1),
        in_specs=[page_spec(p) for p in range(P)] + [pl.BlockSpec((1, T, C), lambda i, s, pt: (i, 0, 0))],
        out_specs=pl.BlockSpec((1, P * PAGE_SIZE, C), lambda i, s, pt: (i, s, 0)),
    )
    return pl.pallas_call(
        functools.partial(_assemble_kernel, n_in=P, T=T),
        grid_spec=grid_spec,
        out_shape=jax.ShapeDtypeStruct((b, (steps + 1) * P * PAGE_SIZE, C), BF16),
        compiler_params=_params("parallel", "arbitrary"),
    )(page_table, *([cache] * P), new)


def _indexer_kernel(qi_ref, wi_ref, ki_ref, sc_ref, thr_ref, key_ref, *, geom, topk):
    tq, tk = geom.tq, geom.tk
    qt, kt = pl.program_id(1), pl.program_id(2)
    qpos0 = geom.past_len + qt * tq
    last = geom.q_tile(qt)

    @pl.when(kt == 0)
    def _():
        sc_ref[0] = jnp.full(sc_ref.shape[1:], NEG, F32)

    @pl.when(kt <= last)
    def _():
        ki = ki_ref[0]
        wi = wi_ref[0]
        acc = jnp.zeros((tq, tk), F32)
        for h in range(H_IDX):
            s = _qk(qi_ref[0, :, h * D_IDX:(h + 1) * D_IDX], ki)
            acc = acc + jnp.maximum(s, 0.0) * wi[:, h:h + 1]
        qpos, kpos = _positions(qpos0, kt * tk, tq, tk)
        sc_ref[0, kt] = jnp.where(kpos <= qpos, acc * (D_IDX ** -0.5 * H_IDX ** -0.5), NEG)

    @pl.when(kt == geom.nkt - 1)
    def _():
        i = lax.bitcast_convert_type(sc_ref[0], jnp.int32)
        key_ref[...] = i ^ ((i >> 31) & 0x7FFFFFFF)

        def body(bit, lo):
            cand = lo + jnp.left_shift(jnp.int32(1), 31 - bit)
            ge = (key_ref[...] >= cand[None]).astype(F32)
            cnt = jnp.sum(jnp.sum(ge, axis=0), axis=-1, keepdims=True)
            return jnp.where(cnt >= topk, cand, lo)

        lo = lax.fori_loop(0, 32, body, jnp.full((tq, 1), INT32_MIN, jnp.int32))
        thr = lax.bitcast_convert_type(lo ^ ((lo >> 31) & 0x7FFFFFFF), F32)
        thr_ref[0] = jnp.broadcast_to(thr, (tq, LANE))


def dsa_indexer(geom, qarr, qi_col, miscarr, misc_col, kiarr, ki_col):
    b, T, tq, tk, nkt = geom.b, geom.T, geom.tq, geom.tk, geom.nkt
    topk = min(TOPK_A, geom.L // 4)
    return pl.pallas_call(
        functools.partial(_indexer_kernel, geom=geom, topk=topk),
        grid=(b, geom.nqt, nkt),
        in_specs=[pl.BlockSpec((1, tq, H_IDX * D_IDX), lambda i, q, k: (i, q, qi_col)),
                  pl.BlockSpec((1, tq, LANE), lambda i, q, k: (i, q, misc_col)),
                  pl.BlockSpec((1, tk, D_IDX), lambda i, q, k: (i, jnp.minimum(k, geom.q_tile(q)), ki_col))],
        out_specs=[pl.BlockSpec((1, nkt, tq, tk), lambda i, q, k: (i, 0, q, 0)),
                   pl.BlockSpec((1, tq, LANE), lambda i, q, k: (i, q, 0))],
        out_shape=[jax.ShapeDtypeStruct((b, nkt, T, tk), F32), jax.ShapeDtypeStruct((b, T, LANE), F32)],
        scratch_shapes=[pltpu.VMEM((nkt, tq, tk), jnp.int32)],
        compiler_params=_params("parallel", "parallel", "arbitrary"),
    )(qarr, miscarr, kiarr)


def _dsa_attn_kernel(q_ref, k_ref, v_ref, sc_ref, thr_ref, bias_ref, o_ref, m_ref, l_ref, acc_ref, *, geom, rep):
    tq, tk = geom.tq, geom.tk
    qt, kt = pl.program_id(2), pl.program_id(3)
    qpos0 = geom.past_len + qt * tq
    last = geom.q_tile(qt)

    @pl.when(kt == 0)
    def _():
        _softmax_init(m_ref, l_ref, acc_ref)

    @pl.when(kt <= last)
    def _():
        qpos, kpos = _positions(qpos0, kt * tk, tq, tk)
        mask = (sc_ref[0, 0] >= thr_ref[0][:, :1]) & (kpos <= qpos)
        near = (last - kt) <= 1
        k, v = k_ref[0], v_ref[0]
        for r in range(rep):
            s = _qk(q_ref[0, :, r * HEAD_DIM:(r + 1) * HEAD_DIM], k) * SCALE + jnp.where(near, bias_ref[r, 0], 0.0)
            _softmax_step(s, mask, v, m_ref, l_ref, acc_ref, r)

    @pl.when(kt == geom.nkt - 1)
    def _():
        for r in range(rep):
            o_ref[0, :, r * HEAD_DIM:(r + 1) * HEAD_DIM] = _softmax_out(l_ref, acc_ref, r).astype(o_ref.dtype)


def dsa_attention(geom, qarr, q_col, karr, k_col, v_col, scores, thr, bias):
    b, T, tq, tk, nkt = geom.b, geom.T, geom.tq, geom.tk, geom.nkt
    rep = H_A // KV_A

    def kmap(col):
        return lambda i, g, q, k: (i, jnp.minimum(k, geom.q_tile(q)), col + g)

    return pl.pallas_call(
        functools.partial(_dsa_attn_kernel, geom=geom, rep=rep),
        grid=(b, KV_A, geom.nqt, nkt),
        in_specs=[pl.BlockSpec((1, tq, rep * HEAD_DIM), lambda i, g, q, k: (i, q, q_col + g)),
                  pl.BlockSpec((1, tk, HEAD_DIM), kmap(k_col)),
                  pl.BlockSpec((1, tk, HEAD_DIM), kmap(v_col)),
                  pl.BlockSpec((1, 1, tq, tk), lambda i, g, q, k: (i, jnp.minimum(k, geom.q_tile(q)), q, 0)),
                  pl.BlockSpec((1, tq, LANE), lambda i, g, q, k: (i, q, 0)),
                  _bias_spec(geom, rep)],
        out_specs=pl.BlockSpec((1, tq, rep * HEAD_DIM), lambda i, g, q, k: (i, q, g)),
        out_shape=jax.ShapeDtypeStruct((b, T, H_A * HEAD_DIM), BF16),
        scratch_shapes=[pltpu.VMEM((rep, tq, 1), F32), pltpu.VMEM((rep, tq, 1), F32),
                        pltpu.VMEM((rep, tq, HEAD_DIM), F32)],
        compiler_params=_params("parallel", "parallel", "parallel", "arbitrary"),
    )(qarr, karr, karr, scores, thr, bias)


def _moba_select_kernel(q_ref, k_ref, avg_ref, sel_ref, *, geom, nblk, ksel):
    qt = pl.program_id(2)
    cur = geom.q_tile(qt)
    kmean = jnp.dot(avg_ref[...], k_ref[0], preferred_element_type=F32)
    gs = _qk(q_ref[0], kmean.astype(BF16))
    lane = lax.broadcasted_iota(jnp.int32, gs.shape, 1)
    past = lane < cur
    g = jnp.where(past, gs, NEG)
    cnt = jnp.zeros_like(g)
    for m in range(nblk):
        col = g[:, m:m + 1]
        cnt = cnt + ((col > g) | ((col == g) & (lane > m))).astype(F32)
    sel_ref[0, 0] = (past & (cnt < ksel)).astype(F32)


def moba_select(geom, qarr, q_col, karr, k_col):
    b, T, tq = geom.b, geom.T, geom.tq
    nblk = geom.L // MOBA_BLOCK
    assert 0 < nblk <= LANE and MOBA_BLOCK == geom.tk
    ksel = min(MOBA_TOPK, nblk)
    rows = nblk * MOBA_BLOCK
    avg = np.zeros((LANE, rows), np.float32)
    avg[np.arange(rows) // MOBA_BLOCK, np.arange(rows)] = 1.0 / MOBA_BLOCK
    return pl.pallas_call(
        functools.partial(_moba_select_kernel, geom=geom, nblk=nblk, ksel=ksel),
        grid=(b, H_C, geom.nqt),
        in_specs=[pl.BlockSpec((1, tq, HEAD_DIM), lambda i, h, q: (i, q, q_col + h)),
                  pl.BlockSpec((1, rows, HEAD_DIM), lambda i, h, q: (i, 0, k_col + h)),
                  pl.BlockSpec((LANE, rows), lambda i, h, q: (0, 0))],
        out_specs=pl.BlockSpec((1, 1, tq, LANE), lambda i, h, q: (i, h, q, 0)),
        out_shape=jax.ShapeDtypeStruct((b, H_C, T, LANE), F32),
        compiler_params=_params("parallel", "parallel", "arbitrary"),
    )(qarr, karr, jnp.asarray(avg, BF16))


def _moba_attn_kernel(q_ref, k_ref, v_ref, sel_ref, bias_ref, o_ref, m_ref, l_ref, acc_ref, *, geom):
    tq, tk = geom.tq, geom.tk
    qt, kt = pl.program_id(2), pl.program_id(3)
    qpos0 = geom.past_len + qt * tq
    cur = geom.q_tile(qt)

    @pl.when(kt == 0)
    def _():
        _softmax_init(m_ref, l_ref, acc_ref)

    @pl.when(kt <= cur)
    def _():
        qpos, kpos = _positions(qpos0, kt * tk, tq, tk)
        own = kt == cur
        chosen = (_lane_column(sel_ref[0, 0], kt) + own.astype(F32)) > 0.5
        mask = chosen & (kpos <= jnp.where(own, qpos, jnp.int32(2 ** 30)))
        s = _qk(q_ref[0], k_ref[0]) * SCALE + jnp.where((cur - kt) <= 1, bias_ref[0, 0], 0.0)
        _softmax_step(s, mask, v_ref[0], m_ref, l_ref, acc_ref, 0)

    @pl.when(kt == geom.nkt - 1)
    def _():
        o_ref[0] = _softmax_out(l_ref, acc_ref, 0).astype(o_ref.dtype)


def moba_attention(geom, qarr, q_col, karr, k_col, v_col, sel, bias):
    b, T, tq, tk, nkt = geom.b, geom.T, geom.tq, geom.tk, geom.nkt

    def kmap(col):
        return lambda i, h, q, k: (i, jnp.minimum(k, geom.q_tile(q)), col + h)

    return pl.pallas_call(
        functools.partial(_moba_attn_kernel, geom=geom),
        grid=(b, H_C, geom.nqt, nkt),
        in_specs=[pl.BlockSpec((1, tq, HEAD_DIM), lambda i, h, q, k: (i, q, q_col + h)),
                  pl.BlockSpec((1, tk, HEAD_DIM), kmap(k_col)),
                  pl.BlockSpec((1, tk, HEAD_DIM), kmap(v_col)),
                  pl.BlockSpec((1, 1, tq, LANE), lambda i, h, q, k: (i, h, q, 0)),
                  _bias_spec(geom, 1)],
        out_specs=pl.BlockSpec((1, tq, HEAD_DIM), lambda i, h, q, k: (i, q, h)),
        out_shape=jax.ShapeDtypeStruct((b, T, H_C * HEAD_DIM), BF16),
        scratch_shapes=[pltpu.VMEM((1, tq, 1), F32), pltpu.VMEM((1, tq, 1), F32), pltpu.VMEM((1, tq, HEAD_DIM), F32)],
        compiler_params=_params("parallel", "parallel", "parallel", "arbitrary"),
    )(qarr, karr, karr, sel, bias)


CMP_ROW = CMP_STRIDE * 2 * KV_B * HEAD_DIM


def _compress_kernel(x_ref, xn_ref, wlo_ref, whi_ref, c_ref, w2_ref, g_ref, o_ref, *, tr):
    x = x_ref[0]
    lo = jnp.dot(x, wlo_ref[...], preferred_element_type=F32)
    hi = jnp.dot(x, whi_ref[...], preferred_element_type=F32)
    hi_next = jnp.dot(xn_ref[0], whi_ref[...], preferred_element_type=F32)
    row = lax.broadcasted_iota(jnp.int32, hi.shape, 0)
    hi = jnp.where(row == tr - 1, hi_next[0:1], pltpu.roll(hi, tr - 1, 0))
    act = jax.nn.gelu(lo + hi + c_ref[...]).astype(BF16)
    for c in range(2 * KV_B):
        cols = slice(c * HEAD_DIM, (c + 1) * HEAD_DIM)
        y = jnp.dot(act[:, cols], w2_ref[c // KV_B], preferred_element_type=F32)
        if c < KV_B:
            y = y * lax.rsqrt(jnp.mean(y * y, axis=-1, keepdims=True) + EPS) * g_ref[...]
        o_ref[0, :, cols] = y.astype(o_ref.dtype)


def nsa_compress(x, wlo, whi, const, w2, gain):
    b, R, _ = x.shape
    tr = _tile(R, 128, BF16_SUBLANE)
    nxt = tr // BF16_SUBLANE
    n_sub16 = R // BF16_SUBLANE
    C = 2 * KV_B * HEAD_DIM
    return pl.pallas_call(
        functools.partial(_compress_kernel, tr=tr),
        grid=(b, R // tr),
        in_specs=[pl.BlockSpec((1, tr, CMP_ROW), lambda i, r: (i, r, 0)),
                  pl.BlockSpec((1, BF16_SUBLANE, CMP_ROW), lambda i, r: (i, jnp.minimum((r + 1) * nxt, n_sub16 - 1), 0)),
                  pl.BlockSpec((CMP_ROW, C), lambda i, r: (0, 0)),
                  pl.BlockSpec((CMP_ROW, C), lambda i, r: (0, 0)),
                  pl.BlockSpec((1, C), lambda i, r: (0, 0)),
                  pl.BlockSpec((2, HEAD_DIM, HEAD_DIM), lambda i, r: (0, 0, 0)),
                  pl.BlockSpec((1, HEAD_DIM), lambda i, r: (0, 0))],
        out_specs=pl.BlockSpec((1, tr, C), lambda i, r: (i, r, 0)),
        out_shape=jax.ShapeDtypeStruct((b, R, C), BF16),
        compiler_params=_params("parallel", "arbitrary"),
    )(x, x, wlo, whi, const, w2, gain)


def _prep_compress(cmp_pos, cmp_w1, cmp_w2, kn_cmp):
    r = CMP_LEN // CMP_STRIDE
    assert r == 2
    w1 = cmp_w1.reshape(2, r, CMP_STRIDE, HEAD_DIM, HEAD_DIM)
    kv_of_col = np.repeat(np.arange(2), KV_B)
    eye = jnp.eye(2 * KV_B, dtype=cmp_w1.dtype)

    def expand(half):
        w = w1[kv_of_col, half]
        return jnp.einsum('crde,cf->rcdfe', w, eye).reshape(CMP_ROW, 2 * KV_B * HEAD_DIM).astype(BF16)

    const = jnp.einsum('krd,krde->ke', cmp_pos.reshape(2, CMP_LEN, HEAD_DIM),
                       cmp_w1.reshape(2, CMP_LEN, HEAD_DIM, HEAD_DIM), precision=lax.Precision.HIGHEST)
    const = const[kv_of_col].reshape(1, -1).astype(F32)
    return dict(wlo=expand(0), whi=expand(1), const=const, w2=cmp_w2.astype(BF16), gain=kn_cmp.reshape(1, -1).astype(F32))


def _nsa_cmp_kernel(q_ref, kc_ref, vc_ref, bias_ref, ov_ref, oc_ref, sel_ref, *, geom, rep, n_slc, nsel):
    tq = geom.tq
    qt = pl.program_id(2)
    qpos0 = geom.past_len + qt * tq
    kc, vc, ov = kc_ref[0], vc_ref[0], ov_ref[...]
    ncp, nsp = ov.shape
    qpos, n_idx = _positions(qpos0, 0, tq, ncp)
    valid = (n_idx * CMP_STRIDE + (CMP_LEN - 1)) <= qpos
    imp = jnp.zeros((tq, nsp), F32)
    for r in range(rep):
        cols = slice(r * HEAD_DIM, (r + 1) * HEAD_DIM)
        s = jnp.where(valid, _qk(q_ref[0, :, cols], kc) * SCALE + bias_ref[r], NEG)
        e = jnp.where(valid, jnp.exp(s - jnp.max(s, axis=-1, keepdims=True)), 0.0)
        p = (e / jnp.maximum(jnp.sum(e, axis=-1, keepdims=True), 1e-30)).astype(BF16)
        oc_ref[0, :, cols] = jnp.dot(p, vc, preferred_element_type=F32)
        imp = imp + jnp.dot(p, ov, preferred_element_type=F32)
    spos, jj = _positions(qpos0, 0, tq, nsp)
    cur = spos // SLC_LEN
    forced = (jj == 0) | (jj == cur) | (jj == cur - 1)
    imp = jnp.where(jj > cur, NEG, jnp.where(forced, -NEG, imp))
    cnt = jnp.zeros_like(imp)
    for s_blk in range(n_slc):
        col = imp[:, s_blk:s_blk + 1]
        cnt = cnt + ((col > imp) | ((col == imp) & (jj > s_blk))).astype(F32)
    sel_ref[0, 0] = ((cnt < nsel) & (jj < n_slc)).astype(F32)


def nsa_compressed(geom, qarr, q_col, cmpkv, bias_cmp):
    b, T, tq = geom.b, geom.T, geom.tq
    rep = H_B // KV_B
    ncp = cmpkv.shape[1]
    n_slc = -(-geom.L // SLC_LEN)
    nsel = min(N_SLC, n_slc)
    nsp = _round_up(n_slc, LANE)
    n_cmp = geom.L // CMP_STRIDE - CMP_LEN // CMP_STRIDE + 1
    cstart = np.arange(ncp) * CMP_STRIDE
    cend = cstart + CMP_LEN - 1
    sstart = np.arange(nsp) * SLC_LEN
    ov = ((cstart[:, None] < sstart[None, :] + SLC_LEN) & (cend[:, None] >= sstart[None, :])
          & (np.arange(ncp)[:, None] < n_cmp) & (np.arange(nsp)[None, :] < n_slc)).astype(np.float32)
    return pl.pallas_call(
        functools.partial(_nsa_cmp_kernel, geom=geom, rep=rep, n_slc=n_slc, nsel=nsel),
        grid=(b, KV_B, geom.nqt),
        in_specs=[pl.BlockSpec((1, tq, rep * HEAD_DIM), lambda i, g, q: (i, q, q_col + g)),
                  pl.BlockSpec((1, ncp, HEAD_DIM), lambda i, g, q: (i, 0, g)),
                  pl.BlockSpec((1, ncp, HEAD_DIM), lambda i, g, q: (i, 0, KV_B + g)),
                  pl.BlockSpec((rep, tq, ncp), lambda i, g, q: (g, q, 0)),
                  pl.BlockSpec((ncp, nsp), lambda i, g, q: (0, 0))],
        out_specs=[pl.BlockSpec((1, tq, rep * HEAD_DIM), lambda i, g, q: (i, q, g)),
                   pl.BlockSpec((1, 1, tq, nsp), lambda i, g, q: (i, g, q, 0))],
        out_shape=[jax.ShapeDtypeStruct((b, T, H_B * HEAD_DIM), F32), jax.ShapeDtypeStruct((b, KV_B, T, nsp), F32)],
        compiler_params=_params("parallel", "parallel", "arbitrary"),
    )(qarr, cmpkv, cmpkv, bias_cmp, jnp.asarray(ov, BF16))


def _nsa_attn_kernel(q_ref, ks_ref, vs_ref, kw_ref, vw_ref, sel_ref, bias_ref, oc_ref, misc_ref, o_ref,
                     m_ref, l_ref, acc_ref, *, geom, rep):
    tq, tk = geom.tq, geom.tk
    g, qt, kt = pl.program_id(1), pl.program_id(2), pl.program_id(3)
    qpos0 = geom.past_len + qt * tq
    last = geom.q_tile(qt)

    @pl.when(kt == 0)
    def _():
        _softmax_init(m_ref, l_ref, acc_ref)

    def branch(k, v, mask, slot0):
        near = (last - kt) <= 1
        for r in range(rep):
            s = _qk(q_ref[0, :, r * HEAD_DIM:(r + 1) * HEAD_DIM], k) * SCALE + jnp.where(near, bias_ref[r, 0], 0.0)
            _softmax_step(s, mask, v, m_ref, l_ref, acc_ref, slot0 + r)

    @pl.when(kt <= last)
    def _():
        qpos, kpos = _positions(qpos0, kt * tk, tq, tk)
        nsp = sel_ref.shape[-1]
        blk = lax.broadcasted_iota(jnp.int32, (nsp, tk), 0)
        tok = lax.broadcasted_iota(jnp.int32, (nsp, tk), 1)
        expand = (blk == kt * (tk // SLC_LEN) + tok // SLC_LEN).astype(BF16)
        chosen = jnp.dot(sel_ref[0, 0].astype(BF16), expand, preferred_element_type=F32) > 0.5
        branch(ks_ref[0], vs_ref[0], chosen & (kpos <= qpos), 0)

    @pl.when((kt <= last) & (kt >= last - WINDOW // tk))
    def _():
        qpos, kpos = _positions(qpos0, kt * tk, tq, tk)
        dist = qpos - kpos
        branch(kw_ref[0], vw_ref[0], (dist >= 0) & (dist < WINDOW), rep)

    @pl.when(kt == geom.nkt - 1)
    def _():
        gates = jax.nn.sigmoid(misc_ref[0])
        for r in range(rep):
            cols = slice(r * HEAD_DIM, (r + 1) * HEAD_DIM)
            base = H_IDX + (g * rep + r) * N_BRANCH
            o = (_lane_column(gates, base) * oc_ref[0, :, cols]
                 + _lane_column(gates, base + 1) * _softmax_out(l_ref, acc_ref, r)
                 + _lane_column(gates, base + 2) * _softmax_out(l_ref, acc_ref, rep + r))
            o_ref[0, :, cols] = o.astype(o_ref.dtype)


def nsa_attention(geom, qarr, q_col, sarr, ks_col, vs_col, warr, kw_col, vw_col, w_tile0, sel, bias, oc, miscarr, misc_col):
    b, T, tq, tk, nkt = geom.b, geom.T, geom.tq, geom.tk, geom.nkt
    rep = H_B // KV_B
    nsp = sel.shape[-1]
    n_wt = warr.shape[1] // tk
    n_back = WINDOW // tk

    def smap(col):
        return lambda i, g, q, k: (i, jnp.minimum(k, geom.q_tile(q)), col + g)

    def wmap(col):
        def index(i, g, q, k):
            last = geom.q_tile(q)
            return (i, jnp.clip(jnp.clip(k, last - n_back, last) - w_tile0, 0, n_wt - 1), col + g)
        return index

    return pl.pallas_call(
        functools.partial(_nsa_attn_kernel, geom=geom, rep=rep),
        grid=(b, KV_B, geom.nqt, nkt),
        in_specs=[pl.BlockSpec((1, tq, rep * HEAD_DIM), lambda i, g, q, k: (i, q, q_col + g)),
                  pl.BlockSpec((1, tk, HEAD_DIM), smap(ks_col)),
                  pl.BlockSpec((1, tk, HEAD_DIM), smap(vs_col)),
                  pl.BlockSpec((1, tk, HEAD_DIM), wmap(kw_col)),
                  pl.BlockSpec((1, tk, HEAD_DIM), wmap(vw_col)),
                  pl.BlockSpec((1, 1, tq, nsp), lambda i, g, q, k: (i, g, q, 0)),
                  _bias_spec(geom, rep),
                  pl.BlockSpec((1, tq, rep * HEAD_DIM), lambda i, g, q, k: (i, q, g)),
                  pl.BlockSpec((1, tq, LANE), lambda i, g, q, k: (i, q, misc_col))],
        out_specs=pl.BlockSpec((1, tq, rep * HEAD_DIM), lambda i, g, q, k: (i, q, g)),
        out_shape=jax.ShapeDtypeStruct((b, T, H_B * HEAD_DIM), BF16),
        scratch_shapes=[pltpu.VMEM((2 * rep, tq, 1), F32), pltpu.VMEM((2 * rep, tq, 1), F32),
                        pltpu.VMEM((2 * rep, tq, HEAD_DIM), F32)],
        compiler_params=_params("parallel", "parallel", "parallel", "arbitrary"),
    )(qarr, sarr, sarr, warr, warr, sel, bias, oc, miscarr)


_SEG = {}
_off = 0
for _name, _w in (("qa", H_A * HEAD_DIM), ("qb", H_B * HEAD_DIM), ("qi", H_IDX * D_IDX), ("qc", H_C * HEAD_DIM),
                  ("kvc", 2 * H_C * HEAD_DIM), ("kva", 2 * KV_A * HEAD_DIM), ("cmp", 2 * KV_B * HEAD_DIM),
                  ("slc", 2 * KV_B * HEAD_DIM), ("win", 2 * KV_B * HEAD_DIM), ("ki", D_IDX), ("misc", LANE)):
    _SEG[_name] = (_off, _off + _w)
    _off += _w
IN_COLS_ALIGNED = _off


def _col(name, width=LANE):
    assert _SEG[name][0] % width == 0
    return _SEG[name][0] // width


def _prep_in_proj(w_in, qn_a, kn_a, qn_b, kn_b, qn_c, kn_c):
    sizes = (H_A * HEAD_DIM, KV_A * HEAD_DIM, KV_A * HEAD_DIM, H_IDX * D_IDX, H_IDX, D_IDX,
             H_B * HEAD_DIM, N_BRANCH * H_B) + (KV_B * HEAD_DIM,) * 6 + (H_C * HEAD_DIM,) * 3
    splits = np.cumsum(sizes)[:-1].tolist()
    (qa, ka, va, qi, wi, ki, q_b, g_b, kbc, vbc, kbs, vbs, kbw, vbw, qc, kc, vc) = jnp.split(w_in, splits, axis=-1)
    D = w_in.shape[0]
    misc = jnp.concatenate([wi, g_b, jnp.zeros((D, LANE - H_IDX - N_BRANCH * H_B), w_in.dtype)], axis=-1)
    n_pad = _round_up(IN_COLS_ALIGNED, 512) - IN_COLS_ALIGNED
    w = jnp.concatenate([qa, q_b, qi, qc, kc, vc, ka, va, kbc, vbc, kbs, vbs, kbw, vbw, ki, misc,
                         jnp.zeros((D, n_pad), w_in.dtype)], axis=-1).astype(BF16)
    one = jnp.ones((HEAD_DIM,), F32)

    def rep(g, n):
        return jnp.tile(g.astype(F32), n)

    gains = jnp.concatenate([
        rep(qn_a, H_A), rep(qn_b, H_B), rep(one, H_IDX), rep(qn_c, H_C), rep(kn_c, H_C), rep(one, H_C),
        rep(kn_a, KV_A), rep(one, KV_A), rep(one, 2 * KV_B), rep(kn_b[1], KV_B), rep(one, KV_B),
        rep(kn_b[2], KV_B), rep(one, KV_B), one, one, jnp.ones((n_pad,), F32)]).reshape(1, -1)
    flags = np.concatenate([
        np.ones(H_A), np.ones(H_B), np.zeros(H_IDX), np.ones(H_C), np.ones(H_C), np.zeros(H_C),
        np.ones(KV_A), np.zeros(KV_A), np.zeros(2 * KV_B), np.ones(KV_B), np.zeros(KV_B),
        np.ones(KV_B), np.zeros(KV_B), np.zeros(1), np.zeros(1), np.zeros(n_pad // LANE)]).astype(np.int32)
    return w, gains, jnp.asarray(flags)


def _prep_layer(i, ffn1_norm, ffn1_wg, ffn1_wu, ffn1_wd, mix_norm, w_in, qn_a, kn_a, qn_b, kn_b, cmp_pos, cmp_w1,
                cmp_w2, qn_c, kn_c, wb_a, wb_b, wb_c, w_gate, b_gate, w_out, ffn2_norm, ffn2_wg, ffn2_wu, ffn2_wd,
                ple_norm, ple_wg, ple_wp):
    F = ffn1_wg.shape[2]
    Fp = _round_up(F, 512) if F > 512 else F

    def up(w):
        return jnp.pad(w[i], ((0, 0), (0, Fp - F))).astype(BF16)

    def down(w):
        return jnp.pad(w[i], ((0, Fp - F), (0, 0))).astype(BF16)

    w_in_p, gains, flags = _prep_in_proj(w_in[i], qn_a[i], kn_a[i], qn_b[i], kn_b[i], qn_c[i], kn_c[i])
    return dict(
        f1n=ffn1_norm[i], f1g=up(ffn1_wg), f1u=up(ffn1_wu), f1d=down(ffn1_wd), mn=mix_norm[i],
        w_in=w_in_p, gains=gains, flags=flags,
        cmp=_prep_compress(cmp_pos[i], cmp_w1[i], cmp_w2[i], kn_b[i, 0]),
        wb_a=wb_a[i].astype(BF16), wb_b=wb_b[i].astype(BF16), wb_c=wb_c[i].astype(BF16),
        w_gate=w_gate[i].astype(BF16), b_gate=b_gate[i].reshape(1, -1), w_out=w_out[i].astype(BF16),
        f2n=ffn2_norm[i], f2g=up(ffn2_wg), f2u=up(ffn2_wu), f2d=down(ffn2_wd),
        pn=ple_norm[i], pwg=ple_wg[i].astype(BF16), pwp=ple_wp[i].astype(BF16))


def _layer(x, p, geom, layer, caches, W, bias):
    b, T = geom.b, geom.T
    M = b * T
    h = ffn_down(ffn_up(rmsnorm_bf16(x, W["f1n"]), W["f1g"], W["f1u"]), W["f1d"], x)
    u = rmsnorm_bf16(h, W["mn"])
    proj32, proj16 = in_proj(u, W["w_in"], W["gains"], W["flags"])
    Np = proj32.shape[1]
    p32 = proj32.reshape(b, T, Np)
    p16 = proj16.reshape(b, T, Np)

    def seg32(name):
        lo, hi = _SEG[name]
        return p32[:, :, lo:hi]

    def seg16(name):
        lo, hi = _SEG[name]
        return p16[:, :, lo:hi]

    win_new = seg32("win")
    if caches is None:
        assert T >= WINDOW
        a_kv, a_ki, slc, c_kv, win = (p16,) * 5
        a_k, a_v, ki_c = _col("kva"), _col("kva") + KV_A, _col("ki")
        s_k, s_v = _col("slc"), _col("slc") + KV_B
        w_k, w_v, w_tile0 = _col("win"), _col("win") + KV_B, 0
        c_k, c_v = _col("kvc"), _col("kvc") + H_C
        cmp_rows = seg16("cmp")
        win_state = win_new[:, T - WINDOW:]
    else:
        cache_a_kv, cache_a_kidx, cache_b_cmp, cache_b_slc, win_past, cache_c_kv, page_table = caches
        a_kv = assemble_cache(cache_a_kv, layer, page_table, seg16("kva"))
        a_ki = assemble_cache(cache_a_kidx, layer, page_table, seg16("ki"))
        cmp_rows = assemble_cache(cache_b_cmp, layer, page_table, seg16("cmp"))
        slc = assemble_cache(cache_b_slc, layer, page_table, seg16("slc"))
        c_kv = assemble_cache(cache_c_kv, layer, page_table, seg16("kvc"))
        a_k, a_v, ki_c, s_k, s_v, w_k, w_v, c_k, c_v = 0, KV_A, 0, 0, KV_B, 0, KV_B, 0, H_C
        n_buf = win_past.shape[2]
        assert n_buf % geom.tk == 0 and T <= geom.tk and n_buf >= WINDOW
        win_past = win_past[layer].reshape(b, n_buf, -1)
        win = jnp.concatenate([win_past.astype(BF16), seg16("win"),
                               jnp.zeros((b, geom.tk - T, win_past.shape[-1]), BF16)], axis=1)
        w_tile0 = (geom.past_len - n_buf) // geom.tk
        win_state = jnp.concatenate([win_past, win_new], axis=1)[:, -WINDOW:]

    scores, thr = dsa_indexer(geom, p16, _col("qi", H_IDX * D_IDX), p32, _col("misc"), a_ki, ki_c)
    o_a = dsa_attention(geom, p16, _col("qa", 4 * HEAD_DIM), a_kv, a_k, a_v, scores, thr, bias["a"])
    cmpkv = nsa_compress(cmp_rows.reshape(b, -1, CMP_ROW), **W["cmp"])
    qpos = geom.past_len + jnp.arange(T)
    cend = jnp.arange(cmpkv.shape[1]) * CMP_STRIDE + CMP_LEN - 1
    bias_cmp = jnp.moveaxis(bias["tab_b"][_t5_bucket(qpos[:, None] - cend[None, :])], -1, 0)
    oc, sel_b = nsa_compressed(geom, p16, _col("qb", 4 * HEAD_DIM), cmpkv, bias_cmp)
    o_b = nsa_attention(geom, p16, _col("qb", 4 * HEAD_DIM), slc, s_k, s_v, win, w_k, w_v, w_tile0, sel_b, bias["b"],
                        oc, p32, _col("misc"))
    sel_c = moba_select(geom, p16, _col("qc"), c_kv, c_k)
    o_c = moba_attention(geom, p16, _col("qc"), c_kv, c_k, c_v, sel_c, bias["c"])

    merged = gate_merge(u, o_a.reshape(M, -1), o_b.reshape(M, -1), o_c.reshape(M, -1),
                        W["w_gate"], W["b_gate"], W["wb_a"], W["wb_b"], W["wb_c"])
    h = out_proj(merged, W["w_out"], h)
    h = ffn_down(ffn_up(rmsnorm_bf16(h, W["f2n"]), W["f2g"], W["f2u"]), W["f2d"], h)
    h = ple(rmsnorm_bf16(h, W["pn"]), W["pwg"], p.astype(BF16), W["pwp"], h)
    state = (seg32("kva").reshape(b, T, 2, KV_A, HEAD_DIM), seg32("ki"), seg32("cmp").reshape(b, T, 2, KV_B, HEAD_DIM),
             seg32("slc").reshape(b, T, 2, KV_B, HEAD_DIM), win_state.reshape(b, WINDOW, 2, KV_B, HEAD_DIM),
             seg32("kvc").reshape(b, T, 2, H_C, HEAD_DIM))
    return h, state


def kernel(x_prompt, x_sample, cache_a_kv, cache_a_kidx, cache_b_cmp_kv, cache_b_slc_kv, state_b_win_kv, cache_c_kv, page_table, p_prompt, p_sample, rel_bias, ffn1_norm, ffn1_wg, ffn1_wu, ffn1_wd, mix_norm, w_in, qn_a, kn_a, qn_b, kn_b, cmp_pos, cmp_w1, cmp_w2, qn_c, kn_c, wb_a, wb_b, wb_c, w_gate, b_gate, w_out, ffn2_norm, ffn2_wg, ffn2_wu, ffn2_wd, ple_norm, ple_wg, ple_wp):
    depth = ffn1_norm.shape[0]
    bp, Tp, D = x_prompt.shape
    bs, Ts, _ = x_sample.shape
    geom_p = _Geom(bp, Tp, 0)
    geom_s = _Geom(bs, Ts, page_table.shape[1] * PAGE_SIZE)
    tab_b = rel_bias[:, H_A:H_A + H_B].astype(F32)
    bias = dict(a=_toeplitz_bias(rel_bias[:, :H_A]), b=_toeplitz_bias(tab_b),
                c=_toeplitz_bias(rel_bias[:, H_A + H_B:]), tab_b=tab_b)
    caches = (cache_a_kv, cache_a_kidx, cache_b_cmp_kv, cache_b_slc_kv, state_b_win_kv, cache_c_kv, page_table)

    y_p = x_prompt.reshape(bp * Tp, D)
    y_s = x_sample.reshape(bs * Ts, D)
    sp_list, ss_list = [], []
    for i in range(depth):
        W = _prep_layer(i, ffn1_norm, ffn1_wg, ffn1_wu, ffn1_wd, mix_norm, w_in, qn_a, kn_a, qn_b, kn_b, cmp_pos,
                        cmp_w1, cmp_w2, qn_c, kn_c, wb_a, wb_b, wb_c, w_gate, b_gate, w_out, ffn2_norm, ffn2_wg,
                        ffn2_wu, ffn2_wd, ple_norm, ple_wg, ple_wp)
        y_p, sp = _layer(y_p, p_prompt[i].reshape(bp * Tp, -1), geom_p, i, None, W, bias)
        y_s, ss = _layer(y_s, p_sample[i].reshape(bs * Ts, -1), geom_s, i, caches, W, bias)
        sp_list.append(sp)
        ss_list.append(ss)

    outs = [y_p.reshape(bp, Tp, D), y_s.reshape(bs, Ts, D)]
    for j in range(6):
        outs.append(jnp.stack([s[j] for s in sp_list]))
        outs.append(jnp.stack([s[j] for s in ss_list]))
    return tuple(outs)
```

```python
import functools
import math

import numpy as np
import jax
import jax.numpy as jnp
from jax import lax
from jax.experimental import pallas as pl
from jax.experimental.pallas import tpu as pltpu

PAGE_SIZE = 128
HEAD_DIM = 128
H_A = 8
KV_A = 2
H_IDX = 16
D_IDX = 128
TOPK_A = 256
H_B = 8
KV_B = 2
CMP_LEN = 32
CMP_STRIDE = 16
SLC_LEN = 64
N_SLC = 16
WINDOW = 512
H_C = 8
MOBA_BLOCK = 256
MOBA_TOPK = 3
N_BUCKETS = 32
MAX_DISTANCE = 128
N_BRANCH = 3
EPS = 1e-6
NEG = -1e30

LANE = 128
BF16_SUBLANE = 16
V7X_VMEM_LIMIT = 56 * 1024 * 1024

F32 = jnp.float32
BF16 = jnp.bfloat16


def _tile(n, target, align):
    best = None
    for t in range(align, min(n, target) + 1, align):
        if n % t == 0:
            best = t
    return best if best is not None else n


def _round_up(n, m):
    return -(-n // m) * m


def _params(*sem):
    return pltpu.CompilerParams(dimension_semantics=sem, vmem_limit_bytes=V7X_VMEM_LIMIT)


def _rmsnorm_kernel(x_ref, g_ref, o_ref):
    x = x_ref[...]
    y = x * lax.rsqrt(jnp.mean(x * x, axis=-1, keepdims=True) + EPS)
    o_ref[...] = (y * g_ref[...]).astype(o_ref.dtype)


def rmsnorm_bf16(x, g):
    M, D = x.shape
    tm = _tile(M, 256, BF16_SUBLANE)
    return pl.pallas_call(
        _rmsnorm_kernel,
        grid=(M // tm,),
        in_specs=[pl.BlockSpec((tm, D), lambda i: (i, 0)), pl.BlockSpec((1, D), lambda i: (0, 0))],
        out_specs=pl.BlockSpec((tm, D), lambda i: (i, 0)),
        out_shape=jax.ShapeDtypeStruct((M, D), BF16),
        compiler_params=_params("parallel"),
    )(x, g.reshape(1, D))


def _ffn_up_kernel(a_ref, wg_ref, wu_ref, o_ref):
    a = a_ref[...]
    g = jnp.dot(a, wg_ref[...], preferred_element_type=F32)
    u = jnp.dot(a, wu_ref[...], preferred_element_type=F32)
    o_ref[...] = (g * jax.nn.sigmoid(g) * u).astype(o_ref.dtype)


def ffn_up(a, wg, wu):
    M, D = a.shape
    F = wg.shape[1]
    tm = _tile(M, 1024, BF16_SUBLANE)
    tn = _tile(F, 512, LANE)
    return pl.pallas_call(
        _ffn_up_kernel,
        grid=(M // tm, F // tn),
        in_specs=[pl.BlockSpec((tm, D), lambda i, j: (i, 0)),
                  pl.BlockSpec((D, tn), lambda i, j: (0, j)),
                  pl.BlockSpec((D, tn), lambda i, j: (0, j))],
        out_specs=pl.BlockSpec((tm, tn), lambda i, j: (i, j)),
        out_shape=jax.ShapeDtypeStruct((M, F), BF16),
        compiler_params=_params("parallel", "arbitrary"),
    )(a, wg, wu)


def _ffn_down_kernel(a_ref, w_ref, x_ref, o_ref, acc_ref):
    k = pl.program_id(2)

    @pl.when(k == 0)
    def _():
        acc_ref[...] = jnp.zeros_like(acc_ref)

    acc_ref[...] += jnp.dot(a_ref[...], w_ref[...], preferred_element_type=F32)

    @pl.when(k == pl.num_programs(2) - 1)
    def _():
        o_ref[...] = x_ref[...] + 0.5 * acc_ref[...]


def ffn_down(a, w, x):
    M, F = a.shape
    D = w.shape[1]
    tm = _tile(M, 1024, BF16_SUBLANE)
    tn = _tile(D, 1024, LANE)
    tk = _tile(F, 2816, LANE)
    return pl.pallas_call(
        _ffn_down_kernel,
        grid=(M // tm, D // tn, F // tk),
        in_specs=[pl.BlockSpec((tm, tk), lambda i, j, k: (i, k)),
                  pl.BlockSpec((tk, tn), lambda i, j, k: (k, j)),
                  pl.BlockSpec((tm, tn), lambda i, j, k: (i, j))],
        out_specs=pl.BlockSpec((tm, tn), lambda i, j, k: (i, j)),
        out_shape=jax.ShapeDtypeStruct((M, D), F32),
        scratch_shapes=[pltpu.VMEM((tm, tn), F32)],
        compiler_params=_params("parallel", "arbitrary", "arbitrary"),
    )(a, w, x)


def _w_in_kernel(flags_ref, a_ref, w_ref, g_ref, o_ref, o16_ref, *, heads_per_tile):
    j = pl.program_id(1)
    acc = jnp.dot(a_ref[...], w_ref[...], preferred_element_type=F32)
    for c in range(heads_per_tile):
        cols = slice(c * HEAD_DIM, (c + 1) * HEAD_DIM)
        y = acc[:, cols]
        yn = y * lax.rsqrt(jnp.mean(y * y, axis=-1, keepdims=True) + EPS) * g_ref[:, cols]
        y = jnp.where(flags_ref[j * heads_per_tile + c] > 0, yn, y)
        o_ref[:, cols] = y
        o16_ref[:, cols] = y.astype(o16_ref.dtype)


def in_proj(a, w, gains, flags):
    M, D = a.shape
    N = w.shape[1]
    tm = _tile(M, 1024, BF16_SUBLANE)
    tn = _tile(N, 512, LANE)
    grid_spec = pltpu.PrefetchScalarGridSpec(
        num_scalar_prefetch=1,
        grid=(M // tm, N // tn),
        in_specs=[pl.BlockSpec((tm, D), lambda i, j, f: (i, 0)),
                  pl.BlockSpec((D, tn), lambda i, j, f: (0, j)),
                  pl.BlockSpec((1, tn), lambda i, j, f: (0, j))],
        out_specs=[pl.BlockSpec((tm, tn), lambda i, j, f: (i, j)), pl.BlockSpec((tm, tn), lambda i, j, f: (i, j))],
    )
    return pl.pallas_call(
        functools.partial(_w_in_kernel, heads_per_tile=tn // HEAD_DIM),
        grid_spec=grid_spec,
        out_shape=[jax.ShapeDtypeStruct((M, N), F32), jax.ShapeDtypeStruct((M, N), BF16)],
        compiler_params=_params("parallel", "arbitrary"),
    )(flags, a, w, gains)


def _gate_merge_kernel(u_ref, oa_ref, ob_ref, oc_ref, wg0_ref, wg1_ref, wg2_ref, bg0_ref, bg1_ref, bg2_ref,
                       wba_ref, wbb_ref, wbc_ref, o_ref):
    u = u_ref[...]
    out = None
    for o_r, wg_r, bg_r, wb_r in ((oa_ref, wg0_ref, bg0_ref, wba_ref),
                                  (ob_ref, wg1_ref, bg1_ref, wbb_ref),
                                  (oc_ref, wg2_ref, bg2_ref, wbc_ref)):
        gate = jax.nn.sigmoid(jnp.dot(u, wg_r[...], preferred_element_type=F32) + bg_r[...])
        branch = jnp.dot(o_r[...], wb_r[...], preferred_element_type=F32)
        out = gate * branch if out is None else out + gate * branch
    o_ref[...] = out.astype(o_ref.dtype)


def gate_merge(u, o_a, o_b, o_c, w_gate, b_gate, wb_a, wb_b, wb_c):
    M, D = u.shape
    Ho = o_a.shape[1]
    tm = _tile(M, 1024, BF16_SUBLANE)
    tn = _tile(D, 256, LANE)
    nj = D // tn
    a_spec = pl.BlockSpec((tm, D), lambda i, j: (i, 0))
    o_spec = pl.BlockSpec((tm, Ho), lambda i, j: (i, 0))
    wb_spec = pl.BlockSpec((Ho, tn), lambda i, j: (0, j))

    def branch_spec(rows, br):
        return pl.BlockSpec((rows, tn), lambda i, j: (0, br * nj + j))

    return pl.pallas_call(
        _gate_merge_kernel,
        grid=(M // tm, nj),
        in_specs=[a_spec, o_spec, o_spec, o_spec,
                  branch_spec(D, 0), branch_spec(D, 1), branch_spec(D, 2),
                  branch_spec(1, 0), branch_spec(1, 1), branch_spec(1, 2),
                  wb_spec, wb_spec, wb_spec],
        out_specs=pl.BlockSpec((tm, tn), lambda i, j: (i, j)),
        out_shape=jax.ShapeDtypeStruct((M, D), BF16),
        compiler_params=_params("parallel", "arbitrary"),
    )(u, o_a, o_b, o_c, w_gate, w_gate, w_gate, b_gate, b_gate, b_gate, wb_a, wb_b, wb_c)


def _out_proj_kernel(a_ref, w_ref, x_ref, o_ref):
    o_ref[...] = x_ref[...] + jnp.dot(a_ref[...], w_ref[...], preferred_element_type=F32)


def out_proj(a, w, x):
    M, K = a.shape
    N = w.shape[1]
    tm = _tile(M, 1024, BF16_SUBLANE)
    tn = _tile(N, 512, LANE)
    return pl.pallas_call(
        _out_proj_kernel,
        grid=(M // tm, N // tn),
        in_specs=[pl.BlockSpec((tm, K), lambda i, j: (i, 0)),
                  pl.BlockSpec((K, tn), lambda i, j: (0, j)),
                  pl.BlockSpec((tm, tn), lambda i, j: (i, j))],
        out_specs=pl.BlockSpec((tm, tn), lambda i, j: (i, j)),
        out_shape=jax.ShapeDtypeStruct((M, N), F32),
        compiler_params=_params("parallel", "arbitrary"),
    )(a, w, x)


def _ple_kernel(a_ref, wg_ref, p_ref, wp_ref, x_ref, o_ref):
    gate = jax.nn.sigmoid(jnp.dot(a_ref[...], wg_ref[...], preferred_element_type=F32))
    emb = jnp.dot(p_ref[...], wp_ref[...], preferred_element_type=F32)
    o_ref[...] = x_ref[...] + gate * emb


def ple(a, wg, p, wp, x):
    M, K = a.shape
    N = wg.shape[1]
    P = p.shape[1]
    tm = _tile(M, 1024, BF16_SUBLANE)
    tn = _tile(N, 512, LANE)
    return pl.pallas_call(
        _ple_kernel,
        grid=(M // tm, N // tn),
        in_specs=[pl.BlockSpec((tm, K), lambda i, j: (i, 0)),
                  pl.BlockSpec((K, tn), lambda i, j: (0, j)),
                  pl.BlockSpec((tm, P), lambda i, j: (i, 0)),
                  pl.BlockSpec((P, tn), lambda i, j: (0, j)),
                  pl.BlockSpec((tm, tn), lambda i, j: (i, j))],
        out_specs=pl.BlockSpec((tm, tn), lambda i, j: (i, j)),
        out_shape=jax.ShapeDtypeStruct((M, N), F32),
        compiler_params=_params("parallel", "arbitrary"),
    )(a, wg, p, wp, x)


SCALE = HEAD_DIM ** -0.5
KEY_TILE = 256
QUERY_TILE = 256
SHORT_QUERY_KEY_TILE = 2048
ASSEMBLE_PAGES = 8
INT32_MIN = -2 ** 31

_NT = (((1,), (1,)), ((), ()))


def _qk(q, k):
    return lax.dot_general(q, k, _NT, preferred_element_type=F32)


def _softmax_init(m_ref, l_ref, acc_ref):
    m_ref[...] = jnp.full(m_ref.shape, NEG, F32)
    l_ref[...] = jnp.zeros(l_ref.shape, F32)
    acc_ref[...] = jnp.zeros(acc_ref.shape, F32)


def _softmax_step(s, mask, v, m_ref, l_ref, acc_ref, r):
    s = jnp.where(mask, s, NEG)
    m_prev = m_ref[r]
    m_new = jnp.maximum(m_prev, jnp.max(s, axis=-1, keepdims=True))
    p = jnp.where(mask, jnp.exp(s - m_new), 0.0)
    alpha = jnp.exp(m_prev - m_new)
    l_ref[r] = alpha * l_ref[r] + jnp.sum(p, axis=-1, keepdims=True)
    acc_ref[r] = alpha * acc_ref[r] + jnp.dot(p.astype(BF16), v, preferred_element_type=F32)
    m_ref[r] = m_new


def _softmax_out(l_ref, acc_ref, r):
    return acc_ref[r] / jnp.maximum(l_ref[r], 1e-30)


def _positions(qpos0, kpos0, tq, tk):
    qpos = qpos0 + lax.broadcasted_iota(jnp.int32, (tq, tk), 0)
    kpos = kpos0 + lax.broadcasted_iota(jnp.int32, (tq, tk), 1)
    return qpos, kpos


def _lane_column(x, idx):
    lane = lax.broadcasted_iota(jnp.int32, x.shape, 1)
    return jnp.sum(jnp.where(lane == idx, x, 0.0), axis=-1, keepdims=True)


def _t5_bucket(n):
    n = jnp.maximum(n, 0)
    exact = N_BUCKETS // 2
    nf = jnp.maximum(n, 1).astype(jnp.float32)
    big = exact + (jnp.log(nf / exact) * ((N_BUCKETS - exact) / math.log(MAX_DISTANCE / exact))).astype(jnp.int32)
    return jnp.where(n < exact, n, jnp.minimum(big, N_BUCKETS - 1))


def _bias_lookup(tab, dist):
    onehot = jax.nn.one_hot(_t5_bucket(dist), N_BUCKETS, dtype=F32)
    return jnp.einsum('...n,nh->h...', onehot, tab.astype(F32), precision=lax.Precision.HIGHEST)


def _toeplitz_bias(tab, tq, tk, offsets):
    i = jnp.arange(tq)[:, None]
    j = jnp.arange(tk)[None, :]
    d = jnp.stack([off + i - j for off in offsets])
    return _bias_lookup(tab, d) - tab[N_BUCKETS - 1].astype(F32)[:, None, None, None]


class _Geom:
    def __init__(self, b, T, past_len):
        self.b, self.T, self.past_len = b, T, past_len
        self.tq = min(QUERY_TILE, T)
        self.tk = KEY_TILE if T >= QUERY_TILE else _tile(past_len, SHORT_QUERY_KEY_TILE, KEY_TILE)
        assert T % self.tq == 0 and self.tk % self.tq == 0 and past_len % self.tk == 0
        assert self.tq == self.tk or T == self.tq, "a query tile must not straddle key tiles"
        assert MAX_DISTANCE <= self.tk, "bias must be constant two key tiles behind the query tile"
        self.nqt = T // self.tq
        self.L = past_len + T
        self.nkt = (self.L - 1) // self.tk + 1

    def q_tile(self, qt):
        return (self.past_len + qt * self.tq) // self.tk


def _bias_spec(geom, rep):
    return pl.BlockSpec((rep, 1, geom.tq, geom.tk),
                        lambda i, g, q, k: (g, jnp.clip(geom.q_tile(q) - k, 0, 1), 0, 0))


def _assemble_kernel(pt_ref, *refs, n_in, n_page_steps, T, R):
    ins, new_ref, o_ref = refs[:n_in], refs[n_in], refs[n_in + 1]
    s = pl.program_id(1)

    @pl.when(s < n_page_steps)
    def _():
        for p in range(n_in):
            for c in range(R):
                rows = ins[p][0, 0, pl.ds(c, PAGE_SIZE, stride=R), :] if R > 1 else ins[p][0, 0]
                o_ref[0, p * PAGE_SIZE:(p + 1) * PAGE_SIZE, c * HEAD_DIM:(c + 1) * HEAD_DIM] = rows.astype(o_ref.dtype)

    @pl.when(s >= n_page_steps)
    def _():
        o_ref[0] = jnp.zeros(o_ref.shape[1:], o_ref.dtype)

    @pl.when(s == n_page_steps)
    def _():
        o_ref[0, :T, :] = new_ref[0]


def assemble_cache(cache, layer, page_table, new, pad_rows):
    b, n_pages = page_table.shape
    T, C = new.shape[1], new.shape[2]
    R = C // HEAD_DIM
    cache = cache.reshape(cache.shape[0], cache.shape[1], PAGE_SIZE * R, HEAD_DIM)
    P = _tile(n_pages, ASSEMBLE_PAGES, 1)
    steps = n_pages // P
    assert pad_rows % (P * PAGE_SIZE) == 0 and T <= P * PAGE_SIZE
    pad_steps = pad_rows // (P * PAGE_SIZE)

    def page_spec(p):
        return pl.BlockSpec((1, 1, PAGE_SIZE * R, HEAD_DIM),
                            lambda i, s, pt: (layer, pt[i, jnp.minimum(s * P + p, n_pages - 1)], 0, 0))

    grid_spec = pltpu.PrefetchScalarGridSpec(
        num_scalar_prefetch=1,
        grid=(b, steps + pad_steps),
        in_specs=[page_spec(p) for p in range(P)] + [pl.BlockSpec((1, T, C), lambda i, s, pt: (i, 0, 0))],
        out_specs=pl.BlockSpec((1, P * PAGE_SIZE, C), lambda i, s, pt: (i, s, 0)),
    )
    return pl.pallas_call(
        functools.partial(_assemble_kernel, n_in=P, n_page_steps=steps, T=T, R=R),
        grid_spec=grid_spec,
        out_shape=jax.ShapeDtypeStruct((b, (steps + pad_steps) * P * PAGE_SIZE, C), BF16),
        compiler_params=_params("parallel", "arbitrary"),
    )(page_table, *([cache] * P), new)


def _indexer_kernel(qi_ref, wi_ref, ki_ref, sc_ref, thr_ref, key_ref, *, geom, topk):
    tq, tk = geom.tq, geom.tk
    qt, kt = pl.program_id(1), pl.program_id(2)
    qpos0 = geom.past_len + qt * tq
    last = geom.q_tile(qt)

    @pl.when(kt == 0)
    def _():
        sc_ref[0] = jnp.full(sc_ref.shape[1:], NEG, F32)

    @pl.when(kt <= last)
    def _():
        ki = ki_ref[0]
        wi = wi_ref[0]
        acc = jnp.zeros((tq, tk), F32)
        for h in range(H_IDX):
            s = _qk(qi_ref[0, :, h * D_IDX:(h + 1) * D_IDX], ki)
            acc = acc + jnp.maximum(s, 0.0) * wi[:, h:h + 1]
        qpos, kpos = _positions(qpos0, kt * tk, tq, tk)
        sc_ref[0, kt] = jnp.where(kpos <= qpos, acc * (D_IDX ** -0.5 * H_IDX ** -0.5), NEG)

    @pl.when(kt == geom.nkt - 1)
    def _():
        i = lax.bitcast_convert_type(sc_ref[0], jnp.int32)
        key_ref[...] = i ^ ((i >> 31) & 0x7FFFFFFF)

        def body(bit, lo):
            cand = lo + jnp.left_shift(jnp.int32(1), 31 - bit)
            ge = (key_ref[...] >= cand[None]).astype(F32)
            cnt = jnp.sum(jnp.sum(ge, axis=0), axis=-1, keepdims=True)
            return jnp.where(cnt >= topk, cand, lo)

        lo = lax.fori_loop(0, 32, body, jnp.full((tq, 1), INT32_MIN, jnp.int32))
        thr = lax.bitcast_convert_type(lo ^ ((lo >> 31) & 0x7FFFFFFF), F32)
        thr_ref[0] = jnp.broadcast_to(thr, (tq, LANE))


def dsa_indexer(geom, qarr, qi_col, miscarr, misc_col, kiarr, ki_col):
    b, T, tq, tk, nkt = geom.b, geom.T, geom.tq, geom.tk, geom.nkt
    topk = min(TOPK_A, geom.L // 4)
    return pl.pallas_call(
        functools.partial(_indexer_kernel, geom=geom, topk=topk),
        grid=(b, geom.nqt, nkt),
        in_specs=[pl.BlockSpec((1, tq, H_IDX * D_IDX), lambda i, q, k: (i, q, qi_col)),
                  pl.BlockSpec((1, tq, LANE), lambda i, q, k: (i, q, misc_col)),
                  pl.BlockSpec((1, tk, D_IDX), lambda i, q, k: (i, jnp.minimum(k, geom.q_tile(q)), ki_col))],
        out_specs=[pl.BlockSpec((1, nkt, tq, tk), lambda i, q, k: (i, 0, q, 0)),
                   pl.BlockSpec((1, tq, LANE), lambda i, q, k: (i, q, 0))],
        out_shape=[jax.ShapeDtypeStruct((b, nkt, T, tk), F32), jax.ShapeDtypeStruct((b, T, LANE), F32)],
        scratch_shapes=[pltpu.VMEM((nkt, tq, tk), jnp.int32)],
        compiler_params=_params("parallel", "parallel", "arbitrary"),
    )(qarr, miscarr, kiarr)


def _dsa_attn_kernel(q_ref, k_ref, v_ref, sc_ref, thr_ref, bias_ref, o_ref, m_ref, l_ref, acc_ref, *, geom, rep):
    tq, tk = geom.tq, geom.tk
    qt, kt = pl.program_id(2), pl.program_id(3)
    qpos0 = geom.past_len + qt * tq
    last = geom.q_tile(qt)

    @pl.when(kt == 0)
    def _():
        _softmax_init(m_ref, l_ref, acc_ref)

    @pl.when(kt <= last)
    def _():
        qpos, kpos = _positions(qpos0, kt * tk, tq, tk)
        mask = (sc_ref[0, 0] >= thr_ref[0][:, :1]) & (kpos <= qpos)
        near = (last - kt) <= 1
        k, v = k_ref[0], v_ref[0]
        for r in range(rep):
            s = _qk(q_ref[0, :, r * HEAD_DIM:(r + 1) * HEAD_DIM], k) * SCALE + jnp.where(near, bias_ref[r, 0], 0.0)
            _softmax_step(s, mask, v, m_ref, l_ref, acc_ref, r)

    @pl.when(kt == geom.nkt - 1)
    def _():
        for r in range(rep):
            o_ref[0, :, r * HEAD_DIM:(r + 1) * HEAD_DIM] = _softmax_out(l_ref, acc_ref, r).astype(o_ref.dtype)


def dsa_attention(geom, qarr, q_col, karr, k_col, v_col, scores, thr, bias):
    b, T, tq, tk, nkt = geom.b, geom.T, geom.tq, geom.tk, geom.nkt
    rep = H_A // KV_A

    def kmap(col):
        return lambda i, g, q, k: (i, jnp.minimum(k, geom.q_tile(q)), col + g)

    return pl.pallas_call(
        functools.partial(_dsa_attn_kernel, geom=geom, rep=rep),
        grid=(b, KV_A, geom.nqt, nkt),
        in_specs=[pl.BlockSpec((1, tq, rep * HEAD_DIM), lambda i, g, q, k: (i, q, q_col + g)),
                  pl.BlockSpec((1, tk, HEAD_DIM), kmap(k_col)),
                  pl.BlockSpec((1, tk, HEAD_DIM), kmap(v_col)),
                  pl.BlockSpec((1, 1, tq, tk), lambda i, g, q, k: (i, jnp.minimum(k, geom.q_tile(q)), q, 0)),
                  pl.BlockSpec((1, tq, LANE), lambda i, g, q, k: (i, q, 0)),
                  _bias_spec(geom, rep)],
        out_specs=pl.BlockSpec((1, tq, rep * HEAD_DIM), lambda i, g, q, k: (i, q, g)),
        out_shape=jax.ShapeDtypeStruct((b, T, H_A * HEAD_DIM), BF16),
        scratch_shapes=[pltpu.VMEM((rep, tq, 1), F32), pltpu.VMEM((rep, tq, 1), F32),
                        pltpu.VMEM((rep, tq, HEAD_DIM), F32)],
        compiler_params=_params("parallel", "parallel", "parallel", "arbitrary"),
    )(qarr, karr, karr, scores, thr, bias)


def _moba_select_kernel(q_ref, k_ref, avg_ref, sel_ref, *, geom, nblk, ksel):
    qt = pl.program_id(2)
    cur = (geom.past_len + qt * geom.tq) // MOBA_BLOCK
    kmean = jnp.dot(avg_ref[...], k_ref[0], preferred_element_type=F32)
    gs = _qk(q_ref[0], kmean.astype(BF16))
    lane = lax.broadcasted_iota(jnp.int32, gs.shape, 1)
    past = lane < cur
    g = jnp.where(past, gs, NEG)
    cnt = jnp.zeros_like(g)
    for m in range(nblk):
        col = g[:, m:m + 1]
        cnt = cnt + ((col > g) | ((col == g) & (lane > m))).astype(F32)
    sel_ref[0, 0] = (past & (cnt < ksel)).astype(F32)


def moba_select(geom, qarr, q_col, karr, k_col):
    b, T, tq = geom.b, geom.T, geom.tq
    nblk = geom.L // MOBA_BLOCK
    assert 0 < nblk <= LANE and MOBA_BLOCK % tq == 0 and geom.tk % MOBA_BLOCK == 0
    ksel = min(MOBA_TOPK, nblk)
    rows = nblk * MOBA_BLOCK
    avg = np.zeros((LANE, rows), np.float32)
    avg[np.arange(rows) // MOBA_BLOCK, np.arange(rows)] = 1.0 / MOBA_BLOCK
    return pl.pallas_call(
        functools.partial(_moba_select_kernel, geom=geom, nblk=nblk, ksel=ksel),
        grid=(b, H_C, geom.nqt),
        in_specs=[pl.BlockSpec((1, tq, HEAD_DIM), lambda i, h, q: (i, q, q_col + h)),
                  pl.BlockSpec((1, rows, HEAD_DIM), lambda i, h, q: (i, 0, k_col + h)),
                  pl.BlockSpec((LANE, rows), lambda i, h, q: (0, 0))],
        out_specs=pl.BlockSpec((1, 1, tq, LANE), lambda i, h, q: (i, h, q, 0)),
        out_shape=jax.ShapeDtypeStruct((b, H_C, T, LANE), F32),
        compiler_params=_params("parallel", "parallel", "arbitrary"),
    )(qarr, karr, jnp.asarray(avg, BF16))


def _moba_attn_kernel(q_ref, k_ref, v_ref, sel_ref, bias_ref, o_ref, m_ref, l_ref, acc_ref, *, geom):
    tq, tk = geom.tq, geom.tk
    qt, kt = pl.program_id(2), pl.program_id(3)
    qpos0 = geom.past_len + qt * tq
    cur = geom.q_tile(qt)

    @pl.when(kt == 0)
    def _():
        _softmax_init(m_ref, l_ref, acc_ref)

    @pl.when(kt <= cur)
    def _():
        qpos, kpos = _positions(qpos0, kt * tk, tq, tk)
        own_blk = qpos0 // MOBA_BLOCK
        blk = lax.broadcasted_iota(jnp.int32, (LANE, tk), 0)
        tok = lax.broadcasted_iota(jnp.int32, (LANE, tk), 1)
        expand = (blk == kt * (tk // MOBA_BLOCK) + tok // MOBA_BLOCK).astype(BF16)
        chosen = jnp.dot(sel_ref[0, 0].astype(BF16), expand, preferred_element_type=F32) > 0.5
        mask = chosen | ((kpos // MOBA_BLOCK == own_blk) & (kpos <= qpos))
        s = _qk(q_ref[0], k_ref[0]) * SCALE + jnp.where((cur - kt) <= 1, bias_ref[0, 0], 0.0)
        _softmax_step(s, mask, v_ref[0], m_ref, l_ref, acc_ref, 0)

    @pl.when(kt == geom.nkt - 1)
    def _():
        o_ref[0] = _softmax_out(l_ref, acc_ref, 0).astype(o_ref.dtype)


def moba_attention(geom, qarr, q_col, karr, k_col, v_col, sel, bias):
    b, T, tq, tk, nkt = geom.b, geom.T, geom.tq, geom.tk, geom.nkt

    def kmap(col):
        return lambda i, h, q, k: (i, jnp.minimum(k, geom.q_tile(q)), col + h)

    return pl.pallas_call(
        functools.partial(_moba_attn_kernel, geom=geom),
        grid=(b, H_C, geom.nqt, nkt),
        in_specs=[pl.BlockSpec((1, tq, HEAD_DIM), lambda i, h, q, k: (i, q, q_col + h)),
                  pl.BlockSpec((1, tk, HEAD_DIM), kmap(k_col)),
                  pl.BlockSpec((1, tk, HEAD_DIM), kmap(v_col)),
                  pl.BlockSpec((1, 1, tq, LANE), lambda i, h, q, k: (i, h, q, 0)),
                  _bias_spec(geom, 1)],
        out_specs=pl.BlockSpec((1, tq, HEAD_DIM), lambda i, h, q, k: (i, q, h)),
        out_shape=jax.ShapeDtypeStruct((b, T, H_C * HEAD_DIM), BF16),
        scratch_shapes=[pltpu.VMEM((1, tq, 1), F32), pltpu.VMEM((1, tq, 1), F32), pltpu.VMEM((1, tq, HEAD_DIM), F32)],
        compiler_params=_params("parallel", "parallel", "parallel", "arbitrary"),
    )(qarr, karr, karr, sel, bias)


CMP_ROW = CMP_STRIDE * 2 * KV_B * HEAD_DIM


def _compress_kernel(x_ref, xn_ref, wlo_ref, whi_ref, c_ref, w2_ref, g_ref, o_ref, *, tr):
    x = x_ref[0]
    lo = jnp.dot(x, wlo_ref[...], preferred_element_type=F32)
    hi = jnp.dot(x, whi_ref[...], preferred_element_type=F32)
    hi_next = jnp.dot(xn_ref[0], whi_ref[...], preferred_element_type=F32)
    row = lax.broadcasted_iota(jnp.int32, hi.shape, 0)
    hi = jnp.where(row == tr - 1, hi_next[0:1], pltpu.roll(hi, tr - 1, 0))
    act = jax.nn.gelu(lo + hi + c_ref[...]).astype(BF16)
    for c in range(2 * KV_B):
        cols = slice(c * HEAD_DIM, (c + 1) * HEAD_DIM)
        y = jnp.dot(act[:, cols], w2_ref[c // KV_B], preferred_element_type=F32)
        if c < KV_B:
            y = y * lax.rsqrt(jnp.mean(y * y, axis=-1, keepdims=True) + EPS) * g_ref[...]
        o_ref[0, :, cols] = y.astype(o_ref.dtype)


def nsa_compress(x, wlo, whi, const, w2, gain):
    b, R, _ = x.shape
    tr = _tile(R, 128, BF16_SUBLANE)
    nxt = tr // BF16_SUBLANE
    n_sub16 = R // BF16_SUBLANE
    C = 2 * KV_B * HEAD_DIM
    return pl.pallas_call(
        functools.partial(_compress_kernel, tr=tr),
        grid=(b, R // tr),
        in_specs=[pl.BlockSpec((1, tr, CMP_ROW), lambda i, r: (i, r, 0)),
                  pl.BlockSpec((1, BF16_SUBLANE, CMP_ROW), lambda i, r: (i, jnp.minimum((r + 1) * nxt, n_sub16 - 1), 0)),
                  pl.BlockSpec((CMP_ROW, C), lambda i, r: (0, 0)),
                  pl.BlockSpec((CMP_ROW, C), lambda i, r: (0, 0)),
                  pl.BlockSpec((1, C), lambda i, r: (0, 0)),
                  pl.BlockSpec((2, HEAD_DIM, HEAD_DIM), lambda i, r: (0, 0, 0)),
                  pl.BlockSpec((1, HEAD_DIM), lambda i, r: (0, 0))],
        out_specs=pl.BlockSpec((1, tr, C), lambda i, r: (i, r, 0)),
        out_shape=jax.ShapeDtypeStruct((b, R, C), BF16),
        compiler_params=_params("parallel", "arbitrary"),
    )(x, x, wlo, whi, const, w2, gain)


def _prep_compress(cmp_pos, cmp_w1, cmp_w2, kn_cmp):
    r = CMP_LEN // CMP_STRIDE
    assert r == 2
    w1 = cmp_w1.reshape(2, r, CMP_STRIDE, HEAD_DIM, HEAD_DIM)
    kv_of_col = np.repeat(np.arange(2), KV_B)
    eye = jnp.eye(2 * KV_B, dtype=cmp_w1.dtype)

    def expand(half):
        w = w1[kv_of_col, half]
        return jnp.einsum('crde,cf->rcdfe', w, eye).reshape(CMP_ROW, 2 * KV_B * HEAD_DIM).astype(BF16)

    const = jnp.einsum('krd,krde->ke', cmp_pos.reshape(2, CMP_LEN, HEAD_DIM),
                       cmp_w1.reshape(2, CMP_LEN, HEAD_DIM, HEAD_DIM), precision=lax.Precision.HIGHEST)
    const = const[kv_of_col].reshape(1, -1).astype(F32)
    return dict(wlo=expand(0), whi=expand(1), const=const, w2=cmp_w2.astype(BF16), gain=kn_cmp.reshape(1, -1).astype(F32))


def _nsa_cmp_kernel(q_ref, kc_ref, vc_ref, bias_ref, ov_ref, oc_ref, sel_ref, *, geom, rep, n_slc, nsel):
    tq = geom.tq
    qt = pl.program_id(2)
    qpos0 = geom.past_len + qt * tq
    kc, vc, ov = kc_ref[0], vc_ref[0], ov_ref[...]
    ncp, nsp = ov.shape
    qpos, n_idx = _positions(qpos0, 0, tq, ncp)
    valid = (n_idx * CMP_STRIDE + (CMP_LEN - 1)) <= qpos
    imp = jnp.zeros((tq, nsp), F32)
    for r in range(rep):
        cols = slice(r * HEAD_DIM, (r + 1) * HEAD_DIM)
        s = jnp.where(valid, _qk(q_ref[0, :, cols], kc) * SCALE + bias_ref[r], NEG)
        e = jnp.where(valid, jnp.exp(s - jnp.max(s, axis=-1, keepdims=True)), 0.0)
        p = (e / jnp.maximum(jnp.sum(e, axis=-1, keepdims=True), 1e-30)).astype(BF16)
        oc_ref[0, :, cols] = jnp.dot(p, vc, preferred_element_type=F32)
        imp = imp + jnp.dot(p, ov, preferred_element_type=F32)
    spos, jj = _positions(qpos0, 0, tq, nsp)
    cur = spos // SLC_LEN
    forced = (jj == 0) | (jj == cur) | (jj == cur - 1)
    imp = jnp.where(jj > cur, NEG, jnp.where(forced, -NEG, imp))
    cnt = jnp.zeros_like(imp)
    for s_blk in range(n_slc):
        col = imp[:, s_blk:s_blk + 1]
        cnt = cnt + ((col > imp) | ((col == imp) & (jj > s_blk))).astype(F32)
    sel_ref[0, 0] = ((cnt < nsel) & (jj < n_slc)).astype(F32)


def nsa_compressed(geom, qarr, q_col, cmpkv, bias_cmp):
    b, T, tq = geom.b, geom.T, geom.tq
    rep = H_B // KV_B
    ncp = cmpkv.shape[1]
    n_slc = -(-geom.L // SLC_LEN)
    nsel = min(N_SLC, n_slc)
    nsp = _round_up(n_slc, LANE)
    n_cmp = geom.L // CMP_STRIDE - CMP_LEN // CMP_STRIDE + 1
    cstart = np.arange(ncp) * CMP_STRIDE
    cend = cstart + CMP_LEN - 1
    sstart = np.arange(nsp) * SLC_LEN
    ov = ((cstart[:, None] < sstart[None, :] + SLC_LEN) & (cend[:, None] >= sstart[None, :])
          & (np.arange(ncp)[:, None] < n_cmp) & (np.arange(nsp)[None, :] < n_slc)).astype(np.float32)
    return pl.pallas_call(
        functools.partial(_nsa_cmp_kernel, geom=geom, rep=rep, n_slc=n_slc, nsel=nsel),
        grid=(b, KV_B, geom.nqt),
        in_specs=[pl.BlockSpec((1, tq, rep * HEAD_DIM), lambda i, g, q: (i, q, q_col + g)),
                  pl.BlockSpec((1, ncp, HEAD_DIM), lambda i, g, q: (i, 0, g)),
                  pl.BlockSpec((1, ncp, HEAD_DIM), lambda i, g, q: (i, 0, KV_B + g)),
                  pl.BlockSpec((rep, tq, ncp), lambda i, g, q: (g, q, 0)),
                  pl.BlockSpec((ncp, nsp), lambda i, g, q: (0, 0))],
        out_specs=[pl.BlockSpec((1, tq, rep * HEAD_DIM), lambda i, g, q: (i, q, g)),
                   pl.BlockSpec((1, 1, tq, nsp), lambda i, g, q: (i, g, q, 0))],
        out_shape=[jax.ShapeDtypeStruct((b, T, H_B * HEAD_DIM), F32), jax.ShapeDtypeStruct((b, KV_B, T, nsp), F32)],
        compiler_params=_params("parallel", "parallel", "arbitrary"),
    )(qarr, cmpkv, cmpkv, bias_cmp, jnp.asarray(ov, BF16))


class _Window:
    def __init__(self, arr, k_col, v_col, tw, n_back, tile0, pos0, bias, bias_always):
        self.arr, self.k_col, self.v_col, self.tw, self.n_back = arr, k_col, v_col, tw, n_back
        self.tile0, self.pos0, self.bias, self.bias_always = tile0, pos0, bias, bias_always


def _nsa_attn_kernel(q_ref, ks_ref, vs_ref, kw_ref, vw_ref, sel_ref, bias_ref, wbias_ref, oc_ref, misc_ref, o_ref,
                     m_ref, l_ref, acc_ref, *, geom, rep, win):
    tq, tk = geom.tq, geom.tk
    g, qt, kt = pl.program_id(1), pl.program_id(2), pl.program_id(3)
    qpos0 = geom.past_len + qt * tq
    last = geom.q_tile(qt)

    @pl.when(kt == 0)
    def _():
        _softmax_init(m_ref, l_ref, acc_ref)

    def branch(k, v, mask, slot0, b_ref, always):
        near = (last - kt) <= 1
        for r in range(rep):
            bias = b_ref[r, 0] if always else jnp.where(near, b_ref[r, 0], 0.0)
            s = _qk(q_ref[0, :, r * HEAD_DIM:(r + 1) * HEAD_DIM], k) * SCALE + bias
            _softmax_step(s, mask, v, m_ref, l_ref, acc_ref, slot0 + r)

    @pl.when(kt <= last)
    def _():
        qpos, kpos = _positions(qpos0, kt * tk, tq, tk)
        nsp = sel_ref.shape[-1]
        blk = lax.broadcasted_iota(jnp.int32, (nsp, tk), 0)
        tok = lax.broadcasted_iota(jnp.int32, (nsp, tk), 1)
        expand = (blk == kt * (tk // SLC_LEN) + tok // SLC_LEN).astype(BF16)
        chosen = jnp.dot(sel_ref[0, 0].astype(BF16), expand, preferred_element_type=F32) > 0.5
        branch(ks_ref[0], vs_ref[0], chosen & (kpos <= qpos), 0, bias_ref, False)

    @pl.when((kt <= last) & (kt >= last - win.n_back))
    def _():
        qpos, kpos = _positions(qpos0, win.pos0 + (kt - win.tile0) * win.tw, tq, win.tw)
        dist = qpos - kpos
        branch(kw_ref[0], vw_ref[0], (dist >= 0) & (dist < WINDOW), rep, wbias_ref, win.bias_always)

    @pl.when(kt == geom.nkt - 1)
    def _():
        gates = jax.nn.sigmoid(misc_ref[0])
        for r in range(rep):
            cols = slice(r * HEAD_DIM, (r + 1) * HEAD_DIM)
            base = H_IDX + (g * rep + r) * N_BRANCH
            o = (_lane_column(gates, base) * oc_ref[0, :, cols]
                 + _lane_column(gates, base + 1) * _softmax_out(l_ref, acc_ref, r)
                 + _lane_column(gates, base + 2) * _softmax_out(l_ref, acc_ref, rep + r))
            o_ref[0, :, cols] = o.astype(o_ref.dtype)


def nsa_attention(geom, qarr, q_col, sarr, ks_col, vs_col, win, sel, bias, oc, miscarr, misc_col):
    b, T, tq, tk, nkt = geom.b, geom.T, geom.tq, geom.tk, geom.nkt
    rep = H_B // KV_B
    nsp = sel.shape[-1]
    tw = win.tw
    n_wt = win.arr.shape[1] // tw

    def smap(col):
        return lambda i, g, q, k: (i, jnp.minimum(k, geom.q_tile(q)), col + g)

    def wtile(q, k):
        last = geom.q_tile(q)
        return jnp.clip(jnp.clip(k, last - win.n_back, last) - win.tile0, 0, n_wt - 1)

    def wmap(col):
        return lambda i, g, q, k: (i, wtile(q, k), col + g)

    if win.bias_always:
        wbias_spec = pl.BlockSpec((rep, 1, tq, tw), lambda i, g, q, k: (g, jnp.minimum(wtile(q, k), win.bias.shape[1] - 1), 0, 0))
    else:
        wbias_spec = _bias_spec(geom, rep)

    return pl.pallas_call(
        functools.partial(_nsa_attn_kernel, geom=geom, rep=rep, win=win),
        grid=(b, KV_B, geom.nqt, nkt),
        in_specs=[pl.BlockSpec((1, tq, rep * HEAD_DIM), lambda i, g, q, k: (i, q, q_col + g)),
                  pl.BlockSpec((1, tk, HEAD_DIM), smap(ks_col)),
                  pl.BlockSpec((1, tk, HEAD_DIM), smap(vs_col)),
                  pl.BlockSpec((1, tw, HEAD_DIM), wmap(win.k_col)),
                  pl.BlockSpec((1, tw, HEAD_DIM), wmap(win.v_col)),
                  pl.BlockSpec((1, 1, tq, nsp), lambda i, g, q, k: (i, g, q, 0)),
                  _bias_spec(geom, rep),
                  wbias_spec,
                  pl.BlockSpec((1, tq, rep * HEAD_DIM), lambda i, g, q, k: (i, q, g)),
                  pl.BlockSpec((1, tq, LANE), lambda i, g, q, k: (i, q, misc_col))],
        out_specs=pl.BlockSpec((1, tq, rep * HEAD_DIM), lambda i, g, q, k: (i, q, g)),
        out_shape=jax.ShapeDtypeStruct((b, T, H_B * HEAD_DIM), BF16),
        scratch_shapes=[pltpu.VMEM((2 * rep, tq, 1), F32), pltpu.VMEM((2 * rep, tq, 1), F32),
                        pltpu.VMEM((2 * rep, tq, HEAD_DIM), F32)],
        compiler_params=_params("parallel", "parallel", "parallel", "arbitrary"),
    )(qarr, sarr, sarr, win.arr, win.arr, sel, bias, win.bias, oc, miscarr)


_SEG = {}
_off = 0
for _name, _w in (("qa", H_A * HEAD_DIM), ("qb", H_B * HEAD_DIM), ("qi", H_IDX * D_IDX), ("qc", H_C * HEAD_DIM),
                  ("kvc", 2 * H_C * HEAD_DIM), ("kva", 2 * KV_A * HEAD_DIM), ("cmp", 2 * KV_B * HEAD_DIM),
                  ("slc", 2 * KV_B * HEAD_DIM), ("win", 2 * KV_B * HEAD_DIM), ("ki", D_IDX), ("misc", LANE)):
    _SEG[_name] = (_off, _off + _w)
    _off += _w
IN_COLS_ALIGNED = _off


def _col(name, width=LANE):
    assert _SEG[name][0] % width == 0
    return _SEG[name][0] // width


def _prep_in_proj(w_in, qn_a, kn_a, qn_b, kn_b, qn_c, kn_c):
    sizes = (H_A * HEAD_DIM, KV_A * HEAD_DIM, KV_A * HEAD_DIM, H_IDX * D_IDX, H_IDX, D_IDX,
             H_B * HEAD_DIM, N_BRANCH * H_B) + (KV_B * HEAD_DIM,) * 6 + (H_C * HEAD_DIM,) * 3
    splits = np.cumsum(sizes)[:-1].tolist()
    (qa, ka, va, qi, wi, ki, q_b, g_b, kbc, vbc, kbs, vbs, kbw, vbw, qc, kc, vc) = jnp.split(w_in, splits, axis=-1)
    D = w_in.shape[0]
    misc = jnp.concatenate([wi, g_b, jnp.zeros((D, LANE - H_IDX - N_BRANCH * H_B), w_in.dtype)], axis=-1)
    n_pad = _round_up(IN_COLS_ALIGNED, 512) - IN_COLS_ALIGNED
    w = jnp.concatenate([qa, q_b, qi, qc, kc, vc, ka, va, kbc, vbc, kbs, vbs, kbw, vbw, ki, misc,
                         jnp.zeros((D, n_pad), w_in.dtype)], axis=-1).astype(BF16)
    one = jnp.ones((HEAD_DIM,), F32)

    def rep(g, n):
        return jnp.tile(g.astype(F32), n)

    gains = jnp.concatenate([
        rep(qn_a, H_A), rep(qn_b, H_B), rep(one, H_IDX), rep(qn_c, H_C), rep(kn_c, H_C), rep(one, H_C),
        rep(kn_a, KV_A), rep(one, KV_A), rep(one, 2 * KV_B), rep(kn_b[1], KV_B), rep(one, KV_B),
        rep(kn_b[2], KV_B), rep(one, KV_B), one, one, jnp.ones((n_pad,), F32)]).reshape(1, -1)
    flags = np.concatenate([
        np.ones(H_A), np.ones(H_B), np.zeros(H_IDX), np.ones(H_C), np.ones(H_C), np.zeros(H_C),
        np.ones(KV_A), np.zeros(KV_A), np.zeros(2 * KV_B), np.ones(KV_B), np.zeros(KV_B),
        np.ones(KV_B), np.zeros(KV_B), np.zeros(1), np.zeros(1), np.zeros(n_pad // LANE)]).astype(np.int32)
    return w, gains, jnp.asarray(flags)


def _prep_layer(i, ffn1_norm, ffn1_wg, ffn1_wu, ffn1_wd, mix_norm, w_in, qn_a, kn_a, qn_b, kn_b, cmp_pos, cmp_w1,
                cmp_w2, qn_c, kn_c, wb_a, wb_b, wb_c, w_gate, b_gate, w_out, ffn2_norm, ffn2_wg, ffn2_wu, ffn2_wd,
                ple_norm, ple_wg, ple_wp):
    F = ffn1_wg.shape[2]
    Fp = _round_up(F, 512) if F > 512 else F

    def up(w):
        return jnp.pad(w[i], ((0, 0), (0, Fp - F))).astype(BF16)

    def down(w):
        return jnp.pad(w[i], ((0, Fp - F), (0, 0))).astype(BF16)

    w_in_p, gains, flags = _prep_in_proj(w_in[i], qn_a[i], kn_a[i], qn_b[i], kn_b[i], qn_c[i], kn_c[i])
    return dict(
        f1n=ffn1_norm[i], f1g=up(ffn1_wg), f1u=up(ffn1_wu), f1d=down(ffn1_wd), mn=mix_norm[i],
        w_in=w_in_p, gains=gains, flags=flags,
        cmp=_prep_compress(cmp_pos[i], cmp_w1[i], cmp_w2[i], kn_b[i, 0]),
        wb_a=wb_a[i].astype(BF16), wb_b=wb_b[i].astype(BF16), wb_c=wb_c[i].astype(BF16),
        w_gate=w_gate[i].astype(BF16), b_gate=b_gate[i].reshape(1, -1), w_out=w_out[i].astype(BF16),
        f2n=ffn2_norm[i], f2g=up(ffn2_wg), f2u=up(ffn2_wu), f2d=down(ffn2_wd),
        pn=ple_norm[i], pwg=ple_wg[i].astype(BF16), pwp=ple_wp[i].astype(BF16))


def _layer(x, p, geom, layer, caches, W, bias):
    b, T = geom.b, geom.T
    M = b * T
    h = ffn_down(ffn_up(rmsnorm_bf16(x, W["f1n"]), W["f1g"], W["f1u"]), W["f1d"], x)
    u = rmsnorm_bf16(h, W["mn"])
    proj32, proj16 = in_proj(u, W["w_in"], W["gains"], W["flags"])
    Np = proj32.shape[1]
    p32 = proj32.reshape(b, T, Np)
    p16 = proj16.reshape(b, T, Np)

    def seg32(name):
        lo, hi = _SEG[name]
        return p32[:, :, lo:hi]

    def seg16(name):
        lo, hi = _SEG[name]
        return p16[:, :, lo:hi]

    win_new = seg32("win")
    if caches is None:
        assert T >= WINDOW
        a_kv, a_ki, slc, c_kv = (p16,) * 4
        a_k, a_v, ki_c = _col("kva"), _col("kva") + KV_A, _col("ki")
        s_k, s_v = _col("slc"), _col("slc") + KV_B
        c_k, c_v = _col("kvc"), _col("kvc") + H_C
        win = _Window(p16, _col("win"), _col("win") + KV_B, geom.tk, WINDOW // geom.tk, 0, 0, bias["b"], False)
        cmp_rows = seg16("cmp")
        win_state = win_new[:, T - WINDOW:]
    else:
        cache_a_kv, cache_a_kidx, cache_b_cmp, cache_b_slc, win_past, cache_c_kv, page_table = caches
        a_kv = assemble_cache(cache_a_kv, layer, page_table, seg16("kva"), geom.tk)
        a_ki = assemble_cache(cache_a_kidx, layer, page_table, seg16("ki"), geom.tk)
        cmp_rows = assemble_cache(cache_b_cmp, layer, page_table, seg16("cmp"), geom.tk)
        slc = assemble_cache(cache_b_slc, layer, page_table, seg16("slc"), geom.tk)
        c_kv = assemble_cache(cache_c_kv, layer, page_table, seg16("kvc"), geom.tk)
        a_k, a_v, ki_c, s_k, s_v, c_k, c_v = 0, KV_A, 0, 0, KV_B, 0, H_C
        n_buf = win_past.shape[2]
        win_past = win_past[layer].reshape(b, n_buf, -1)
        warr = jnp.concatenate([win_past.astype(BF16), seg16("win"),
                                jnp.zeros((b, KEY_TILE - T, win_past.shape[-1]), BF16)], axis=1)
        win = _Window(warr, 0, KV_B, warr.shape[1], 0, geom.q_tile(0), geom.past_len - n_buf, bias["b_win"], True)
        win_state = jnp.concatenate([win_past, win_new], axis=1)[:, -WINDOW:]

    scores, thr = dsa_indexer(geom, p16, _col("qi", H_IDX * D_IDX), p32, _col("misc"), a_ki, ki_c)
    o_a = dsa_attention(geom, p16, _col("qa", 4 * HEAD_DIM), a_kv, a_k, a_v, scores, thr, bias["a"])
    cmpkv = nsa_compress(cmp_rows.reshape(b, -1, CMP_ROW), **W["cmp"])
    qpos = geom.past_len + jnp.arange(T)
    cend = jnp.arange(cmpkv.shape[1]) * CMP_STRIDE + CMP_LEN - 1
    bias_cmp = _bias_lookup(bias["tab_b"], qpos[:, None] - cend[None, :])
    oc, sel_b = nsa_compressed(geom, p16, _col("qb", 4 * HEAD_DIM), cmpkv, bias_cmp)
    o_b = nsa_attention(geom, p16, _col("qb", 4 * HEAD_DIM), slc, s_k, s_v, win, sel_b, bias["b"], oc, p32, _col("misc"))
    sel_c = moba_select(geom, p16, _col("qc"), c_kv, c_k)
    o_c = moba_attention(geom, p16, _col("qc"), c_kv, c_k, c_v, sel_c, bias["c"])

    merged = gate_merge(u, o_a.reshape(M, -1), o_b.reshape(M, -1), o_c.reshape(M, -1),
                        W["w_gate"], W["b_gate"], W["wb_a"], W["wb_b"], W["wb_c"])
    h = out_proj(merged, W["w_out"], h)
    h = ffn_down(ffn_up(rmsnorm_bf16(h, W["f2n"]), W["f2g"], W["f2u"]), W["f2d"], h)
    h = ple(rmsnorm_bf16(h, W["pn"]), W["pwg"], p.astype(BF16), W["pwp"], h)
    state = (seg32("kva").reshape(b, T, 2, KV_A, HEAD_DIM), seg32("ki"), seg32("cmp").reshape(b, T, 2, KV_B, HEAD_DIM),
             seg32("slc").reshape(b, T, 2, KV_B, HEAD_DIM), win_state.reshape(b, WINDOW, 2, KV_B, HEAD_DIM),
             seg32("kvc").reshape(b, T, 2, H_C, HEAD_DIM))
    return h, state


def kernel(x_prompt, x_sample, cache_a_kv, cache_a_kidx, cache_b_cmp_kv, cache_b_slc_kv, state_b_win_kv, cache_c_kv, page_table, p_prompt, p_sample, rel_bias, ffn1_norm, ffn1_wg, ffn1_wu, ffn1_wd, mix_norm, w_in, qn_a, kn_a, qn_b, kn_b, cmp_pos, cmp_w1, cmp_w2, qn_c, kn_c, wb_a, wb_b, wb_c, w_gate, b_gate, w_out, ffn2_norm, ffn2_wg, ffn2_wu, ffn2_wd, ple_norm, ple_wg, ple_wp):
    depth = ffn1_norm.shape[0]
    bp, Tp, D = x_prompt.shape
    bs, Ts, _ = x_sample.shape
    geom_p = _Geom(bp, Tp, 0)
    geom_s = _Geom(bs, Ts, page_table.shape[1] * PAGE_SIZE)
    n_buf = state_b_win_kv.shape[2]
    assert n_buf >= WINDOW and Ts <= KEY_TILE
    tab_a, tab_b, tab_c = rel_bias[:, :H_A], rel_bias[:, H_A:H_A + H_B], rel_bias[:, H_A + H_B:]

    def tile_bias(geom):
        near = (0, geom.tk)
        return dict(a=_toeplitz_bias(tab_a, geom.tq, geom.tk, near), b=_toeplitz_bias(tab_b, geom.tq, geom.tk, near),
                    c=_toeplitz_bias(tab_c, geom.tq, geom.tk, near), tab_b=tab_b)

    bias_p = tile_bias(geom_p)
    bias_s = tile_bias(geom_s)
    bias_s["b_win"] = _toeplitz_bias(tab_b, geom_s.tq, n_buf + KEY_TILE, (n_buf,))
    caches = (cache_a_kv, cache_a_kidx, cache_b_cmp_kv, cache_b_slc_kv, state_b_win_kv, cache_c_kv, page_table)

    y_p = x_prompt.reshape(bp * Tp, D)
    y_s = x_sample.reshape(bs * Ts, D)
    sp_list, ss_list = [], []
    for i in range(depth):
        W = _prep_layer(i, ffn1_norm, ffn1_wg, ffn1_wu, ffn1_wd, mix_norm, w_in, qn_a, kn_a, qn_b, kn_b, cmp_pos,
                        cmp_w1, cmp_w2, qn_c, kn_c, wb_a, wb_b, wb_c, w_gate, b_gate, w_out, ffn2_norm, ffn2_wg,
                        ffn2_wu, ffn2_wd, ple_norm, ple_wg, ple_wp)
        y_p, sp = _layer(y_p, p_prompt[i].reshape(bp * Tp, -1), geom_p, i, None, W, bias_p)
        y_s, ss = _layer(y_s, p_sample[i].reshape(bs * Ts, -1), geom_s, i, caches, W, bias_s)
        sp_list.append(sp)
        ss_list.append(ss)

    outs = [y_p.reshape(bp, Tp, D), y_s.reshape(bs, Ts, D)]
    for j in range(6):
        outs.append(jnp.stack([s[j] for s in sp_list]))
        outs.append(jnp.stack([s[j] for s in ss_list]))
    return tuple(outs)
```

```python
import functools
import math

import numpy as np
import jax
import jax.numpy as jnp
from jax import lax
from jax.experimental import pallas as pl
from jax.experimental.pallas import tpu as pltpu

PAGE_SIZE = 128
HEAD_DIM = 128
H_A = 8
KV_A = 2
H_IDX = 16
D_IDX = 128
TOPK_A = 256
H_B = 8
KV_B = 2
CMP_LEN = 32
CMP_STRIDE = 16
SLC_LEN = 64
N_SLC = 16
WINDOW = 512
H_C = 8
MOBA_BLOCK = 256
MOBA_TOPK = 3
N_BUCKETS = 32
MAX_DISTANCE = 128
N_BRANCH = 3
EPS = 1e-6
NEG = -1e30

LANE = 128
BF16_SUBLANE = 16
V7X_VMEM_LIMIT = 56 * 1024 * 1024

F32 = jnp.float32
BF16 = jnp.bfloat16


def _tile(n, target, align):
    best = None
    for t in range(align, min(n, target) + 1, align):
        if n % t == 0:
            best = t
    return best if best is not None else n


def _round_up(n, m):
    return -(-n // m) * m


def _params(*sem):
    return pltpu.CompilerParams(dimension_semantics=sem, vmem_limit_bytes=V7X_VMEM_LIMIT)


def _rmsnorm_kernel(x_ref, g_ref, o_ref):
    x = x_ref[...]
    y = x * lax.rsqrt(jnp.mean(x * x, axis=-1, keepdims=True) + EPS)
    o_ref[...] = (y * g_ref[...]).astype(o_ref.dtype)


def rmsnorm_bf16(x, g):
    M, D = x.shape
    tm = _tile(M, 256, BF16_SUBLANE)
    return pl.pallas_call(
        _rmsnorm_kernel,
        grid=(M // tm,),
        in_specs=[pl.BlockSpec((tm, D), lambda i: (i, 0)), pl.BlockSpec((1, D), lambda i: (0, 0))],
        out_specs=pl.BlockSpec((tm, D), lambda i: (i, 0)),
        out_shape=jax.ShapeDtypeStruct((M, D), BF16),
        compiler_params=_params("parallel"),
    )(x, g.reshape(1, D))


def _ffn_up_kernel(a_ref, wg_ref, wu_ref, o_ref):
    a = a_ref[...]
    g = jnp.dot(a, wg_ref[...], preferred_element_type=F32)
    u = jnp.dot(a, wu_ref[...], preferred_element_type=F32)
    o_ref[...] = (g * jax.nn.sigmoid(g) * u).astype(o_ref.dtype)


def ffn_up(a, wg, wu):
    M, D = a.shape
    F = wg.shape[1]
    tm = _tile(M, 1024, BF16_SUBLANE)
    tn = _tile(F, 512, LANE)
    return pl.pallas_call(
        _ffn_up_kernel,
        grid=(M // tm, F // tn),
        in_specs=[pl.BlockSpec((tm, D), lambda i, j: (i, 0)),
                  pl.BlockSpec((D, tn), lambda i, j: (0, j)),
                  pl.BlockSpec((D, tn), lambda i, j: (0, j))],
        out_specs=pl.BlockSpec((tm, tn), lambda i, j: (i, j)),
        out_shape=jax.ShapeDtypeStruct((M, F), BF16),
        compiler_params=_params("parallel", "arbitrary"),
    )(a, wg, wu)


def _ffn_down_kernel(a_ref, w_ref, x_ref, o_ref, acc_ref):
    k = pl.program_id(2)

    @pl.when(k == 0)
    def _():
        acc_ref[...] = jnp.zeros_like(acc_ref)

    acc_ref[...] += jnp.dot(a_ref[...], w_ref[...], preferred_element_type=F32)

    @pl.when(k == pl.num_programs(2) - 1)
    def _():
        o_ref[...] = x_ref[...] + 0.5 * acc_ref[...]


def ffn_down(a, w, x):
    M, F = a.shape
    D = w.shape[1]
    tm = _tile(M, 1024, BF16_SUBLANE)
    tn = _tile(D, 1024, LANE)
    tk = _tile(F, 2816, LANE)
    return pl.pallas_call(
        _ffn_down_kernel,
        grid=(M // tm, D // tn, F // tk),
        in_specs=[pl.BlockSpec((tm, tk), lambda i, j, k: (i, k)),
                  pl.BlockSpec((tk, tn), lambda i, j, k: (k, j)),
                  pl.BlockSpec((tm, tn), lambda i, j, k: (i, j))],
        out_specs=pl.BlockSpec((tm, tn), lambda i, j, k: (i, j)),
        out_shape=jax.ShapeDtypeStruct((M, D), F32),
        scratch_shapes=[pltpu.VMEM((tm, tn), F32)],
        compiler_params=_params("parallel", "arbitrary", "arbitrary"),
    )(a, w, x)


def _w_in_kernel(flags_ref, a_ref, w_ref, g_ref, o_ref, o16_ref, *, heads_per_tile):
    j = pl.program_id(1)
    acc = jnp.dot(a_ref[...], w_ref[...], preferred_element_type=F32)
    for c in range(heads_per_tile):
        cols = slice(c * HEAD_DIM, (c + 1) * HEAD_DIM)
        y = acc[:, cols]
        yn = y * lax.rsqrt(jnp.mean(y * y, axis=-1, keepdims=True) + EPS) * g_ref[:, cols]
        y = jnp.where(flags_ref[j * heads_per_tile + c] > 0, yn, y)
        o_ref[:, cols] = y
        o16_ref[:, cols] = y.astype(o16_ref.dtype)


def in_proj(a, w, gains, flags):
    M, D = a.shape
    N = w.shape[1]
    tm = _tile(M, 1024, BF16_SUBLANE)
    tn = _tile(N, 512, LANE)
    grid_spec = pltpu.PrefetchScalarGridSpec(
        num_scalar_prefetch=1,
        grid=(M // tm, N // tn),
        in_specs=[pl.BlockSpec((tm, D), lambda i, j, f: (i, 0)),
                  pl.BlockSpec((D, tn), lambda i, j, f: (0, j)),
                  pl.BlockSpec((1, tn), lambda i, j, f: (0, j))],
        out_specs=[pl.BlockSpec((tm, tn), lambda i, j, f: (i, j)), pl.BlockSpec((tm, tn), lambda i, j, f: (i, j))],
    )
    return pl.pallas_call(
        functools.partial(_w_in_kernel, heads_per_tile=tn // HEAD_DIM),
        grid_spec=grid_spec,
        out_shape=[jax.ShapeDtypeStruct((M, N), F32), jax.ShapeDtypeStruct((M, N), BF16)],
        compiler_params=_params("parallel", "arbitrary"),
    )(flags, a, w, gains)


def _gate_merge_kernel(u_ref, oa_ref, ob_ref, oc_ref, wg0_ref, wg1_ref, wg2_ref, bg0_ref, bg1_ref, bg2_ref,
                       wba_ref, wbb_ref, wbc_ref, o_ref):
    u = u_ref[...]
    out = None
    for o_r, wg_r, bg_r, wb_r in ((oa_ref, wg0_ref, bg0_ref, wba_ref),
                                  (ob_ref, wg1_ref, bg1_ref, wbb_ref),
                                  (oc_ref, wg2_ref, bg2_ref, wbc_ref)):
        gate = jax.nn.sigmoid(jnp.dot(u, wg_r[...], preferred_element_type=F32) + bg_r[...])
        branch = jnp.dot(o_r[...], wb_r[...], preferred_element_type=F32)
        out = gate * branch if out is None else out + gate * branch
    o_ref[...] = out.astype(o_ref.dtype)


def gate_merge(u, o_a, o_b, o_c, w_gate, b_gate, wb_a, wb_b, wb_c):
    M, D = u.shape
    Ho = o_a.shape[1]
    tm = _tile(M, 1024, BF16_SUBLANE)
    tn = _tile(D, 256, LANE)
    nj = D // tn
    a_spec = pl.BlockSpec((tm, D), lambda i, j: (i, 0))
    o_spec = pl.BlockSpec((tm, Ho), lambda i, j: (i, 0))
    wb_spec = pl.BlockSpec((Ho, tn), lambda i, j: (0, j))

    def branch_spec(rows, br):
        return pl.BlockSpec((rows, tn), lambda i, j: (0, br * nj + j))

    return pl.pallas_call(
        _gate_merge_kernel,
        grid=(M // tm, nj),
        in_specs=[a_spec, o_spec, o_spec, o_spec,
                  branch_spec(D, 0), branch_spec(D, 1), branch_spec(D, 2),
                  branch_spec(1, 0), branch_spec(1, 1), branch_spec(1, 2),
                  wb_spec, wb_spec, wb_spec],
        out_specs=pl.BlockSpec((tm, tn), lambda i, j: (i, j)),
        out_shape=jax.ShapeDtypeStruct((M, D), BF16),
        compiler_params=_params("parallel", "arbitrary"),
    )(u, o_a, o_b, o_c, w_gate, w_gate, w_gate, b_gate, b_gate, b_gate, wb_a, wb_b, wb_c)


def _out_proj_kernel(a_ref, w_ref, x_ref, o_ref):
    o_ref[...] = x_ref[...] + jnp.dot(a_ref[...], w_ref[...], preferred_element_type=F32)


def out_proj(a, w, x):
    M, K = a.shape
    N = w.shape[1]
    tm = _tile(M, 1024, BF16_SUBLANE)
    tn = _tile(N, 512, LANE)
    return pl.pallas_call(
        _out_proj_kernel,
        grid=(M // tm, N // tn),
        in_specs=[pl.BlockSpec((tm, K), lambda i, j: (i, 0)),
                  pl.BlockSpec((K, tn), lambda i, j: (0, j)),
                  pl.BlockSpec((tm, tn), lambda i, j: (i, j))],
        out_specs=pl.BlockSpec((tm, tn), lambda i, j: (i, j)),
        out_shape=jax.ShapeDtypeStruct((M, N), F32),
        compiler_params=_params("parallel", "arbitrary"),
    )(a, w, x)


def _ple_kernel(a_ref, wg_ref, p_ref, wp_ref, x_ref, o_ref):
    gate = jax.nn.sigmoid(jnp.dot(a_ref[...], wg_ref[...], preferred_element_type=F32))
    emb = jnp.dot(p_ref[...], wp_ref[...], preferred_element_type=F32)
    o_ref[...] = x_ref[...] + gate * emb


def ple(a, wg, p, wp, x):
    M, K = a.shape
    N = wg.shape[1]
    P = p.shape[1]
    tm = _tile(M, 1024, BF16_SUBLANE)
    tn = _tile(N, 512, LANE)
    return pl.pallas_call(
        _ple_kernel,
        grid=(M // tm, N // tn),
        in_specs=[pl.BlockSpec((tm, K), lambda i, j: (i, 0)),
                  pl.BlockSpec((K, tn), lambda i, j: (0, j)),
                  pl.BlockSpec((tm, P), lambda i, j: (i, 0)),
                  pl.BlockSpec((P, tn), lambda i, j: (0, j)),
                  pl.BlockSpec((tm, tn), lambda i, j: (i, j))],
        out_specs=pl.BlockSpec((tm, tn), lambda i, j: (i, j)),
        out_shape=jax.ShapeDtypeStruct((M, N), F32),
        compiler_params=_params("parallel", "arbitrary"),
    )(a, wg, p, wp, x)


SCALE = HEAD_DIM ** -0.5
KEY_TILE = 256
QUERY_TILE = 256
SHORT_QUERY_KEY_TILE = 2048
ASSEMBLE_PAGES = 8
INT32_MIN = -2 ** 31

_NT = (((1,), (1,)), ((), ()))


def _qk(q, k):
    return lax.dot_general(q, k, _NT, preferred_element_type=F32)


def _softmax_init(m_ref, l_ref, acc_ref):
    m_ref[...] = jnp.full(m_ref.shape, NEG, F32)
    l_ref[...] = jnp.zeros(l_ref.shape, F32)
    acc_ref[...] = jnp.zeros(acc_ref.shape, F32)


def _softmax_step(s, mask, v, m_ref, l_ref, acc_ref, r):
    s = jnp.where(mask, s, NEG)
    m_prev = m_ref[r]
    m_new = jnp.maximum(m_prev, jnp.max(s, axis=-1, keepdims=True))
    p = jnp.where(mask, jnp.exp(s - m_new), 0.0)
    alpha = jnp.exp(m_prev - m_new)
    l_ref[r] = alpha * l_ref[r] + jnp.sum(p, axis=-1, keepdims=True)
    acc_ref[r] = alpha * acc_ref[r] + jnp.dot(p.astype(BF16), v, preferred_element_type=F32)
    m_ref[r] = m_new


def _softmax_out(l_ref, acc_ref, r):
    return acc_ref[r] / jnp.maximum(l_ref[r], 1e-30)


def _positions(qpos0, kpos0, tq, tk):
    qpos = qpos0 + lax.broadcasted_iota(jnp.int32, (tq, tk), 0)
    kpos = kpos0 + lax.broadcasted_iota(jnp.int32, (tq, tk), 1)
    return qpos, kpos


def _lane_column(x, idx):
    lane = lax.broadcasted_iota(jnp.int32, x.shape, 1)
    return jnp.sum(jnp.where(lane == idx, x, 0.0), axis=-1, keepdims=True)


def _t5_bucket(n):
    n = jnp.maximum(n, 0)
    exact = N_BUCKETS // 2
    nf = jnp.maximum(n, 1).astype(jnp.float32)
    big = exact + (jnp.log(nf / exact) * ((N_BUCKETS - exact) / math.log(MAX_DISTANCE / exact))).astype(jnp.int32)
    return jnp.where(n < exact, n, jnp.minimum(big, N_BUCKETS - 1))


def _bias_lookup(tab, dist):
    onehot = jax.nn.one_hot(_t5_bucket(dist), N_BUCKETS, dtype=F32)
    return jnp.einsum('...n,nh->h...', onehot, tab.astype(F32), precision=lax.Precision.HIGHEST)


def _toeplitz_bias(tab, tq, tk, offsets):
    i = jnp.arange(tq)[:, None]
    j = jnp.arange(tk)[None, :]
    d = jnp.stack([off + i - j for off in offsets])
    return _bias_lookup(tab, d) - tab[N_BUCKETS - 1].astype(F32)[:, None, None, None]


class _Geom:
    def __init__(self, b, T, past_len):
        self.b, self.T, self.past_len = b, T, past_len
        self.tq = min(QUERY_TILE, T)
        self.tk = KEY_TILE if T >= QUERY_TILE else _tile(past_len, SHORT_QUERY_KEY_TILE, KEY_TILE)
        assert T % self.tq == 0 and self.tk % self.tq == 0 and past_len % self.tk == 0
        assert self.tq == self.tk or T == self.tq, "a query tile must not straddle key tiles"
        assert MAX_DISTANCE <= self.tk, "bias must be constant two key tiles behind the query tile"
        self.nqt = T // self.tq
        self.L = past_len + T
        self.nkt = (self.L - 1) // self.tk + 1

    def q_tile(self, qt):
        return (self.past_len + qt * self.tq) // self.tk


def _bias_spec(geom, heads):
    return pl.BlockSpec((heads, 1, geom.tq, geom.tk),
                        lambda i, q, k: (0, jnp.clip(geom.q_tile(q) - k, 0, 1), 0, 0))


def _assemble_kernel(pt_ref, *refs, n_in, n_page_steps, T, R):
    ins, new_ref, o_ref = refs[:n_in], refs[n_in], refs[n_in + 1]
    s = pl.program_id(1)

    @pl.when(s < n_page_steps)
    def _():
        for p in range(n_in):
            for c in range(R):
                rows = ins[p][0, 0, pl.ds(c, PAGE_SIZE, stride=R), :] if R > 1 else ins[p][0, 0]
                o_ref[0, p * PAGE_SIZE:(p + 1) * PAGE_SIZE, c * HEAD_DIM:(c + 1) * HEAD_DIM] = rows.astype(o_ref.dtype)

    @pl.when(s >= n_page_steps)
    def _():
        o_ref[0] = jnp.zeros(o_ref.shape[1:], o_ref.dtype)

    @pl.when(s == n_page_steps)
    def _():
        o_ref[0, :T, :] = new_ref[0]


def assemble_cache(cache, layer, page_table, new, pad_rows):
    b, n_pages = page_table.shape
    T, C = new.shape[1], new.shape[2]
    R = C // HEAD_DIM
    cache = cache.reshape(cache.shape[0], cache.shape[1], PAGE_SIZE * R, HEAD_DIM)
    P = _tile(n_pages, ASSEMBLE_PAGES, 1)
    steps = n_pages // P
    assert pad_rows % (P * PAGE_SIZE) == 0 and T <= P * PAGE_SIZE
    pad_steps = pad_rows // (P * PAGE_SIZE)

    def page_spec(p):
        return pl.BlockSpec((1, 1, PAGE_SIZE * R, HEAD_DIM),
                            lambda i, s, pt: (layer, pt[i, jnp.minimum(s * P + p, n_pages - 1)], 0, 0))

    grid_spec = pltpu.PrefetchScalarGridSpec(
        num_scalar_prefetch=1,
        grid=(b, steps + pad_steps),
        in_specs=[page_spec(p) for p in range(P)] + [pl.BlockSpec((1, T, C), lambda i, s, pt: (i, 0, 0))],
        out_specs=pl.BlockSpec((1, P * PAGE_SIZE, C), lambda i, s, pt: (i, s, 0)),
    )
    return pl.pallas_call(
        functools.partial(_assemble_kernel, n_in=P, n_page_steps=steps, T=T, R=R),
        grid_spec=grid_spec,
        out_shape=jax.ShapeDtypeStruct((b, (steps + pad_steps) * P * PAGE_SIZE, C), BF16),
        compiler_params=_params("parallel", "arbitrary"),
    )(page_table, *([cache] * P), new)


def _indexer_kernel(qi_ref, wi_ref, ki_ref, sc_ref, thr_ref, key_ref, *, geom, topk):
    tq, tk = geom.tq, geom.tk
    qt, kt = pl.program_id(1), pl.program_id(2)
    qpos0 = geom.past_len + qt * tq
    last = geom.q_tile(qt)

    @pl.when(kt == 0)
    def _():
        sc_ref[0] = jnp.full(sc_ref.shape[1:], NEG, F32)

    @pl.when(kt <= last)
    def _():
        ki = ki_ref[0]
        wi = wi_ref[0]
        acc = jnp.zeros((tq, tk), F32)
        for h in range(H_IDX):
            s = _qk(qi_ref[0, :, h * D_IDX:(h + 1) * D_IDX], ki)
            acc = acc + jnp.maximum(s, 0.0) * wi[:, h:h + 1]
        qpos, kpos = _positions(qpos0, kt * tk, tq, tk)
        score = jnp.where(kpos <= qpos, acc * (D_IDX ** -0.5 * H_IDX ** -0.5), NEG)
        sc_ref[0, kt] = score
        i = lax.bitcast_convert_type(score, jnp.int32)
        key_ref[kt] = i ^ ((i >> 31) & 0x7FFFFFFF)

    @pl.when(kt == geom.nkt - 1)
    def _():
        def body(bit, lo):
            cand = lo + jnp.left_shift(jnp.int32(1), 31 - bit)
            ge = lax.fori_loop(0, last + 1, lambda j, c: c + (key_ref[j] >= cand).astype(F32),
                               jnp.zeros((tq, tk), F32))
            cnt = jnp.sum(ge, axis=-1, keepdims=True)
            return jnp.where(cnt >= topk, cand, lo)

        lo = lax.fori_loop(0, 32, body, jnp.full((tq, 1), INT32_MIN, jnp.int32))
        thr = lax.bitcast_convert_type(lo ^ ((lo >> 31) & 0x7FFFFFFF), F32)
        thr_ref[0] = jnp.broadcast_to(thr, (tq, LANE))


def dsa_indexer(geom, qarr, qi_col, miscarr, misc_col, kiarr, ki_col):
    b, T, tq, tk, nkt = geom.b, geom.T, geom.tq, geom.tk, geom.nkt
    topk = min(TOPK_A, geom.L // 4)
    assert topk <= tk, "the first causal key tile alone must hold topk entries"
    return pl.pallas_call(
        functools.partial(_indexer_kernel, geom=geom, topk=topk),
        grid=(b, geom.nqt, nkt),
        in_specs=[pl.BlockSpec((1, tq, H_IDX * D_IDX), lambda i, q, k: (i, q, qi_col)),
                  pl.BlockSpec((1, tq, LANE), lambda i, q, k: (i, q, misc_col)),
                  pl.BlockSpec((1, tk, D_IDX), lambda i, q, k: (i, jnp.minimum(k, geom.q_tile(q)), ki_col))],
        out_specs=[pl.BlockSpec((1, nkt, tq, tk), lambda i, q, k: (i, 0, q, 0)),
                   pl.BlockSpec((1, tq, LANE), lambda i, q, k: (i, q, 0))],
        out_shape=[jax.ShapeDtypeStruct((b, nkt, T, tk), F32), jax.ShapeDtypeStruct((b, T, LANE), F32)],
        scratch_shapes=[pltpu.VMEM((nkt, tq, tk), jnp.int32)],
        compiler_params=_params("parallel", "parallel", "arbitrary"),
    )(qarr, miscarr, kiarr)


def _dsa_attn_kernel(q_ref, k_ref, v_ref, sc_ref, thr_ref, bias_ref, o_ref, m_ref, l_ref, acc_ref, *, geom, rep):
    tq, tk = geom.tq, geom.tk
    qt, kt = pl.program_id(1), pl.program_id(2)
    qpos0 = geom.past_len + qt * tq
    last = geom.q_tile(qt)

    @pl.when(kt == 0)
    def _():
        _softmax_init(m_ref, l_ref, acc_ref)

    @pl.when(kt <= last)
    def _():
        qpos, kpos = _positions(qpos0, kt * tk, tq, tk)
        mask = (sc_ref[0, 0] >= thr_ref[0][:, :1]) & (kpos <= qpos)
        near = (last - kt) <= 1
        for h in range(H_A):
            kv_cols = slice((h // rep) * HEAD_DIM, (h // rep + 1) * HEAD_DIM)
            s = (_qk(q_ref[0, :, h * HEAD_DIM:(h + 1) * HEAD_DIM], k_ref[0, :, kv_cols]) * SCALE
                 + jnp.where(near, bias_ref[h, 0], 0.0))
            _softmax_step(s, mask, v_ref[0, :, kv_cols], m_ref, l_ref, acc_ref, h)

    @pl.when(kt == geom.nkt - 1)
    def _():
        for h in range(H_A):
            o_ref[0, :, h * HEAD_DIM:(h + 1) * HEAD_DIM] = _softmax_out(l_ref, acc_ref, h).astype(o_ref.dtype)


def dsa_attention(geom, qarr, q_col, karr, k_col, v_col, scores, thr, bias):
    b, T, tq, tk, nkt = geom.b, geom.T, geom.tq, geom.tk, geom.nkt

    def kmap(col):
        return lambda i, q, k: (i, jnp.minimum(k, geom.q_tile(q)), col)

    return pl.pallas_call(
        functools.partial(_dsa_attn_kernel, geom=geom, rep=H_A // KV_A),
        grid=(b, geom.nqt, nkt),
        in_specs=[pl.BlockSpec((1, tq, H_A * HEAD_DIM), lambda i, q, k: (i, q, q_col)),
                  pl.BlockSpec((1, tk, KV_A * HEAD_DIM), kmap(k_col)),
                  pl.BlockSpec((1, tk, KV_A * HEAD_DIM), kmap(v_col)),
                  pl.BlockSpec((1, 1, tq, tk), lambda i, q, k: (i, jnp.minimum(k, geom.q_tile(q)), q, 0)),
                  pl.BlockSpec((1, tq, LANE), lambda i, q, k: (i, q, 0)),
                  _bias_spec(geom, H_A)],
        out_specs=pl.BlockSpec((1, tq, H_A * HEAD_DIM), lambda i, q, k: (i, q, 0)),
        out_shape=jax.ShapeDtypeStruct((b, T, H_A * HEAD_DIM), BF16),
        scratch_shapes=[pltpu.VMEM((H_A, tq, 1), F32), pltpu.VMEM((H_A, tq, 1), F32),
                        pltpu.VMEM((H_A, tq, HEAD_DIM), F32)],
        compiler_params=_params("parallel", "parallel", "arbitrary"),
    )(qarr, karr, karr, scores, thr, bias)


def _moba_select_kernel(q_ref, k_ref, avg_ref, sel_ref, *, geom, nblk, ksel):
    qt = pl.program_id(2)
    cur = (geom.past_len + qt * geom.tq) // MOBA_BLOCK
    kmean = jnp.dot(avg_ref[...], k_ref[0], preferred_element_type=F32)
    gs = _qk(q_ref[0], kmean.astype(BF16))
    lane = lax.broadcasted_iota(jnp.int32, gs.shape, 1)
    past = lane < cur
    g = jnp.where(past, gs, NEG)
    cnt = jnp.zeros_like(g)
    for m in range(nblk):
        col = g[:, m:m + 1]
        cnt = cnt + ((col > g) | ((col == g) & (lane > m))).astype(F32)
    sel_ref[0, 0] = (past & (cnt < ksel)).astype(F32)


def moba_select(geom, qarr, q_col, karr, k_col):
    b, T, tq = geom.b, geom.T, geom.tq
    nblk = geom.L // MOBA_BLOCK
    assert 0 < nblk <= LANE and MOBA_BLOCK % tq == 0 and geom.tk % MOBA_BLOCK == 0
    ksel = min(MOBA_TOPK, nblk)
    rows = nblk * MOBA_BLOCK
    avg = np.zeros((LANE, rows), np.float32)
    avg[np.arange(rows) // MOBA_BLOCK, np.arange(rows)] = 1.0 / MOBA_BLOCK
    return pl.pallas_call(
        functools.partial(_moba_select_kernel, geom=geom, nblk=nblk, ksel=ksel),
        grid=(b, H_C, geom.nqt),
        in_specs=[pl.BlockSpec((1, tq, HEAD_DIM), lambda i, h, q: (i, q, q_col + h)),
                  pl.BlockSpec((1, rows, HEAD_DIM), lambda i, h, q: (i, 0, k_col + h)),
                  pl.BlockSpec((LANE, rows), lambda i, h, q: (0, 0))],
        out_specs=pl.BlockSpec((1, 1, tq, LANE), lambda i, h, q: (i, h, q, 0)),
        out_shape=jax.ShapeDtypeStruct((b, H_C, T, LANE), F32),
        compiler_params=_params("parallel", "parallel", "arbitrary"),
    )(qarr, karr, jnp.asarray(avg, BF16))


def _moba_attn_kernel(q_ref, k_ref, v_ref, sel_ref, bias_ref, o_ref, m_ref, l_ref, acc_ref, *, geom):
    tq, tk = geom.tq, geom.tk
    qt, kt = pl.program_id(1), pl.program_id(2)
    qpos0 = geom.past_len + qt * tq
    cur = geom.q_tile(qt)

    @pl.when(kt == 0)
    def _():
        _softmax_init(m_ref, l_ref, acc_ref)

    @pl.when(kt <= cur)
    def _():
        qpos, kpos = _positions(qpos0, kt * tk, tq, tk)
        own_blk = qpos0 // MOBA_BLOCK
        blk = lax.broadcasted_iota(jnp.int32, (LANE, tk), 0)
        tok = lax.broadcasted_iota(jnp.int32, (LANE, tk), 1)
        expand = (blk == kt * (tk // MOBA_BLOCK) + tok // MOBA_BLOCK).astype(BF16)
        own = (kpos // MOBA_BLOCK == own_blk) & (kpos <= qpos)
        near = (cur - kt) <= 1
        for h in range(H_C):
            cols = slice(h * HEAD_DIM, (h + 1) * HEAD_DIM)
            chosen = jnp.dot(sel_ref[0, h].astype(BF16), expand, preferred_element_type=F32) > 0.5
            s = _qk(q_ref[0, :, cols], k_ref[0, :, cols]) * SCALE + jnp.where(near, bias_ref[h, 0], 0.0)
            _softmax_step(s, chosen | own, v_ref[0, :, cols], m_ref, l_ref, acc_ref, h)

    @pl.when(kt == geom.nkt - 1)
    def _():
        for h in range(H_C):
            o_ref[0, :, h * HEAD_DIM:(h + 1) * HEAD_DIM] = _softmax_out(l_ref, acc_ref, h).astype(o_ref.dtype)


def moba_attention(geom, qarr, q_col, karr, k_col, v_col, sel, bias):
    b, T, tq, tk, nkt = geom.b, geom.T, geom.tq, geom.tk, geom.nkt
    width = H_C * HEAD_DIM

    def kmap(col):
        return lambda i, q, k: (i, jnp.minimum(k, geom.q_tile(q)), col)

    return pl.pallas_call(
        functools.partial(_moba_attn_kernel, geom=geom),
        grid=(b, geom.nqt, nkt),
        in_specs=[pl.BlockSpec((1, tq, width), lambda i, q, k: (i, q, q_col)),
                  pl.BlockSpec((1, tk, width), kmap(k_col)),
                  pl.BlockSpec((1, tk, width), kmap(v_col)),
                  pl.BlockSpec((1, H_C, tq, LANE), lambda i, q, k: (i, 0, q, 0)),
                  _bias_spec(geom, H_C)],
        out_specs=pl.BlockSpec((1, tq, width), lambda i, q, k: (i, q, 0)),
        out_shape=jax.ShapeDtypeStruct((b, T, width), BF16),
        scratch_shapes=[pltpu.VMEM((H_C, tq, 1), F32), pltpu.VMEM((H_C, tq, 1), F32),
                        pltpu.VMEM((H_C, tq, HEAD_DIM), F32)],
        compiler_params=_params("parallel", "parallel", "arbitrary"),
    )(qarr, karr, karr, sel, bias)


CMP_ROW = CMP_STRIDE * 2 * KV_B * HEAD_DIM


def _compress_kernel(x_ref, xn_ref, wlo_ref, whi_ref, c_ref, w2_ref, g_ref, o_ref, *, tr):
    x = x_ref[0]
    lo = jnp.dot(x, wlo_ref[...], preferred_element_type=F32)
    hi = jnp.dot(x, whi_ref[...], preferred_element_type=F32)
    hi_next = jnp.dot(xn_ref[0], whi_ref[...], preferred_element_type=F32)
    row = lax.broadcasted_iota(jnp.int32, hi.shape, 0)
    hi = jnp.where(row == tr - 1, hi_next[0:1], pltpu.roll(hi, tr - 1, 0))
    act = jax.nn.gelu(lo + hi + c_ref[...]).astype(BF16)
    for c in range(2 * KV_B):
        cols = slice(c * HEAD_DIM, (c + 1) * HEAD_DIM)
        y = jnp.dot(act[:, cols], w2_ref[c // KV_B], preferred_element_type=F32)
        if c < KV_B:
            y = y * lax.rsqrt(jnp.mean(y * y, axis=-1, keepdims=True) + EPS) * g_ref[...]
        o_ref[0, :, cols] = y.astype(o_ref.dtype)


def nsa_compress(x, wlo, whi, const, w2, gain):
    b, R, _ = x.shape
    tr = _tile(R, 128, BF16_SUBLANE)
    nxt = tr // BF16_SUBLANE
    n_sub16 = R // BF16_SUBLANE
    C = 2 * KV_B * HEAD_DIM
    return pl.pallas_call(
        functools.partial(_compress_kernel, tr=tr),
        grid=(b, R // tr),
        in_specs=[pl.BlockSpec((1, tr, CMP_ROW), lambda i, r: (i, r, 0)),
                  pl.BlockSpec((1, BF16_SUBLANE, CMP_ROW), lambda i, r: (i, jnp.minimum((r + 1) * nxt, n_sub16 - 1), 0)),
                  pl.BlockSpec((CMP_ROW, C), lambda i, r: (0, 0)),
                  pl.BlockSpec((CMP_ROW, C), lambda i, r: (0, 0)),
                  pl.BlockSpec((1, C), lambda i, r: (0, 0)),
                  pl.BlockSpec((2, HEAD_DIM, HEAD_DIM), lambda i, r: (0, 0, 0)),
                  pl.BlockSpec((1, HEAD_DIM), lambda i, r: (0, 0))],
        out_specs=pl.BlockSpec((1, tr, C), lambda i, r: (i, r, 0)),
        out_shape=jax.ShapeDtypeStruct((b, R, C), BF16),
        compiler_params=_params("parallel", "arbitrary"),
    )(x, x, wlo, whi, const, w2, gain)


def _prep_compress(cmp_pos, cmp_w1, cmp_w2, kn_cmp):
    r = CMP_LEN // CMP_STRIDE
    assert r == 2
    w1 = cmp_w1.reshape(2, r, CMP_STRIDE, HEAD_DIM, HEAD_DIM)
    kv_of_col = np.repeat(np.arange(2), KV_B)
    eye = jnp.eye(2 * KV_B, dtype=cmp_w1.dtype)

    def expand(half):
        w = w1[kv_of_col, half]
        return jnp.einsum('crde,cf->rcdfe', w, eye).reshape(CMP_ROW, 2 * KV_B * HEAD_DIM).astype(BF16)

    const = jnp.einsum('krd,krde->ke', cmp_pos.reshape(2, CMP_LEN, HEAD_DIM),
                       cmp_w1.reshape(2, CMP_LEN, HEAD_DIM, HEAD_DIM), precision=lax.Precision.HIGHEST)
    const = const[kv_of_col].reshape(1, -1).astype(F32)
    return dict(wlo=expand(0), whi=expand(1), const=const, w2=cmp_w2.astype(BF16), gain=kn_cmp.reshape(1, -1).astype(F32))


def _nsa_cmp_kernel(q_ref, kc_ref, vc_ref, bias_ref, ov_ref, oc_ref, sel_ref, *, geom, rep, n_slc, nsel):
    tq = geom.tq
    qt = pl.program_id(2)
    qpos0 = geom.past_len + qt * tq
    kc, vc, ov = kc_ref[0], vc_ref[0], ov_ref[...]
    ncp, nsp = ov.shape
    qpos, n_idx = _positions(qpos0, 0, tq, ncp)
    valid = (n_idx * CMP_STRIDE + (CMP_LEN - 1)) <= qpos
    imp = jnp.zeros((tq, nsp), F32)
    for r in range(rep):
        cols = slice(r * HEAD_DIM, (r + 1) * HEAD_DIM)
        s = jnp.where(valid, _qk(q_ref[0, :, cols], kc) * SCALE + bias_ref[r], NEG)
        e = jnp.where(valid, jnp.exp(s - jnp.max(s, axis=-1, keepdims=True)), 0.0)
        p = (e / jnp.maximum(jnp.sum(e, axis=-1, keepdims=True), 1e-30)).astype(BF16)
        oc_ref[0, :, cols] = jnp.dot(p, vc, preferred_element_type=F32)
        imp = imp + jnp.dot(p, ov, preferred_element_type=F32)
    spos, jj = _positions(qpos0, 0, tq, nsp)
    cur = spos // SLC_LEN
    forced = (jj == 0) | (jj == cur) | (jj == cur - 1)
    imp = jnp.where(jj > cur, NEG, jnp.where(forced, -NEG, imp))
    cnt = jnp.zeros_like(imp)
    for s_blk in range(n_slc):
        col = imp[:, s_blk:s_blk + 1]
        cnt = cnt + ((col > imp) | ((col == imp) & (jj > s_blk))).astype(F32)
    sel_ref[0, 0] = ((cnt < nsel) & (jj < n_slc)).astype(F32)


def nsa_compressed(geom, qarr, q_col, cmpkv, bias_cmp):
    b, T, tq = geom.b, geom.T, geom.tq
    rep = H_B // KV_B
    ncp = cmpkv.shape[1]
    n_slc = -(-geom.L // SLC_LEN)
    nsel = min(N_SLC, n_slc)
    nsp = _round_up(n_slc, LANE)
    n_cmp = geom.L // CMP_STRIDE - CMP_LEN // CMP_STRIDE + 1
    cstart = np.arange(ncp) * CMP_STRIDE
    cend = cstart + CMP_LEN - 1
    sstart = np.arange(nsp) * SLC_LEN
    ov = ((cstart[:, None] < sstart[None, :] + SLC_LEN) & (cend[:, None] >= sstart[None, :])
          & (np.arange(ncp)[:, None] < n_cmp) & (np.arange(nsp)[None, :] < n_slc)).astype(np.float32)
    return pl.pallas_call(
        functools.partial(_nsa_cmp_kernel, geom=geom, rep=rep, n_slc=n_slc, nsel=nsel),
        grid=(b, KV_B, geom.nqt),
        in_specs=[pl.BlockSpec((1, tq, rep * HEAD_DIM), lambda i, g, q: (i, q, q_col + g)),
                  pl.BlockSpec((1, ncp, HEAD_DIM), lambda i, g, q: (i, 0, g)),
                  pl.BlockSpec((1, ncp, HEAD_DIM), lambda i, g, q: (i, 0, KV_B + g)),
                  pl.BlockSpec((rep, tq, ncp), lambda i, g, q: (g, q, 0)),
                  pl.BlockSpec((ncp, nsp), lambda i, g, q: (0, 0))],
        out_specs=[pl.BlockSpec((1, tq, rep * HEAD_DIM), lambda i, g, q: (i, q, g)),
                   pl.BlockSpec((1, 1, tq, nsp), lambda i, g, q: (i, g, q, 0))],
        out_shape=[jax.ShapeDtypeStruct((b, T, H_B * HEAD_DIM), F32), jax.ShapeDtypeStruct((b, KV_B, T, nsp), F32)],
        compiler_params=_params("parallel", "parallel", "arbitrary"),
    )(qarr, cmpkv, cmpkv, bias_cmp, jnp.asarray(ov, BF16))


class _Window:
    def __init__(self, arr, k_col, v_col, tw, n_back, tile0, pos0, bias, bias_always):
        self.arr, self.k_col, self.v_col, self.tw, self.n_back = arr, k_col, v_col, tw, n_back
        self.tile0, self.pos0, self.bias, self.bias_always = tile0, pos0, bias, bias_always


def _nsa_attn_kernel(q_ref, ks_ref, vs_ref, kw_ref, vw_ref, sel_ref, bias_ref, wbias_ref, oc_ref, misc_ref, o_ref,
                     m_ref, l_ref, acc_ref, *, geom, rep, win):
    tq, tk = geom.tq, geom.tk
    qt, kt = pl.program_id(1), pl.program_id(2)
    qpos0 = geom.past_len + qt * tq
    last = geom.q_tile(qt)

    @pl.when(kt == 0)
    def _():
        _softmax_init(m_ref, l_ref, acc_ref)

    def branch(g, k_ref, v_ref, mask, slot0, b_ref, always):
        near = (last - kt) <= 1
        kv_cols = slice(g * HEAD_DIM, (g + 1) * HEAD_DIM)
        for h in range(g * rep, (g + 1) * rep):
            bias = b_ref[h, 0] if always else jnp.where(near, b_ref[h, 0], 0.0)
            s = _qk(q_ref[0, :, h * HEAD_DIM:(h + 1) * HEAD_DIM], k_ref[0, :, kv_cols]) * SCALE + bias
            _softmax_step(s, mask, v_ref[0, :, kv_cols], m_ref, l_ref, acc_ref, slot0 + h)

    @pl.when(kt <= last)
    def _():
        qpos, kpos = _positions(qpos0, kt * tk, tq, tk)
        nsp = sel_ref.shape[-1]
        blk = lax.broadcasted_iota(jnp.int32, (nsp, tk), 0)
        tok = lax.broadcasted_iota(jnp.int32, (nsp, tk), 1)
        expand = (blk == kt * (tk // SLC_LEN) + tok // SLC_LEN).astype(BF16)
        for g in range(KV_B):
            chosen = jnp.dot(sel_ref[0, g].astype(BF16), expand, preferred_element_type=F32) > 0.5
            branch(g, ks_ref, vs_ref, chosen & (kpos <= qpos), 0, bias_ref, False)

    @pl.when((kt <= last) & (kt >= last - win.n_back))
    def _():
        qpos, kpos = _positions(qpos0, win.pos0 + (kt - win.tile0) * win.tw, tq, win.tw)
        dist = qpos - kpos
        for g in range(KV_B):
            branch(g, kw_ref, vw_ref, (dist >= 0) & (dist < WINDOW), H_B, wbias_ref, win.bias_always)

    @pl.when(kt == geom.nkt - 1)
    def _():
        gates = jax.nn.sigmoid(misc_ref[0])
        for h in range(H_B):
            cols = slice(h * HEAD_DIM, (h + 1) * HEAD_DIM)
            base = H_IDX + h * N_BRANCH
            o = (gates[:, base:base + 1] * oc_ref[0, :, cols]
                 + gates[:, base + 1:base + 2] * _softmax_out(l_ref, acc_ref, h)
                 + gates[:, base + 2:base + 3] * _softmax_out(l_ref, acc_ref, H_B + h))
            o_ref[0, :, cols] = o.astype(o_ref.dtype)


def nsa_attention(geom, qarr, q_col, sarr, ks_col, vs_col, win, sel, bias, oc, miscarr, misc_col):
    b, T, tq, tk, nkt = geom.b, geom.T, geom.tq, geom.tk, geom.nkt
    nsp = sel.shape[-1]
    tw = win.tw
    n_wt = win.arr.shape[1] // tw
    kv_width = KV_B * HEAD_DIM

    def smap(col):
        return lambda i, q, k: (i, jnp.minimum(k, geom.q_tile(q)), col)

    def wtile(q, k):
        last = geom.q_tile(q)
        return jnp.clip(jnp.clip(k, last - win.n_back, last) - win.tile0, 0, n_wt - 1)

    def wmap(col):
        return lambda i, q, k: (i, wtile(q, k), col)

    if win.bias_always:
        assert win.bias.shape[1] == 1 and n_wt == 1
        wbias_spec = pl.BlockSpec((H_B, 1, tq, tw), lambda i, q, k: (0, 0, 0, 0))
    else:
        wbias_spec = _bias_spec(geom, H_B)

    return pl.pallas_call(
        functools.partial(_nsa_attn_kernel, geom=geom, rep=H_B // KV_B, win=win),
        grid=(b, geom.nqt, nkt),
        in_specs=[pl.BlockSpec((1, tq, H_B * HEAD_DIM), lambda i, q, k: (i, q, q_col)),
                  pl.BlockSpec((1, tk, kv_width), smap(ks_col)),
                  pl.BlockSpec((1, tk, kv_width), smap(vs_col)),
                  pl.BlockSpec((1, tw, kv_width), wmap(win.k_col)),
                  pl.BlockSpec((1, tw, kv_width), wmap(win.v_col)),
                  pl.BlockSpec((1, KV_B, tq, nsp), lambda i, q, k: (i, 0, q, 0)),
                  _bias_spec(geom, H_B),
                  wbias_spec,
                  pl.BlockSpec((1, tq, H_B * HEAD_DIM), lambda i, q, k: (i, q, 0)),
                  pl.BlockSpec((1, tq, LANE), lambda i, q, k: (i, q, misc_col))],
        out_specs=pl.BlockSpec((1, tq, H_B * HEAD_DIM), lambda i, q, k: (i, q, 0)),
        out_shape=jax.ShapeDtypeStruct((b, T, H_B * HEAD_DIM), BF16),
        scratch_shapes=[pltpu.VMEM((2 * H_B, tq, 1), F32), pltpu.VMEM((2 * H_B, tq, 1), F32),
                        pltpu.VMEM((2 * H_B, tq, HEAD_DIM), F32)],
        compiler_params=_params("parallel", "parallel", "arbitrary"),
    )(qarr, sarr, sarr, win.arr, win.arr, sel, bias, win.bias, oc, miscarr)


_SEG = {}
_off = 0
for _name, _w in (("qa", H_A * HEAD_DIM), ("qb", H_B * HEAD_DIM), ("qi", H_IDX * D_IDX), ("qc", H_C * HEAD_DIM),
                  ("kvc", 2 * H_C * HEAD_DIM), ("kva", 2 * KV_A * HEAD_DIM), ("cmp", 2 * KV_B * HEAD_DIM),
                  ("slc", 2 * KV_B * HEAD_DIM), ("win", 2 * KV_B * HEAD_DIM), ("ki", D_IDX), ("misc", LANE)):
    _SEG[_name] = (_off, _off + _w)
    _off += _w
IN_COLS_ALIGNED = _off


def _col(name, width=LANE):
    assert _SEG[name][0] % width == 0
    return _SEG[name][0] // width


def _prep_in_proj(w_in, qn_a, kn_a, qn_b, kn_b, qn_c, kn_c):
    sizes = (H_A * HEAD_DIM, KV_A * HEAD_DIM, KV_A * HEAD_DIM, H_IDX * D_IDX, H_IDX, D_IDX,
             H_B * HEAD_DIM, N_BRANCH * H_B) + (KV_B * HEAD_DIM,) * 6 + (H_C * HEAD_DIM,) * 3
    splits = np.cumsum(sizes)[:-1].tolist()
    (qa, ka, va, qi, wi, ki, q_b, g_b, kbc, vbc, kbs, vbs, kbw, vbw, qc, kc, vc) = jnp.split(w_in, splits, axis=-1)
    D = w_in.shape[0]
    misc = jnp.concatenate([wi, g_b, jnp.zeros((D, LANE - H_IDX - N_BRANCH * H_B), w_in.dtype)], axis=-1)
    n_pad = _round_up(IN_COLS_ALIGNED, 512) - IN_COLS_ALIGNED
    w = jnp.concatenate([qa, q_b, qi, qc, kc, vc, ka, va, kbc, vbc, kbs, vbs, kbw, vbw, ki, misc,
                         jnp.zeros((D, n_pad), w_in.dtype)], axis=-1).astype(BF16)
    one = jnp.ones((HEAD_DIM,), F32)

    def rep(g, n):
        return jnp.tile(g.astype(F32), n)

    gains = jnp.concatenate([
        rep(qn_a, H_A), rep(qn_b, H_B), rep(one, H_IDX), rep(qn_c, H_C), rep(kn_c, H_C), rep(one, H_C),
        rep(kn_a, KV_A), rep(one, KV_A), rep(one, 2 * KV_B), rep(kn_b[1], KV_B), rep(one, KV_B),
        rep(kn_b[2], KV_B), rep(one, KV_B), one, one, jnp.ones((n_pad,), F32)]).reshape(1, -1)
    flags = np.concatenate([
        np.ones(H_A), np.ones(H_B), np.zeros(H_IDX), np.ones(H_C), np.ones(H_C), np.zeros(H_C),
        np.ones(KV_A), np.zeros(KV_A), np.zeros(2 * KV_B), np.ones(KV_B), np.zeros(KV_B),
        np.ones(KV_B), np.zeros(KV_B), np.zeros(1), np.zeros(1), np.zeros(n_pad // LANE)]).astype(np.int32)
    return w, gains, jnp.asarray(flags)


def _prep_layer(i, ffn1_norm, ffn1_wg, ffn1_wu, ffn1_wd, mix_norm, w_in, qn_a, kn_a, qn_b, kn_b, cmp_pos, cmp_w1,
                cmp_w2, qn_c, kn_c, wb_a, wb_b, wb_c, w_gate, b_gate, w_out, ffn2_norm, ffn2_wg, ffn2_wu, ffn2_wd,
                ple_norm, ple_wg, ple_wp):
    F = ffn1_wg.shape[2]
    Fp = _round_up(F, 512) if F > 512 else F

    def up(w):
        return jnp.pad(w[i], ((0, 0), (0, Fp - F))).astype(BF16)

    def down(w):
        return jnp.pad(w[i], ((0, Fp - F), (0, 0))).astype(BF16)

    w_in_p, gains, flags = _prep_in_proj(w_in[i], qn_a[i], kn_a[i], qn_b[i], kn_b[i], qn_c[i], kn_c[i])
    return dict(
        f1n=ffn1_norm[i], f1g=up(ffn1_wg), f1u=up(ffn1_wu), f1d=down(ffn1_wd), mn=mix_norm[i],
        w_in=w_in_p, gains=gains, flags=flags,
        cmp=_prep_compress(cmp_pos[i], cmp_w1[i], cmp_w2[i], kn_b[i, 0]),
        wb_a=wb_a[i].astype(BF16), wb_b=wb_b[i].astype(BF16), wb_c=wb_c[i].astype(BF16),
        w_gate=w_gate[i].astype(BF16), b_gate=b_gate[i].reshape(1, -1), w_out=w_out[i].astype(BF16),
        f2n=ffn2_norm[i], f2g=up(ffn2_wg), f2u=up(ffn2_wu), f2d=down(ffn2_wd),
        pn=ple_norm[i], pwg=ple_wg[i].astype(BF16), pwp=ple_wp[i].astype(BF16))


def _layer(x, p, geom, layer, caches, W, bias):
    b, T = geom.b, geom.T
    M = b * T
    h = ffn_down(ffn_up(rmsnorm_bf16(x, W["f1n"]), W["f1g"], W["f1u"]), W["f1d"], x)
    u = rmsnorm_bf16(h, W["mn"])
    proj32, proj16 = in_proj(u, W["w_in"], W["gains"], W["flags"])
    Np = proj32.shape[1]
    p32 = proj32.reshape(b, T, Np)
    p16 = proj16.reshape(b, T, Np)

    def seg32(name):
        lo, hi = _SEG[name]
        return p32[:, :, lo:hi]

    def seg16(name):
        lo, hi = _SEG[name]
        return p16[:, :, lo:hi]

    win_new = seg32("win")
    if caches is None:
        assert T >= WINDOW
        a_kv, a_ki, slc, c_kv = (p16,) * 4
        a_k, ki_c = _col("kva", KV_A * HEAD_DIM), _col("ki")
        s_k = _col("slc", KV_B * HEAD_DIM)
        c_k = _col("kvc", H_C * HEAD_DIM)
        w_k = _col("win", KV_B * HEAD_DIM)
        win = _Window(p16, w_k, w_k + 1, geom.tk, WINDOW // geom.tk, 0, 0, bias["b"], False)
        cmp_rows = seg16("cmp")
        win_state = win_new[:, T - WINDOW:]
    else:
        cache_a_kv, cache_a_kidx, cache_b_cmp, cache_b_slc, win_past, cache_c_kv, page_table = caches
        a_kv = assemble_cache(cache_a_kv, layer, page_table, seg16("kva"), geom.tk)
        a_ki = assemble_cache(cache_a_kidx, layer, page_table, seg16("ki"), geom.tk)
        cmp_rows = assemble_cache(cache_b_cmp, layer, page_table, seg16("cmp"), geom.tk)
        slc = assemble_cache(cache_b_slc, layer, page_table, seg16("slc"), geom.tk)
        c_kv = assemble_cache(cache_c_kv, layer, page_table, seg16("kvc"), geom.tk)
        a_k, ki_c, s_k, c_k = 0, 0, 0, 0
        n_buf = win_past.shape[2]
        win_past = win_past[layer].reshape(b, n_buf, -1)
        warr = jnp.concatenate([win_past.astype(BF16), seg16("win"),
                                jnp.zeros((b, KEY_TILE - T, win_past.shape[-1]), BF16)], axis=1)
        win = _Window(warr, 0, 1, warr.shape[1], 0, geom.q_tile(0), geom.past_len - n_buf, bias["b_win"], True)
        win_state = jnp.concatenate([win_past, win_new], axis=1)[:, -WINDOW:]

    scores, thr = dsa_indexer(geom, p16, _col("qi", H_IDX * D_IDX), p32, _col("misc"), a_ki, ki_c)
    o_a = dsa_attention(geom, p16, _col("qa", H_A * HEAD_DIM), a_kv, a_k, a_k + 1, scores, thr, bias["a"])
    cmpkv = nsa_compress(cmp_rows.reshape(b, -1, CMP_ROW), **W["cmp"])
    qpos = geom.past_len + jnp.arange(T)
    cend = jnp.arange(cmpkv.shape[1]) * CMP_STRIDE + CMP_LEN - 1
    bias_cmp = _bias_lookup(bias["tab_b"], qpos[:, None] - cend[None, :])
    oc, sel_b = nsa_compressed(geom, p16, _col("qb", 4 * HEAD_DIM), cmpkv, bias_cmp)
    o_b = nsa_attention(geom, p16, _col("qb", H_B * HEAD_DIM), slc, s_k, s_k + 1, win, sel_b, bias["b"], oc, p32,
                        _col("misc"))
    sel_c = moba_select(geom, p16, _col("qc"), c_kv, c_k * H_C)
    o_c = moba_attention(geom, p16, _col("qc", H_C * HEAD_DIM), c_kv, c_k, c_k + 1, sel_c, bias["c"])

    merged = gate_merge(u, o_a.reshape(M, -1), o_b.reshape(M, -1), o_c.reshape(M, -1),
                        W["w_gate"], W["b_gate"], W["wb_a"], W["wb_b"], W["wb_c"])
    h = out_proj(merged, W["w_out"], h)
    h = ffn_down(ffn_up(rmsnorm_bf16(h, W["f2n"]), W["f2g"], W["f2u"]), W["f2d"], h)
    h = ple(rmsnorm_bf16(h, W["pn"]), W["pwg"], p.astype(BF16), W["pwp"], h)
    state = (seg32("kva").reshape(b, T, 2, KV_A, HEAD_DIM), seg32("ki"), seg32("cmp").reshape(b, T, 2, KV_B, HEAD_DIM),
             seg32("slc").reshape(b, T, 2, KV_B, HEAD_DIM), win_state.reshape(b, WINDOW, 2, KV_B, HEAD_DIM),
             seg32("kvc").reshape(b, T, 2, H_C, HEAD_DIM))
    return h, state


def kernel(x_prompt, x_sample, cache_a_kv, cache_a_kidx, cache_b_cmp_kv, cache_b_slc_kv, state_b_win_kv, cache_c_kv, page_table, p_prompt, p_sample, rel_bias, ffn1_norm, ffn1_wg, ffn1_wu, ffn1_wd, mix_norm, w_in, qn_a, kn_a, qn_b, kn_b, cmp_pos, cmp_w1, cmp_w2, qn_c, kn_c, wb_a, wb_b, wb_c, w_gate, b_gate, w_out, ffn2_norm, ffn2_wg, ffn2_wu, ffn2_wd, ple_norm, ple_wg, ple_wp):
    depth = ffn1_norm.shape[0]
    bp, Tp, D = x_prompt.shape
    bs, Ts, _ = x_sample.shape
    geom_p = _Geom(bp, Tp, 0)
    geom_s = _Geom(bs, Ts, page_table.shape[1] * PAGE_SIZE)
    n_buf = state_b_win_kv.shape[2]
    assert n_buf >= WINDOW and Ts <= KEY_TILE
    tab_a, tab_b, tab_c = rel_bias[:, :H_A], rel_bias[:, H_A:H_A + H_B], rel_bias[:, H_A + H_B:]

    def tile_bias(geom):
        near = (0, geom.tk)
        return dict(a=_toeplitz_bias(tab_a, geom.tq, geom.tk, near), b=_toeplitz_bias(tab_b, geom.tq, geom.tk, near),
                    c=_toeplitz_bias(tab_c, geom.tq, geom.tk, near), tab_b=tab_b)

    bias_p = tile_bias(geom_p)
    bias_s = tile_bias(geom_s)
    bias_s["b_win"] = _toeplitz_bias(tab_b, geom_s.tq, n_buf + KEY_TILE, (n_buf,))
    caches = (cache_a_kv, cache_a_kidx, cache_b_cmp_kv, cache_b_slc_kv, state_b_win_kv, cache_c_kv, page_table)

    y_p = x_prompt.reshape(bp * Tp, D)
    y_s = x_sample.reshape(bs * Ts, D)
    sp_list, ss_list = [], []
    for i in range(depth):
        W = _prep_layer(i, ffn1_norm, ffn1_wg, ffn1_wu, ffn1_wd, mix_norm, w_in, qn_a, kn_a, qn_b, kn_b, cmp_pos,
                        cmp_w1, cmp_w2, qn_c, kn_c, wb_a, wb_b, wb_c, w_gate, b_gate, w_out, ffn2_norm, ffn2_wg,
                        ffn2_wu, ffn2_wd, ple_norm, ple_wg, ple_wp)
        y_p, sp = _layer(y_p, p_prompt[i].reshape(bp * Tp, -1), geom_p, i, None, W, bias_p)
        y_s, ss = _layer(y_s, p_sample[i].reshape(bs * Ts, -1), geom_s, i, caches, W, bias_s)
        sp_list.append(sp)
        ss_list.append(ss)

    outs = [y_p.reshape(bp, Tp, D), y_s.reshape(bs, Ts, D)]
    for j in range(6):
        outs.append(jnp.stack([s[j] for s in sp_list]))
        outs.append(jnp.stack([s[j] for s in ss_list]))
    return tuple(outs)
```

```python
import functools
import math

import numpy as np
import jax
import jax.numpy as jnp
from jax import lax
from jax.experimental import pallas as pl
from jax.experimental.pallas import tpu as pltpu

PAGE_SIZE = 128
HEAD_DIM = 128
H_A = 8
KV_A = 2
H_IDX = 16
D_IDX = 128
TOPK_A = 256
H_B = 8
KV_B = 2
CMP_LEN = 32
CMP_STRIDE = 16
SLC_LEN = 64
N_SLC = 16
WINDOW = 512
H_C = 8
MOBA_BLOCK = 256
MOBA_TOPK = 3
N_BUCKETS = 32
MAX_DISTANCE = 128
N_BRANCH = 3
EPS = 1e-6
NEG = -1e30

LANE = 128
BF16_SUBLANE = 16
V7X_VMEM_LIMIT = 56 * 1024 * 1024

F32 = jnp.float32
BF16 = jnp.bfloat16


def _tile(n, target, align):
    best = None
    for t in range(align, min(n, target) + 1, align):
        if n % t == 0:
            best = t
    return best if best is not None else n


def _round_up(n, m):
    return -(-n // m) * m


def _params(*sem):
    return pltpu.CompilerParams(dimension_semantics=sem, vmem_limit_bytes=V7X_VMEM_LIMIT)


def _rmsnorm_kernel(x_ref, g_ref, o_ref):
    x = x_ref[...]
    y = x * lax.rsqrt(jnp.mean(x * x, axis=-1, keepdims=True) + EPS)
    o_ref[...] = (y * g_ref[...]).astype(o_ref.dtype)


def rmsnorm_bf16(x, g):
    M, D = x.shape
    tm = _tile(M, 256, BF16_SUBLANE)
    return pl.pallas_call(
        _rmsnorm_kernel,
        grid=(M // tm,),
        in_specs=[pl.BlockSpec((tm, D), lambda i: (i, 0)), pl.BlockSpec((1, D), lambda i: (0, 0))],
        out_specs=pl.BlockSpec((tm, D), lambda i: (i, 0)),
        out_shape=jax.ShapeDtypeStruct((M, D), BF16),
        compiler_params=_params("parallel"),
    )(x, g.reshape(1, D))


def _ffn_up_kernel(a_ref, wg_ref, wu_ref, o_ref):
    a = a_ref[...]
    g = jnp.dot(a, wg_ref[...], preferred_element_type=F32)
    u = jnp.dot(a, wu_ref[...], preferred_element_type=F32)
    o_ref[...] = (g * jax.nn.sigmoid(g) * u).astype(o_ref.dtype)


def ffn_up(a, wg, wu):
    M, D = a.shape
    F = wg.shape[1]
    tm = _tile(M, 1024, BF16_SUBLANE)
    tn = _tile(F, 512, LANE)
    return pl.pallas_call(
        _ffn_up_kernel,
        grid=(M // tm, F // tn),
        in_specs=[pl.BlockSpec((tm, D), lambda i, j: (i, 0)),
                  pl.BlockSpec((D, tn), lambda i, j: (0, j)),
                  pl.BlockSpec((D, tn), lambda i, j: (0, j))],
        out_specs=pl.BlockSpec((tm, tn), lambda i, j: (i, j)),
        out_shape=jax.ShapeDtypeStruct((M, F), BF16),
        compiler_params=_params("parallel", "arbitrary"),
    )(a, wg, wu)


def _ffn_down_kernel(a_ref, w_ref, x_ref, o_ref, acc_ref):
    k = pl.program_id(2)

    @pl.when(k == 0)
    def _():
        acc_ref[...] = jnp.zeros_like(acc_ref)

    acc_ref[...] += jnp.dot(a_ref[...], w_ref[...], preferred_element_type=F32)

    @pl.when(k == pl.num_programs(2) - 1)
    def _():
        o_ref[...] = x_ref[...] + 0.5 * acc_ref[...]


def ffn_down(a, w, x):
    M, F = a.shape
    D = w.shape[1]
    tm = _tile(M, 1024, BF16_SUBLANE)
    tn = _tile(D, 512, LANE)
    tk = _tile(F, 5504, LANE)
    return pl.pallas_call(
        _ffn_down_kernel,
        grid=(M // tm, D // tn, F // tk),
        in_specs=[pl.BlockSpec((tm, tk), lambda i, j, k: (i, k)),
                  pl.BlockSpec((tk, tn), lambda i, j, k: (k, j)),
                  pl.BlockSpec((tm, tn), lambda i, j, k: (i, j))],
        out_specs=pl.BlockSpec((tm, tn), lambda i, j, k: (i, j)),
        out_shape=jax.ShapeDtypeStruct((M, D), F32),
        scratch_shapes=[pltpu.VMEM((tm, tn), F32)],
        compiler_params=_params("parallel", "arbitrary", "arbitrary"),
    )(a, w, x)


def _w_in_kernel(flags_ref, a_ref, w_ref, g_ref, o_ref, o16_ref, *, heads_per_tile):
    j = pl.program_id(1)
    acc = jnp.dot(a_ref[...], w_ref[...], preferred_element_type=F32)
    for c in range(heads_per_tile):
        cols = slice(c * HEAD_DIM, (c + 1) * HEAD_DIM)
        y = acc[:, cols]
        yn = y * lax.rsqrt(jnp.mean(y * y, axis=-1, keepdims=True) + EPS) * g_ref[:, cols]
        y = jnp.where(flags_ref[j * heads_per_tile + c] > 0, yn, y)
        o_ref[:, cols] = y
        o16_ref[:, cols] = y.astype(o16_ref.dtype)


def in_proj(a, w, gains, flags):
    M, D = a.shape
    N = w.shape[1]
    tm = _tile(M, 1024, BF16_SUBLANE)
    tn = _tile(N, 512, LANE)
    grid_spec = pltpu.PrefetchScalarGridSpec(
        num_scalar_prefetch=1,
        grid=(M // tm, N // tn),
        in_specs=[pl.BlockSpec((tm, D), lambda i, j, f: (i, 0)),
                  pl.BlockSpec((D, tn), lambda i, j, f: (0, j)),
                  pl.BlockSpec((1, tn), lambda i, j, f: (0, j))],
        out_specs=[pl.BlockSpec((tm, tn), lambda i, j, f: (i, j)), pl.BlockSpec((tm, tn), lambda i, j, f: (i, j))],
    )
    return pl.pallas_call(
        functools.partial(_w_in_kernel, heads_per_tile=tn // HEAD_DIM),
        grid_spec=grid_spec,
        out_shape=[jax.ShapeDtypeStruct((M, N), F32), jax.ShapeDtypeStruct((M, N), BF16)],
        compiler_params=_params("parallel", "arbitrary"),
    )(flags, a, w, gains)


def _gate_merge_kernel(u_ref, oa_ref, ob_ref, oc_ref, wg0_ref, wg1_ref, wg2_ref, bg0_ref, bg1_ref, bg2_ref,
                       wba_ref, wbb_ref, wbc_ref, o_ref):
    u = u_ref[...]
    out = None
    for o_r, wg_r, bg_r, wb_r in ((oa_ref, wg0_ref, bg0_ref, wba_ref),
                                  (ob_ref, wg1_ref, bg1_ref, wbb_ref),
                                  (oc_ref, wg2_ref, bg2_ref, wbc_ref)):
        gate = jax.nn.sigmoid(jnp.dot(u, wg_r[...], preferred_element_type=F32) + bg_r[...])
        branch = jnp.dot(o_r[...], wb_r[...], preferred_element_type=F32)
        out = gate * branch if out is None else out + gate * branch
    o_ref[...] = out.astype(o_ref.dtype)


def gate_merge(u, o_a, o_b, o_c, w_gate, b_gate, wb_a, wb_b, wb_c):
    M, D = u.shape
    Ho = o_a.shape[1]
    tm = _tile(M, 1024, BF16_SUBLANE)
    tn = _tile(D, 256, LANE)
    nj = D // tn
    a_spec = pl.BlockSpec((tm, D), lambda i, j: (i, 0))
    o_spec = pl.BlockSpec((tm, Ho), lambda i, j: (i, 0))
    wb_spec = pl.BlockSpec((Ho, tn), lambda i, j: (0, j))

    def branch_spec(rows, br):
        return pl.BlockSpec((rows, tn), lambda i, j: (0, br * nj + j))

    return pl.pallas_call(
        _gate_merge_kernel,
        grid=(M // tm, nj),
        in_specs=[a_spec, o_spec, o_spec, o_spec,
                  branch_spec(D, 0), branch_spec(D, 1), branch_spec(D, 2),
                  branch_spec(1, 0), branch_spec(1, 1), branch_spec(1, 2),
                  wb_spec, wb_spec, wb_spec],
        out_specs=pl.BlockSpec((tm, tn), lambda i, j: (i, j)),
        out_shape=jax.ShapeDtypeStruct((M, D), BF16),
        compiler_params=_params("parallel", "arbitrary"),
    )(u, o_a, o_b, o_c, w_gate, w_gate, w_gate, b_gate, b_gate, b_gate, wb_a, wb_b, wb_c)


def _out_proj_kernel(a_ref, w_ref, x_ref, o_ref):
    o_ref[...] = x_ref[...] + jnp.dot(a_ref[...], w_ref[...], preferred_element_type=F32)


def out_proj(a, w, x):
    M, K = a.shape
    N = w.shape[1]
    tm = _tile(M, 1024, BF16_SUBLANE)
    tn = _tile(N, 512, LANE)
    return pl.pallas_call(
        _out_proj_kernel,
        grid=(M // tm, N // tn),
        in_specs=[pl.BlockSpec((tm, K), lambda i, j: (i, 0)),
                  pl.BlockSpec((K, tn), lambda i, j: (0, j)),
                  pl.BlockSpec((tm, tn), lambda i, j: (i, j))],
        out_specs=pl.BlockSpec((tm, tn), lambda i, j: (i, j)),
        out_shape=jax.ShapeDtypeStruct((M, N), F32),
        compiler_params=_params("parallel", "arbitrary"),
    )(a, w, x)


def _ple_kernel(a_ref, wg_ref, p_ref, wp_ref, x_ref, o_ref):
    gate = jax.nn.sigmoid(jnp.dot(a_ref[...], wg_ref[...], preferred_element_type=F32))
    emb = jnp.dot(p_ref[...], wp_ref[...], preferred_element_type=F32)
    o_ref[...] = x_ref[...] + gate * emb


def ple(a, wg, p, wp, x):
    M, K = a.shape
    N = wg.shape[1]
    P = p.shape[1]
    tm = _tile(M, 1024, BF16_SUBLANE)
    tn = _tile(N, 512, LANE)
    return pl.pallas_call(
        _ple_kernel,
        grid=(M // tm, N // tn),
        in_specs=[pl.BlockSpec((tm, K), lambda i, j: (i, 0)),
                  pl.BlockSpec((K, tn), lambda i, j: (0, j)),
                  pl.BlockSpec((tm, P), lambda i, j: (i, 0)),
                  pl.BlockSpec((P, tn), lambda i, j: (0, j)),
                  pl.BlockSpec((tm, tn), lambda i, j: (i, j))],
        out_specs=pl.BlockSpec((tm, tn), lambda i, j: (i, j)),
        out_shape=jax.ShapeDtypeStruct((M, N), F32),
        compiler_params=_params("parallel", "arbitrary"),
    )(a, wg, p, wp, x)


SCALE = HEAD_DIM ** -0.5
KEY_TILE = 256
QUERY_TILE = 256
SHORT_QUERY_KEY_TILE = 2048
ASSEMBLE_PAGES = 8
INT32_MIN = -2 ** 31

_NT = (((1,), (1,)), ((), ()))


def _qk(q, k):
    return lax.dot_general(q, k, _NT, preferred_element_type=F32)


def _softmax_init(m_ref, l_ref, acc_ref):
    m_ref[...] = jnp.full(m_ref.shape, NEG, F32)
    l_ref[...] = jnp.zeros(l_ref.shape, F32)
    acc_ref[...] = jnp.zeros(acc_ref.shape, F32)


def _softmax_step(s, mask, v, m_ref, l_ref, acc_ref, r):
    s = jnp.where(mask, s, NEG)
    m_prev = m_ref[r]
    m_new = jnp.maximum(m_prev, jnp.max(s, axis=-1, keepdims=True))
    p = jnp.where(mask, jnp.exp(s - m_new), 0.0)
    alpha = jnp.exp(m_prev - m_new)
    l_ref[r] = alpha * l_ref[r] + jnp.sum(p, axis=-1, keepdims=True)
    acc_ref[r] = alpha * acc_ref[r] + jnp.dot(p.astype(BF16), v, preferred_element_type=F32)
    m_ref[r] = m_new


def _near_or_far(near, step):
    @pl.when(near)
    def _():
        step(True)

    @pl.when(jnp.logical_not(near))
    def _():
        step(False)


def _softmax_out(l_ref, acc_ref, r):
    return acc_ref[r] / jnp.maximum(l_ref[r], 1e-30)


def _positions(qpos0, kpos0, tq, tk):
    qpos = qpos0 + lax.broadcasted_iota(jnp.int32, (tq, tk), 0)
    kpos = kpos0 + lax.broadcasted_iota(jnp.int32, (tq, tk), 1)
    return qpos, kpos


def _lane_column(x, idx):
    lane = lax.broadcasted_iota(jnp.int32, x.shape, 1)
    return jnp.sum(jnp.where(lane == idx, x, 0.0), axis=-1, keepdims=True)


def _t5_bucket(n):
    n = jnp.maximum(n, 0)
    exact = N_BUCKETS // 2
    nf = jnp.maximum(n, 1).astype(jnp.float32)
    big = exact + (jnp.log(nf / exact) * ((N_BUCKETS - exact) / math.log(MAX_DISTANCE / exact))).astype(jnp.int32)
    return jnp.where(n < exact, n, jnp.minimum(big, N_BUCKETS - 1))


def _bias_lookup(tab, dist):
    onehot = jax.nn.one_hot(_t5_bucket(dist), N_BUCKETS, dtype=F32)
    return jnp.einsum('...n,nh->h...', onehot, tab.astype(F32), precision=lax.Precision.HIGHEST)


def _toeplitz_bias(tab, tq, tk, offsets):
    i = jnp.arange(tq)[:, None]
    j = jnp.arange(tk)[None, :]
    d = jnp.stack([off + i - j for off in offsets])
    return _bias_lookup(tab, d) - tab[N_BUCKETS - 1].astype(F32)[:, None, None, None]


class _Geom:
    def __init__(self, b, T, past_len):
        self.b, self.T, self.past_len = b, T, past_len
        self.tq = min(QUERY_TILE, T)
        self.tk = KEY_TILE if T >= QUERY_TILE else _tile(past_len, SHORT_QUERY_KEY_TILE, KEY_TILE)
        assert T % self.tq == 0 and self.tk % self.tq == 0 and past_len % self.tk == 0
        assert self.tq == self.tk or T == self.tq, "a query tile must not straddle key tiles"
        assert MAX_DISTANCE <= self.tk, "bias must be constant two key tiles behind the query tile"
        self.nqt = T // self.tq
        self.L = past_len + T
        self.nkt = (self.L - 1) // self.tk + 1

    def q_tile(self, qt):
        return (self.past_len + qt * self.tq) // self.tk


def _bias_spec(geom, heads):
    return pl.BlockSpec((heads, 1, geom.tq, geom.tk),
                        lambda i, q, k: (0, jnp.clip(geom.q_tile(q) - k, 0, 1), 0, 0))


def _assemble_kernel(pt_ref, *refs, n_in, n_page_steps, T, R):
    ins, new_ref, o_ref = refs[:n_in], refs[n_in], refs[n_in + 1]
    s = pl.program_id(1)

    @pl.when(s < n_page_steps)
    def _():
        for p in range(n_in):
            for c in range(R):
                rows = ins[p][0, 0, pl.ds(c, PAGE_SIZE, stride=R), :] if R > 1 else ins[p][0, 0]
                o_ref[0, p * PAGE_SIZE:(p + 1) * PAGE_SIZE, c * HEAD_DIM:(c + 1) * HEAD_DIM] = rows.astype(o_ref.dtype)

    @pl.when(s >= n_page_steps)
    def _():
        o_ref[0] = jnp.zeros(o_ref.shape[1:], o_ref.dtype)

    @pl.when(s == n_page_steps)
    def _():
        o_ref[0, :T, :] = new_ref[0]


def assemble_cache(cache, layer, page_table, new, pad_rows):
    b, n_pages = page_table.shape
    T, C = new.shape[1], new.shape[2]
    R = C // HEAD_DIM
    cache = cache.reshape(cache.shape[0], cache.shape[1], PAGE_SIZE * R, HEAD_DIM)
    P = _tile(n_pages, ASSEMBLE_PAGES, 1)
    steps = n_pages // P
    assert pad_rows % (P * PAGE_SIZE) == 0 and T <= P * PAGE_SIZE
    pad_steps = pad_rows // (P * PAGE_SIZE)

    def page_spec(p):
        return pl.BlockSpec((1, 1, PAGE_SIZE * R, HEAD_DIM),
                            lambda i, s, pt: (layer, pt[i, jnp.minimum(s * P + p, n_pages - 1)], 0, 0))

    grid_spec = pltpu.PrefetchScalarGridSpec(
        num_scalar_prefetch=1,
        grid=(b, steps + pad_steps),
        in_specs=[page_spec(p) for p in range(P)] + [pl.BlockSpec((1, T, C), lambda i, s, pt: (i, 0, 0))],
        out_specs=pl.BlockSpec((1, P * PAGE_SIZE, C), lambda i, s, pt: (i, s, 0)),
    )
    return pl.pallas_call(
        functools.partial(_assemble_kernel, n_in=P, n_page_steps=steps, T=T, R=R),
        grid_spec=grid_spec,
        out_shape=jax.ShapeDtypeStruct((b, (steps + pad_steps) * P * PAGE_SIZE, C), BF16),
        compiler_params=_params("parallel", "arbitrary"),
    )(page_table, *([cache] * P), new)


def _indexer_kernel(qi_ref, wi_ref, ki_ref, sc_ref, thr_ref, key_ref, *, geom, topk):
    tq, tk = geom.tq, geom.tk
    qt, kt = pl.program_id(1), pl.program_id(2)
    qpos0 = geom.past_len + qt * tq
    last = geom.q_tile(qt)

    @pl.when(kt == 0)
    def _():
        sc_ref[0] = jnp.full(sc_ref.shape[1:], NEG, F32)
        key_ref[...] = jnp.full(key_ref.shape, INT32_MIN, jnp.int32)

    @pl.when(kt <= last)
    def _():
        ki = ki_ref[0]
        wi = wi_ref[0]
        acc = jnp.zeros((tq, tk), F32)
        for h in range(H_IDX):
            s = _qk(qi_ref[0, :, h * D_IDX:(h + 1) * D_IDX], ki)
            acc = acc + jnp.maximum(s, 0.0) * wi[:, h:h + 1]
        qpos, kpos = _positions(qpos0, kt * tk, tq, tk)
        score = jnp.where(kpos <= qpos, acc * (D_IDX ** -0.5 * H_IDX ** -0.5), NEG)
        sc_ref[0, kt] = score
        i = lax.bitcast_convert_type(score, jnp.int32)
        key_ref[kt] = i ^ ((i >> 31) & 0x7FFFFFFF)

    @pl.when(kt == geom.nkt - 1)
    def _():
        def body(bit, lo):
            cand = lo + jnp.left_shift(jnp.int32(1), 31 - bit)
            ge = (key_ref[...] >= cand[None]).astype(F32)
            cnt = jnp.sum(jnp.sum(ge, axis=0), axis=-1, keepdims=True)
            return jnp.where(cnt >= topk, cand, lo)

        lo = lax.fori_loop(0, 32, body, jnp.full((tq, 1), INT32_MIN, jnp.int32))
        thr = lax.bitcast_convert_type(lo ^ ((lo >> 31) & 0x7FFFFFFF), F32)
        thr_ref[0] = jnp.broadcast_to(thr, (tq, LANE))


def dsa_indexer(geom, qarr, qi_col, miscarr, misc_col, kiarr, ki_col):
    b, T, tq, tk, nkt = geom.b, geom.T, geom.tq, geom.tk, geom.nkt
    topk = min(TOPK_A, geom.L // 4)
    assert topk <= tk, "the first causal key tile alone must hold topk entries"
    return pl.pallas_call(
        functools.partial(_indexer_kernel, geom=geom, topk=topk),
        grid=(b, geom.nqt, nkt),
        in_specs=[pl.BlockSpec((1, tq, H_IDX * D_IDX), lambda i, q, k: (i, q, qi_col)),
                  pl.BlockSpec((1, tq, LANE), lambda i, q, k: (i, q, misc_col)),
                  pl.BlockSpec((1, tk, D_IDX), lambda i, q, k: (i, jnp.minimum(k, geom.q_tile(q)), ki_col))],
        out_specs=[pl.BlockSpec((1, nkt, tq, tk), lambda i, q, k: (i, 0, q, 0)),
                   pl.BlockSpec((1, tq, LANE), lambda i, q, k: (i, q, 0))],
        out_shape=[jax.ShapeDtypeStruct((b, nkt, T, tk), F32), jax.ShapeDtypeStruct((b, T, LANE), F32)],
        scratch_shapes=[pltpu.VMEM((nkt, tq, tk), jnp.int32)],
        compiler_params=_params("parallel", "parallel", "arbitrary"),
    )(qarr, miscarr, kiarr)


def _dsa_attn_kernel(q_ref, k_ref, v_ref, sc_ref, thr_ref, bias_ref, o_ref, m_ref, l_ref, acc_ref, *, geom, rep):
    tq, tk = geom.tq, geom.tk
    qt, kt = pl.program_id(1), pl.program_id(2)
    qpos0 = geom.past_len + qt * tq
    last = geom.q_tile(qt)

    @pl.when(kt == 0)
    def _():
        _softmax_init(m_ref, l_ref, acc_ref)

    @pl.when(kt <= last)
    def _():
        qpos, kpos = _positions(qpos0, kt * tk, tq, tk)
        mask = (sc_ref[0, 0] >= thr_ref[0][:, :1]) & (kpos <= qpos)

        def step(add_bias):
            for h in range(H_A):
                kv_cols = slice((h // rep) * HEAD_DIM, (h // rep + 1) * HEAD_DIM)
                s = _qk(q_ref[0, :, h * HEAD_DIM:(h + 1) * HEAD_DIM], k_ref[0, :, kv_cols]) * SCALE
                if add_bias:
                    s = s + bias_ref[h, 0]
                _softmax_step(s, mask, v_ref[0, :, kv_cols], m_ref, l_ref, acc_ref, h)

        _near_or_far((last - kt) <= 1, step)

    @pl.when(kt == geom.nkt - 1)
    def _():
        for h in range(H_A):
            o_ref[0, :, h * HEAD_DIM:(h + 1) * HEAD_DIM] = _softmax_out(l_ref, acc_ref, h).astype(o_ref.dtype)


def dsa_attention(geom, qarr, q_col, karr, k_col, v_col, scores, thr, bias):
    b, T, tq, tk, nkt = geom.b, geom.T, geom.tq, geom.tk, geom.nkt

    def kmap(col):
        return lambda i, q, k: (i, jnp.minimum(k, geom.q_tile(q)), col)

    return pl.pallas_call(
        functools.partial(_dsa_attn_kernel, geom=geom, rep=H_A // KV_A),
        grid=(b, geom.nqt, nkt),
        in_specs=[pl.BlockSpec((1, tq, H_A * HEAD_DIM), lambda i, q, k: (i, q, q_col)),
                  pl.BlockSpec((1, tk, KV_A * HEAD_DIM), kmap(k_col)),
                  pl.BlockSpec((1, tk, KV_A * HEAD_DIM), kmap(v_col)),
                  pl.BlockSpec((1, 1, tq, tk), lambda i, q, k: (i, jnp.minimum(k, geom.q_tile(q)), q, 0)),
                  pl.BlockSpec((1, tq, LANE), lambda i, q, k: (i, q, 0)),
                  _bias_spec(geom, H_A)],
        out_specs=pl.BlockSpec((1, tq, H_A * HEAD_DIM), lambda i, q, k: (i, q, 0)),
        out_shape=jax.ShapeDtypeStruct((b, T, H_A * HEAD_DIM), BF16),
        scratch_shapes=[pltpu.VMEM((H_A, tq, 1), F32), pltpu.VMEM((H_A, tq, 1), F32),
                        pltpu.VMEM((H_A, tq, HEAD_DIM), F32)],
        compiler_params=_params("parallel", "parallel", "arbitrary"),
    )(qarr, karr, karr, scores, thr, bias)


def _moba_select_kernel(q_ref, k_ref, avg_ref, sel_ref, *, geom, nblk, ksel):
    qt = pl.program_id(2)
    cur = (geom.past_len + qt * geom.tq) // MOBA_BLOCK
    kmean = jnp.dot(avg_ref[...], k_ref[0], preferred_element_type=F32)
    gs = _qk(q_ref[0], kmean.astype(BF16))
    lane = lax.broadcasted_iota(jnp.int32, gs.shape, 1)
    past = lane < cur
    g = jnp.where(past, gs, NEG)
    cnt = jnp.zeros_like(g)
    for m in range(nblk):
        col = g[:, m:m + 1]
        cnt = cnt + ((col > g) | ((col == g) & (lane > m))).astype(F32)
    sel_ref[0, 0] = (past & (cnt < ksel)).astype(F32)


def moba_select(geom, qarr, q_col, karr, k_col):
    b, T, tq = geom.b, geom.T, geom.tq
    nblk = geom.L // MOBA_BLOCK
    assert 0 < nblk <= LANE and MOBA_BLOCK % tq == 0 and geom.tk % MOBA_BLOCK == 0
    ksel = min(MOBA_TOPK, nblk)
    rows = nblk * MOBA_BLOCK
    avg = np.zeros((LANE, rows), np.float32)
    avg[np.arange(rows) // MOBA_BLOCK, np.arange(rows)] = 1.0 / MOBA_BLOCK
    return pl.pallas_call(
        functools.partial(_moba_select_kernel, geom=geom, nblk=nblk, ksel=ksel),
        grid=(b, H_C, geom.nqt),
        in_specs=[pl.BlockSpec((1, tq, HEAD_DIM), lambda i, h, q: (i, q, q_col + h)),
                  pl.BlockSpec((1, rows, HEAD_DIM), lambda i, h, q: (i, 0, k_col + h)),
                  pl.BlockSpec((LANE, rows), lambda i, h, q: (0, 0))],
        out_specs=pl.BlockSpec((1, 1, tq, LANE), lambda i, h, q: (i, h, q, 0)),
        out_shape=jax.ShapeDtypeStruct((b, H_C, T, LANE), F32),
        compiler_params=_params("parallel", "parallel", "arbitrary"),
    )(qarr, karr, jnp.asarray(avg, BF16))


def _moba_attn_kernel(q_ref, k_ref, v_ref, sel_ref, bias_ref, o_ref, m_ref, l_ref, acc_ref, *, geom):
    tq, tk = geom.tq, geom.tk
    qt, kt = pl.program_id(1), pl.program_id(2)
    qpos0 = geom.past_len + qt * tq
    cur = geom.q_tile(qt)

    @pl.when(kt == 0)
    def _():
        _softmax_init(m_ref, l_ref, acc_ref)

    @pl.when(kt <= cur)
    def _():
        qpos, kpos = _positions(qpos0, kt * tk, tq, tk)
        own_blk = qpos0 // MOBA_BLOCK
        blk = lax.broadcasted_iota(jnp.int32, (LANE, tk), 0)
        tok = lax.broadcasted_iota(jnp.int32, (LANE, tk), 1)
        expand = (blk == kt * (tk // MOBA_BLOCK) + tok // MOBA_BLOCK).astype(BF16)
        own = (kpos // MOBA_BLOCK == own_blk) & (kpos <= qpos)

        def step(add_bias):
            for h in range(H_C):
                cols = slice(h * HEAD_DIM, (h + 1) * HEAD_DIM)
                chosen = jnp.dot(sel_ref[0, h].astype(BF16), expand, preferred_element_type=F32) > 0.5
                s = _qk(q_ref[0, :, cols], k_ref[0, :, cols]) * SCALE
                if add_bias:
                    s = s + bias_ref[h, 0]
                _softmax_step(s, chosen | own, v_ref[0, :, cols], m_ref, l_ref, acc_ref, h)

        _near_or_far((cur - kt) <= 1, step)

    @pl.when(kt == geom.nkt - 1)
    def _():
        for h in range(H_C):
            o_ref[0, :, h * HEAD_DIM:(h + 1) * HEAD_DIM] = _softmax_out(l_ref, acc_ref, h).astype(o_ref.dtype)


def moba_attention(geom, qarr, q_col, karr, k_col, v_col, sel, bias):
    b, T, tq, tk, nkt = geom.b, geom.T, geom.tq, geom.tk, geom.nkt
    width = H_C * HEAD_DIM

    def kmap(col):
        return lambda i, q, k: (i, jnp.minimum(k, geom.q_tile(q)), col)

    return pl.pallas_call(
        functools.partial(_moba_attn_kernel, geom=geom),
        grid=(b, geom.nqt, nkt),
        in_specs=[pl.BlockSpec((1, tq, width), lambda i, q, k: (i, q, q_col)),
                  pl.BlockSpec((1, tk, width), kmap(k_col)),
                  pl.BlockSpec((1, tk, width), kmap(v_col)),
                  pl.BlockSpec((1, H_C, tq, LANE), lambda i, q, k: (i, 0, q, 0)),
                  _bias_spec(geom, H_C)],
        out_specs=pl.BlockSpec((1, tq, width), lambda i, q, k: (i, q, 0)),
        out_shape=jax.ShapeDtypeStruct((b, T, width), BF16),
        scratch_shapes=[pltpu.VMEM((H_C, tq, 1), F32), pltpu.VMEM((H_C, tq, 1), F32),
                        pltpu.VMEM((H_C, tq, HEAD_DIM), F32)],
        compiler_params=_params("parallel", "parallel", "arbitrary"),
    )(qarr, karr, karr, sel, bias)


CMP_ROW = CMP_STRIDE * 2 * KV_B * HEAD_DIM


def _compress_kernel(x_ref, xn_ref, wlo_ref, whi_ref, c_ref, w2_ref, g_ref, o_ref, *, tr):
    x = x_ref[0]
    lo = jnp.dot(x, wlo_ref[...], preferred_element_type=F32)
    hi = jnp.dot(x, whi_ref[...], preferred_element_type=F32)
    hi_next = jnp.dot(xn_ref[0], whi_ref[...], preferred_element_type=F32)
    row = lax.broadcasted_iota(jnp.int32, hi.shape, 0)
    hi = jnp.where(row == tr - 1, hi_next[0:1], pltpu.roll(hi, tr - 1, 0))
    act = jax.nn.gelu(lo + hi + c_ref[...]).astype(BF16)
    for c in range(2 * KV_B):
        cols = slice(c * HEAD_DIM, (c + 1) * HEAD_DIM)
        y = jnp.dot(act[:, cols], w2_ref[c // KV_B], preferred_element_type=F32)
        if c < KV_B:
            y = y * lax.rsqrt(jnp.mean(y * y, axis=-1, keepdims=True) + EPS) * g_ref[...]
        o_ref[0, :, cols] = y.astype(o_ref.dtype)


def nsa_compress(x, wlo, whi, const, w2, gain):
    b, R, _ = x.shape
    tr = _tile(R, 128, BF16_SUBLANE)
    nxt = tr // BF16_SUBLANE
    n_sub16 = R // BF16_SUBLANE
    C = 2 * KV_B * HEAD_DIM
    return pl.pallas_call(
        functools.partial(_compress_kernel, tr=tr),
        grid=(b, R // tr),
        in_specs=[pl.BlockSpec((1, tr, CMP_ROW), lambda i, r: (i, r, 0)),
                  pl.BlockSpec((1, BF16_SUBLANE, CMP_ROW), lambda i, r: (i, jnp.minimum((r + 1) * nxt, n_sub16 - 1), 0)),
                  pl.BlockSpec((CMP_ROW, C), lambda i, r: (0, 0)),
                  pl.BlockSpec((CMP_ROW, C), lambda i, r: (0, 0)),
                  pl.BlockSpec((1, C), lambda i, r: (0, 0)),
                  pl.BlockSpec((2, HEAD_DIM, HEAD_DIM), lambda i, r: (0, 0, 0)),
                  pl.BlockSpec((1, HEAD_DIM), lambda i, r: (0, 0))],
        out_specs=pl.BlockSpec((1, tr, C), lambda i, r: (i, r, 0)),
        out_shape=jax.ShapeDtypeStruct((b, R, C), BF16),
        compiler_params=_params("parallel", "arbitrary"),
    )(x, x, wlo, whi, const, w2, gain)


def _prep_compress(cmp_pos, cmp_w1, cmp_w2, kn_cmp):
    r = CMP_LEN // CMP_STRIDE
    assert r == 2
    w1 = cmp_w1.reshape(2, r, CMP_STRIDE, HEAD_DIM, HEAD_DIM)
    kv_of_col = np.repeat(np.arange(2), KV_B)
    eye = jnp.eye(2 * KV_B, dtype=cmp_w1.dtype)

    def expand(half):
        w = w1[kv_of_col, half]
        return jnp.einsum('crde,cf->rcdfe', w, eye).reshape(CMP_ROW, 2 * KV_B * HEAD_DIM).astype(BF16)

    const = jnp.einsum('krd,krde->ke', cmp_pos.reshape(2, CMP_LEN, HEAD_DIM),
                       cmp_w1.reshape(2, CMP_LEN, HEAD_DIM, HEAD_DIM), precision=lax.Precision.HIGHEST)
    const = const[kv_of_col].reshape(1, -1).astype(F32)
    return dict(wlo=expand(0), whi=expand(1), const=const, w2=cmp_w2.astype(BF16), gain=kn_cmp.reshape(1, -1).astype(F32))


def _nsa_cmp_kernel(q_ref, kc_ref, vc_ref, bias_ref, ov_ref, oc_ref, sel_ref, *, geom, rep, n_slc, nsel):
    tq = geom.tq
    qt = pl.program_id(2)
    qpos0 = geom.past_len + qt * tq
    kc, vc, ov = kc_ref[0], vc_ref[0], ov_ref[...]
    ncp, nsp = ov.shape
    qpos, n_idx = _positions(qpos0, 0, tq, ncp)
    valid = (n_idx * CMP_STRIDE + (CMP_LEN - 1)) <= qpos
    imp = jnp.zeros((tq, nsp), F32)
    for r in range(rep):
        cols = slice(r * HEAD_DIM, (r + 1) * HEAD_DIM)
        s = jnp.where(valid, _qk(q_ref[0, :, cols], kc) * SCALE + bias_ref[r], NEG)
        e = jnp.where(valid, jnp.exp(s - jnp.max(s, axis=-1, keepdims=True)), 0.0)
        p = (e / jnp.maximum(jnp.sum(e, axis=-1, keepdims=True), 1e-30)).astype(BF16)
        oc_ref[0, :, cols] = jnp.dot(p, vc, preferred_element_type=F32)
        imp = imp + jnp.dot(p, ov, preferred_element_type=F32)
    spos, jj = _positions(qpos0, 0, tq, nsp)
    cur = spos // SLC_LEN
    forced = (jj == 0) | (jj == cur) | (jj == cur - 1)
    imp = jnp.where(jj > cur, NEG, jnp.where(forced, -NEG, imp))
    cnt = jnp.zeros_like(imp)
    for s_blk in range(n_slc):
        col = imp[:, s_blk:s_blk + 1]
        cnt = cnt + ((col > imp) | ((col == imp) & (jj > s_blk))).astype(F32)
    sel_ref[0, 0] = ((cnt < nsel) & (jj < n_slc)).astype(F32)


def nsa_compressed(geom, qarr, q_col, cmpkv, bias_cmp):
    b, T, tq = geom.b, geom.T, geom.tq
    rep = H_B // KV_B
    ncp = cmpkv.shape[1]
    n_slc = -(-geom.L // SLC_LEN)
    nsel = min(N_SLC, n_slc)
    nsp = _round_up(n_slc, LANE)
    n_cmp = geom.L // CMP_STRIDE - CMP_LEN // CMP_STRIDE + 1
    cstart = np.arange(ncp) * CMP_STRIDE
    cend = cstart + CMP_LEN - 1
    sstart = np.arange(nsp) * SLC_LEN
    ov = ((cstart[:, None] < sstart[None, :] + SLC_LEN) & (cend[:, None] >= sstart[None, :])
          & (np.arange(ncp)[:, None] < n_cmp) & (np.arange(nsp)[None, :] < n_slc)).astype(np.float32)
    return pl.pallas_call(
        functools.partial(_nsa_cmp_kernel, geom=geom, rep=rep, n_slc=n_slc, nsel=nsel),
        grid=(b, KV_B, geom.nqt),
        in_specs=[pl.BlockSpec((1, tq, rep * HEAD_DIM), lambda i, g, q: (i, q, q_col + g)),
                  pl.BlockSpec((1, ncp, HEAD_DIM), lambda i, g, q: (i, 0, g)),
                  pl.BlockSpec((1, ncp, HEAD_DIM), lambda i, g, q: (i, 0, KV_B + g)),
                  pl.BlockSpec((rep, tq, ncp), lambda i, g, q: (g, q, 0)),
                  pl.BlockSpec((ncp, nsp), lambda i, g, q: (0, 0))],
        out_specs=[pl.BlockSpec((1, tq, rep * HEAD_DIM), lambda i, g, q: (i, q, g)),
                   pl.BlockSpec((1, 1, tq, nsp), lambda i, g, q: (i, g, q, 0))],
        out_shape=[jax.ShapeDtypeStruct((b, T, H_B * HEAD_DIM), F32), jax.ShapeDtypeStruct((b, KV_B, T, nsp), F32)],
        compiler_params=_params("parallel", "parallel", "arbitrary"),
    )(qarr, cmpkv, cmpkv, bias_cmp, jnp.asarray(ov, BF16))


class _Window:
    def __init__(self, arr, k_col, v_col, tw, n_back, tile0, pos0, bias, bias_always):
        self.arr, self.k_col, self.v_col, self.tw, self.n_back = arr, k_col, v_col, tw, n_back
        self.tile0, self.pos0, self.bias, self.bias_always = tile0, pos0, bias, bias_always


def _nsa_attn_kernel(q_ref, ks_ref, vs_ref, kw_ref, vw_ref, sel_ref, bias_ref, wbias_ref, oc_ref, misc_ref, o_ref,
                     m_ref, l_ref, acc_ref, *, geom, rep, win):
    tq, tk = geom.tq, geom.tk
    qt, kt = pl.program_id(1), pl.program_id(2)
    qpos0 = geom.past_len + qt * tq
    last = geom.q_tile(qt)

    @pl.when(kt == 0)
    def _():
        _softmax_init(m_ref, l_ref, acc_ref)

    def branch(g, k_ref, v_ref, mask, slot0, b_ref, add_bias):
        kv_cols = slice(g * HEAD_DIM, (g + 1) * HEAD_DIM)
        for h in range(g * rep, (g + 1) * rep):
            s = _qk(q_ref[0, :, h * HEAD_DIM:(h + 1) * HEAD_DIM], k_ref[0, :, kv_cols]) * SCALE
            if add_bias:
                s = s + b_ref[h, 0]
            _softmax_step(s, mask, v_ref[0, :, kv_cols], m_ref, l_ref, acc_ref, slot0 + h)

    near = (last - kt) <= 1

    @pl.when(kt <= last)
    def _():
        qpos, kpos = _positions(qpos0, kt * tk, tq, tk)
        nsp = sel_ref.shape[-1]
        blk = lax.broadcasted_iota(jnp.int32, (nsp, tk), 0)
        tok = lax.broadcasted_iota(jnp.int32, (nsp, tk), 1)
        expand = (blk == kt * (tk // SLC_LEN) + tok // SLC_LEN).astype(BF16)

        def step(add_bias):
            for g in range(KV_B):
                chosen = jnp.dot(sel_ref[0, g].astype(BF16), expand, preferred_element_type=F32) > 0.5
                branch(g, ks_ref, vs_ref, chosen & (kpos <= qpos), 0, bias_ref, add_bias)

        _near_or_far(near, step)

    @pl.when((kt <= last) & (kt >= last - win.n_back))
    def _():
        qpos, kpos = _positions(qpos0, win.pos0 + (kt - win.tile0) * win.tw, tq, win.tw)
        dist = qpos - kpos

        def step(add_bias):
            for g in range(KV_B):
                branch(g, kw_ref, vw_ref, (dist >= 0) & (dist < WINDOW), H_B, wbias_ref, add_bias)

        if win.bias_always:
            step(True)
        else:
            _near_or_far(near, step)

    @pl.when(kt == geom.nkt - 1)
    def _():
        gates = jax.nn.sigmoid(misc_ref[0])
        for h in range(H_B):
            cols = slice(h * HEAD_DIM, (h + 1) * HEAD_DIM)
            base = H_IDX + h * N_BRANCH
            o = (gates[:, base:base + 1] * oc_ref[0, :, cols]
                 + gates[:, base + 1:base + 2] * _softmax_out(l_ref, acc_ref, h)
                 + gates[:, base + 2:base + 3] * _softmax_out(l_ref, acc_ref, H_B + h))
            o_ref[0, :, cols] = o.astype(o_ref.dtype)


def nsa_attention(geom, qarr, q_col, sarr, ks_col, vs_col, win, sel, bias, oc, miscarr, misc_col):
    b, T, tq, tk, nkt = geom.b, geom.T, geom.tq, geom.tk, geom.nkt
    nsp = sel.shape[-1]
    tw = win.tw
    n_wt = win.arr.shape[1] // tw
    kv_width = KV_B * HEAD_DIM

    def smap(col):
        return lambda i, q, k: (i, jnp.minimum(k, geom.q_tile(q)), col)

    def wtile(q, k):
        last = geom.q_tile(q)
        return jnp.clip(jnp.clip(k, last - win.n_back, last) - win.tile0, 0, n_wt - 1)

    def wmap(col):
        return lambda i, q, k: (i, wtile(q, k), col)

    if win.bias_always:
        assert win.bias.shape[1] == 1 and n_wt == 1
        wbias_spec = pl.BlockSpec((H_B, 1, tq, tw), lambda i, q, k: (0, 0, 0, 0))
    else:
        wbias_spec = _bias_spec(geom, H_B)

    return pl.pallas_call(
        functools.partial(_nsa_attn_kernel, geom=geom, rep=H_B // KV_B, win=win),
        grid=(b, geom.nqt, nkt),
        in_specs=[pl.BlockSpec((1, tq, H_B * HEAD_DIM), lambda i, q, k: (i, q, q_col)),
                  pl.BlockSpec((1, tk, kv_width), smap(ks_col)),
                  pl.BlockSpec((1, tk, kv_width), smap(vs_col)),
                  pl.BlockSpec((1, tw, kv_width), wmap(win.k_col)),
                  pl.BlockSpec((1, tw, kv_width), wmap(win.v_col)),
                  pl.BlockSpec((1, KV_B, tq, nsp), lambda i, q, k: (i, 0, q, 0)),
                  _bias_spec(geom, H_B),
                  wbias_spec,
                  pl.BlockSpec((1, tq, H_B * HEAD_DIM), lambda i, q, k: (i, q, 0)),
                  pl.BlockSpec((1, tq, LANE), lambda i, q, k: (i, q, misc_col))],
        out_specs=pl.BlockSpec((1, tq, H_B * HEAD_DIM), lambda i, q, k: (i, q, 0)),
        out_shape=jax.ShapeDtypeStruct((b, T, H_B * HEAD_DIM), BF16),
        scratch_shapes=[pltpu.VMEM((2 * H_B, tq, 1), F32), pltpu.VMEM((2 * H_B, tq, 1), F32),
                        pltpu.VMEM((2 * H_B, tq, HEAD_DIM), F32)],
        compiler_params=_params("parallel", "parallel", "arbitrary"),
    )(qarr, sarr, sarr, win.arr, win.arr, sel, bias, win.bias, oc, miscarr)


_SEG = {}
_off = 0
for _name, _w in (("qa", H_A * HEAD_DIM), ("qb", H_B * HEAD_DIM), ("qi", H_IDX * D_IDX), ("qc", H_C * HEAD_DIM),
                  ("kvc", 2 * H_C * HEAD_DIM), ("kva", 2 * KV_A * HEAD_DIM), ("cmp", 2 * KV_B * HEAD_DIM),
                  ("slc", 2 * KV_B * HEAD_DIM), ("win", 2 * KV_B * HEAD_DIM), ("ki", D_IDX), ("misc", LANE)):
    _SEG[_name] = (_off, _off + _w)
    _off += _w
IN_COLS_ALIGNED = _off


def _col(name, width=LANE):
    assert _SEG[name][0] % width == 0
    return _SEG[name][0] // width


def _prep_in_proj(w_in, qn_a, kn_a, qn_b, kn_b, qn_c, kn_c):
    sizes = (H_A * HEAD_DIM, KV_A * HEAD_DIM, KV_A * HEAD_DIM, H_IDX * D_IDX, H_IDX, D_IDX,
             H_B * HEAD_DIM, N_BRANCH * H_B) + (KV_B * HEAD_DIM,) * 6 + (H_C * HEAD_DIM,) * 3
    splits = np.cumsum(sizes)[:-1].tolist()
    (qa, ka, va, qi, wi, ki, q_b, g_b, kbc, vbc, kbs, vbs, kbw, vbw, qc, kc, vc) = jnp.split(w_in, splits, axis=-1)
    D = w_in.shape[0]
    misc = jnp.concatenate([wi, g_b, jnp.zeros((D, LANE - H_IDX - N_BRANCH * H_B), w_in.dtype)], axis=-1)
    n_pad = _round_up(IN_COLS_ALIGNED, 512) - IN_COLS_ALIGNED
    w = jnp.concatenate([qa, q_b, qi, qc, kc, vc, ka, va, kbc, vbc, kbs, vbs, kbw, vbw, ki, misc,
                         jnp.zeros((D, n_pad), w_in.dtype)], axis=-1).astype(BF16)
    one = jnp.ones((HEAD_DIM,), F32)

    def rep(g, n):
        return jnp.tile(g.astype(F32), n)

    gains = jnp.concatenate([
        rep(qn_a, H_A), rep(qn_b, H_B), rep(one, H_IDX), rep(qn_c, H_C), rep(kn_c, H_C), rep(one, H_C),
        rep(kn_a, KV_A), rep(one, KV_A), rep(one, 2 * KV_B), rep(kn_b[1], KV_B), rep(one, KV_B),
        rep(kn_b[2], KV_B), rep(one, KV_B), one, one, jnp.ones((n_pad,), F32)]).reshape(1, -1)
    flags = np.concatenate([
        np.ones(H_A), np.ones(H_B), np.zeros(H_IDX), np.ones(H_C), np.ones(H_C), np.zeros(H_C),
        np.ones(KV_A), np.zeros(KV_A), np.zeros(2 * KV_B), np.ones(KV_B), np.zeros(KV_B),
        np.ones(KV_B), np.zeros(KV_B), np.zeros(1), np.zeros(1), np.zeros(n_pad // LANE)]).astype(np.int32)
    return w, gains, jnp.asarray(flags)


def _prep_layer(i, ffn1_norm, ffn1_wg, ffn1_wu, ffn1_wd, mix_norm, w_in, qn_a, kn_a, qn_b, kn_b, cmp_pos, cmp_w1,
                cmp_w2, qn_c, kn_c, wb_a, wb_b, wb_c, w_gate, b_gate, w_out, ffn2_norm, ffn2_wg, ffn2_wu, ffn2_wd,
                ple_norm, ple_wg, ple_wp):
    def up(w):
        return w[i].astype(BF16)

    down = up

    w_in_p, gains, flags = _prep_in_proj(w_in[i], qn_a[i], kn_a[i], qn_b[i], kn_b[i], qn_c[i], kn_c[i])
    return dict(
        f1n=ffn1_norm[i], f1g=up(ffn1_wg), f1u=up(ffn1_wu), f1d=down(ffn1_wd), mn=mix_norm[i],
        w_in=w_in_p, gains=gains, flags=flags,
        cmp=_prep_compress(cmp_pos[i], cmp_w1[i], cmp_w2[i], kn_b[i, 0]),
        wb_a=wb_a[i].astype(BF16), wb_b=wb_b[i].astype(BF16), wb_c=wb_c[i].astype(BF16),
        w_gate=w_gate[i].astype(BF16), b_gate=b_gate[i].reshape(1, -1), w_out=w_out[i].astype(BF16),
        f2n=ffn2_norm[i], f2g=up(ffn2_wg), f2u=up(ffn2_wu), f2d=down(ffn2_wd),
        pn=ple_norm[i], pwg=ple_wg[i].astype(BF16), pwp=ple_wp[i].astype(BF16))


def _layer(x, p, geom, layer, caches, W, bias):
    b, T = geom.b, geom.T
    M = b * T
    h = ffn_down(ffn_up(rmsnorm_bf16(x, W["f1n"]), W["f1g"], W["f1u"]), W["f1d"], x)
    u = rmsnorm_bf16(h, W["mn"])
    proj32, proj16 = in_proj(u, W["w_in"], W["gains"], W["flags"])
    Np = proj32.shape[1]
    p32 = proj32.reshape(b, T, Np)
    p16 = proj16.reshape(b, T, Np)

    def seg32(name):
        lo, hi = _SEG[name]
        return p32[:, :, lo:hi]

    def seg16(name):
        lo, hi = _SEG[name]
        return p16[:, :, lo:hi]

    win_new = seg32("win")
    if caches is None:
        assert T >= WINDOW
        a_kv, a_ki, slc, c_kv = (p16,) * 4
        a_k, ki_c = _col("kva", KV_A * HEAD_DIM), _col("ki")
        s_k = _col("slc", KV_B * HEAD_DIM)
        c_k = _col("kvc", H_C * HEAD_DIM)
        w_k = _col("win", KV_B * HEAD_DIM)
        win = _Window(p16, w_k, w_k + 1, geom.tk, WINDOW // geom.tk, 0, 0, bias["b"], False)
        cmp_rows = seg16("cmp")
        win_state = win_new[:, T - WINDOW:]
    else:
        cache_a_kv, cache_a_kidx, cache_b_cmp, cache_b_slc, win_past, cache_c_kv, page_table = caches
        a_kv = assemble_cache(cache_a_kv, layer, page_table, seg16("kva"), geom.tk)
        a_ki = assemble_cache(cache_a_kidx, layer, page_table, seg16("ki"), geom.tk)
        cmp_rows = assemble_cache(cache_b_cmp, layer, page_table, seg16("cmp"), geom.tk)
        slc = assemble_cache(cache_b_slc, layer, page_table, seg16("slc"), geom.tk)
        c_kv = assemble_cache(cache_c_kv, layer, page_table, seg16("kvc"), geom.tk)
        a_k, ki_c, s_k, c_k = 0, 0, 0, 0
        n_buf = win_past.shape[2]
        win_past = win_past[layer].reshape(b, n_buf, -1)
        warr = jnp.concatenate([win_past.astype(BF16), seg16("win"),
                                jnp.zeros((b, KEY_TILE - T, win_past.shape[-1]), BF16)], axis=1)
        win = _Window(warr, 0, 1, warr.shape[1], 0, geom.q_tile(0), geom.past_len - n_buf, bias["b_win"], True)
        win_state = jnp.concatenate([win_past, win_new], axis=1)[:, -WINDOW:]

    scores, thr = dsa_indexer(geom, p16, _col("qi", H_IDX * D_IDX), p32, _col("misc"), a_ki, ki_c)
    o_a = dsa_attention(geom, p16, _col("qa", H_A * HEAD_DIM), a_kv, a_k, a_k + 1, scores, thr, bias["a"])
    cmpkv = nsa_compress(cmp_rows.reshape(b, -1, CMP_ROW), **W["cmp"])
    qpos = geom.past_len + jnp.arange(T)
    cend = jnp.arange(cmpkv.shape[1]) * CMP_STRIDE + CMP_LEN - 1
    bias_cmp = _bias_lookup(bias["tab_b"], qpos[:, None] - cend[None, :])
    oc, sel_b = nsa_compressed(geom, p16, _col("qb", 4 * HEAD_DIM), cmpkv, bias_cmp)
    o_b = nsa_attention(geom, p16, _col("qb", H_B * HEAD_DIM), slc, s_k, s_k + 1, win, sel_b, bias["b"], oc, p32,
                        _col("misc"))
    sel_c = moba_select(geom, p16, _col("qc"), c_kv, c_k * H_C)
    o_c = moba_attention(geom, p16, _col("qc", H_C * HEAD_DIM), c_kv, c_k, c_k + 1, sel_c, bias["c"])

    merged = gate_merge(u, o_a.reshape(M, -1), o_b.reshape(M, -1), o_c.reshape(M, -1),
                        W["w_gate"], W["b_gate"], W["wb_a"], W["wb_b"], W["wb_c"])
    h = out_proj(merged, W["w_out"], h)
    h = ffn_down(ffn_up(rmsnorm_bf16(h, W["f2n"]), W["f2g"], W["f2u"]), W["f2d"], h)
    h = ple(rmsnorm_bf16(h, W["pn"]), W["pwg"], p.astype(BF16), W["pwp"], h)
    state = (seg32("kva").reshape(b, T, 2, KV_A, HEAD_DIM), seg32("ki"), seg32("cmp").reshape(b, T, 2, KV_B, HEAD_DIM),
             seg32("slc").reshape(b, T, 2, KV_B, HEAD_DIM), win_state.reshape(b, WINDOW, 2, KV_B, HEAD_DIM),
             seg32("kvc").reshape(b, T, 2, H_C, HEAD_DIM))
    return h, state


def kernel(x_prompt, x_sample, cache_a_kv, cache_a_kidx, cache_b_cmp_kv, cache_b_slc_kv, state_b_win_kv, cache_c_kv, page_table, p_prompt, p_sample, rel_bias, ffn1_norm, ffn1_wg, ffn1_wu, ffn1_wd, mix_norm, w_in, qn_a, kn_a, qn_b, kn_b, cmp_pos, cmp_w1, cmp_w2, qn_c, kn_c, wb_a, wb_b, wb_c, w_gate, b_gate, w_out, ffn2_norm, ffn2_wg, ffn2_wu, ffn2_wd, ple_norm, ple_wg, ple_wp):
    depth = ffn1_norm.shape[0]
    bp, Tp, D = x_prompt.shape
    bs, Ts, _ = x_sample.shape
    geom_p = _Geom(bp, Tp, 0)
    geom_s = _Geom(bs, Ts, page_table.shape[1] * PAGE_SIZE)
    n_buf = state_b_win_kv.shape[2]
    assert n_buf >= WINDOW and Ts <= KEY_TILE
    tab_a, tab_b, tab_c = rel_bias[:, :H_A], rel_bias[:, H_A:H_A + H_B], rel_bias[:, H_A + H_B:]

    def tile_bias(geom):
        near = (0, geom.tk)
        return dict(a=_toeplitz_bias(tab_a, geom.tq, geom.tk, near), b=_toeplitz_bias(tab_b, geom.tq, geom.tk, near),
                    c=_toeplitz_bias(tab_c, geom.tq, geom.tk, near), tab_b=tab_b)

    bias_p = tile_bias(geom_p)
    bias_s = tile_bias(geom_s)
    bias_s["b_win"] = _toeplitz_bias(tab_b, geom_s.tq, n_buf + KEY_TILE, (n_buf,))
    caches = (cache_a_kv, cache_a_kidx, cache_b_cmp_kv, cache_b_slc_kv, state_b_win_kv, cache_c_kv, page_table)

    y_p = x_prompt.reshape(bp * Tp, D)
    y_s = x_sample.reshape(bs * Ts, D)
    sp_list, ss_list = [], []
    for i in range(depth):
        W = _prep_layer(i, ffn1_norm, ffn1_wg, ffn1_wu, ffn1_wd, mix_norm, w_in, qn_a, kn_a, qn_b, kn_b, cmp_pos,
                        cmp_w1, cmp_w2, qn_c, kn_c, wb_a, wb_b, wb_c, w_gate, b_gate, w_out, ffn2_norm, ffn2_wg,
                        ffn2_wu, ffn2_wd, ple_norm, ple_wg, ple_wp)
        y_p, sp = _layer(y_p, p_prompt[i].reshape(bp * Tp, -1), geom_p, i, None, W, bias_p)
        y_s, ss = _layer(y_s, p_sample[i].reshape(bs * Ts, -1), geom_s, i, caches, W, bias_s)
        sp_list.append(sp)
        ss_list.append(ss)

    outs = [y_p.reshape(bp, Tp, D), y_s.reshape(bs, Ts, D)]
    for j in range(6):
        outs.append(jnp.stack([s[j] for s in sp_list]))
        outs.append(jnp.stack([s[j] for s in ss_list]))
    return tuple(outs)
```

```python
import functools
import math

import numpy as np
import jax
import jax.numpy as jnp
from jax import lax
from jax.experimental import pallas as pl
from jax.experimental.pallas import tpu as pltpu

PAGE_SIZE = 128
HEAD_DIM = 128
H_A = 8
KV_A = 2
H_IDX = 16
D_IDX = 128
TOPK_A = 256
H_B = 8
KV_B = 2
CMP_LEN = 32
CMP_STRIDE = 16
SLC_LEN = 64
N_SLC = 16
WINDOW = 512
H_C = 8
MOBA_BLOCK = 256
MOBA_TOPK = 3
N_BUCKETS = 32
MAX_DISTANCE = 128
N_BRANCH = 3
EPS = 1e-6
NEG = -1e30

LANE = 128
BF16_SUBLANE = 16
V7X_VMEM_LIMIT = 56 * 1024 * 1024

F32 = jnp.float32
BF16 = jnp.bfloat16


def _tile(n, target, align):
    best = None
    for t in range(align, min(n, target) + 1, align):
        if n % t == 0:
            best = t
    return best if best is not None else n


def _round_up(n, m):
    return -(-n // m) * m


def _params(*sem):
    return pltpu.CompilerParams(dimension_semantics=sem, vmem_limit_bytes=V7X_VMEM_LIMIT)


def _rmsnorm_kernel(x_ref, g_ref, o_ref):
    x = x_ref[...]
    y = x * lax.rsqrt(jnp.mean(x * x, axis=-1, keepdims=True) + EPS)
    o_ref[...] = (y * g_ref[...]).astype(o_ref.dtype)


def rmsnorm_bf16(x, g):
    M, D = x.shape
    tm = _tile(M, 256, BF16_SUBLANE)
    return pl.pallas_call(
        _rmsnorm_kernel,
        grid=(M // tm,),
        in_specs=[pl.BlockSpec((tm, D), lambda i: (i, 0)), pl.BlockSpec((1, D), lambda i: (0, 0))],
        out_specs=pl.BlockSpec((tm, D), lambda i: (i, 0)),
        out_shape=jax.ShapeDtypeStruct((M, D), BF16),
        compiler_params=_params("parallel"),
    )(x, g.reshape(1, D))


def _ffn_up_kernel(a_ref, wg_ref, wu_ref, o_ref):
    a = a_ref[...]
    g = jnp.dot(a, wg_ref[...].astype(BF16), preferred_element_type=F32)
    u = jnp.dot(a, wu_ref[...].astype(BF16), preferred_element_type=F32)
    o_ref[...] = (g * jax.nn.sigmoid(g) * u).astype(o_ref.dtype)


def ffn_up(a, wg, wu, layer):
    M, D = a.shape
    F = wg.shape[2]
    tm = _tile(M, 1024, BF16_SUBLANE)
    tn = _tile(F, 256, LANE)
    return pl.pallas_call(
        _ffn_up_kernel,
        grid=(M // tm, F // tn),
        in_specs=[pl.BlockSpec((tm, D), lambda i, j: (i, 0)),
                  pl.BlockSpec((None, D, tn), lambda i, j: (layer, 0, j)),
                  pl.BlockSpec((None, D, tn), lambda i, j: (layer, 0, j))],
        out_specs=pl.BlockSpec((tm, tn), lambda i, j: (i, j)),
        out_shape=jax.ShapeDtypeStruct((M, F), BF16),
        compiler_params=_params("parallel", "arbitrary"),
    )(a, wg, wu)


def _ffn_down_kernel(a_ref, w_ref, x_ref, o_ref, acc_ref):
    k = pl.program_id(2)

    @pl.when(k == 0)
    def _():
        acc_ref[...] = jnp.zeros_like(acc_ref)

    acc_ref[...] += jnp.dot(a_ref[...], w_ref[...].astype(BF16), preferred_element_type=F32)

    @pl.when(k == pl.num_programs(2) - 1)
    def _():
        o_ref[...] = x_ref[...] + 0.5 * acc_ref[...]


def ffn_down(a, w, x, layer):
    M, F = a.shape
    D = w.shape[2]
    tm = _tile(M, 1024, BF16_SUBLANE)
    tn = _tile(D, 256, LANE)
    tk = _tile(F, 5504, LANE)
    return pl.pallas_call(
        _ffn_down_kernel,
        grid=(M // tm, D // tn, F // tk),
        in_specs=[pl.BlockSpec((tm, tk), lambda i, j, k: (i, k)),
                  pl.BlockSpec((None, tk, tn), lambda i, j, k: (layer, k, j)),
                  pl.BlockSpec((tm, tn), lambda i, j, k: (i, j))],
        out_specs=pl.BlockSpec((tm, tn), lambda i, j, k: (i, j)),
        out_shape=jax.ShapeDtypeStruct((M, D), F32),
        scratch_shapes=[pltpu.VMEM((tm, tn), F32)],
        compiler_params=_params("parallel", "arbitrary", "arbitrary"),
    )(a, w, x)


def _w_in_kernel(flags_ref, a_ref, w_ref, g_ref, o_ref, o16_ref, *, heads_per_tile):
    j = pl.program_id(1)
    acc = jnp.dot(a_ref[...], w_ref[...], preferred_element_type=F32)
    for c in range(heads_per_tile):
        cols = slice(c * HEAD_DIM, (c + 1) * HEAD_DIM)
        y = acc[:, cols]
        yn = y * lax.rsqrt(jnp.mean(y * y, axis=-1, keepdims=True) + EPS) * g_ref[:, cols]
        y = jnp.where(flags_ref[j * heads_per_tile + c] > 0, yn, y)
        o_ref[:, cols] = y
        o16_ref[:, cols] = y.astype(o16_ref.dtype)


def in_proj(a, w, gains, flags):
    M, D = a.shape
    N = w.shape[1]
    tm = _tile(M, 1024, BF16_SUBLANE)
    tn = _tile(N, 512, LANE)
    grid_spec = pltpu.PrefetchScalarGridSpec(
        num_scalar_prefetch=1,
        grid=(M // tm, N // tn),
        in_specs=[pl.BlockSpec((tm, D), lambda i, j, f: (i, 0)),
                  pl.BlockSpec((D, tn), lambda i, j, f: (0, j)),
                  pl.BlockSpec((1, tn), lambda i, j, f: (0, j))],
        out_specs=[pl.BlockSpec((tm, tn), lambda i, j, f: (i, j)), pl.BlockSpec((tm, tn), lambda i, j, f: (i, j))],
    )
    return pl.pallas_call(
        functools.partial(_w_in_kernel, heads_per_tile=tn // HEAD_DIM),
        grid_spec=grid_spec,
        out_shape=[jax.ShapeDtypeStruct((M, N), F32), jax.ShapeDtypeStruct((M, N), BF16)],
        compiler_params=_params("parallel", "arbitrary"),
    )(flags, a, w, gains)


def _gate_merge_kernel(u_ref, oa_ref, ob_ref, oc_ref, wg0_ref, wg1_ref, wg2_ref, bg0_ref, bg1_ref, bg2_ref,
                       wba_ref, wbb_ref, wbc_ref, o_ref):
    u = u_ref[...]
    out = None
    for o_r, wg_r, bg_r, wb_r in ((oa_ref, wg0_ref, bg0_ref, wba_ref),
                                  (ob_ref, wg1_ref, bg1_ref, wbb_ref),
                                  (oc_ref, wg2_ref, bg2_ref, wbc_ref)):
        gate = jax.nn.sigmoid(jnp.dot(u, wg_r[...], preferred_element_type=F32) + bg_r[...])
        branch = jnp.dot(o_r[...], wb_r[...], preferred_element_type=F32)
        out = gate * branch if out is None else out + gate * branch
    o_ref[...] = out.astype(o_ref.dtype)


def gate_merge(u, o_a, o_b, o_c, w_gate, b_gate, wb_a, wb_b, wb_c):
    M, D = u.shape
    Ho = o_a.shape[1]
    tm = _tile(M, 1024, BF16_SUBLANE)
    tn = _tile(D, 256, LANE)
    nj = D // tn
    a_spec = pl.BlockSpec((tm, D), lambda i, j: (i, 0))
    o_spec = pl.BlockSpec((tm, Ho), lambda i, j: (i, 0))
    wb_spec = pl.BlockSpec((Ho, tn), lambda i, j: (0, j))

    def branch_spec(rows, br):
        return pl.BlockSpec((rows, tn), lambda i, j: (0, br * nj + j))

    return pl.pallas_call(
        _gate_merge_kernel,
        grid=(M // tm, nj),
        in_specs=[a_spec, o_spec, o_spec, o_spec,
                  branch_spec(D, 0), branch_spec(D, 1), branch_spec(D, 2),
                  branch_spec(1, 0), branch_spec(1, 1), branch_spec(1, 2),
                  wb_spec, wb_spec, wb_spec],
        out_specs=pl.BlockSpec((tm, tn), lambda i, j: (i, j)),
        out_shape=jax.ShapeDtypeStruct((M, D), BF16),
        compiler_params=_params("parallel", "arbitrary"),
    )(u, o_a, o_b, o_c, w_gate, w_gate, w_gate, b_gate, b_gate, b_gate, wb_a, wb_b, wb_c)


def _out_proj_kernel(a_ref, w_ref, x_ref, o_ref):
    o_ref[...] = x_ref[...] + jnp.dot(a_ref[...], w_ref[...], preferred_element_type=F32)


def out_proj(a, w, x):
    M, K = a.shape
    N = w.shape[1]
    tm = _tile(M, 1024, BF16_SUBLANE)
    tn = _tile(N, 512, LANE)
    return pl.pallas_call(
        _out_proj_kernel,
        grid=(M // tm, N // tn),
        in_specs=[pl.BlockSpec((tm, K), lambda i, j: (i, 0)),
                  pl.BlockSpec((K, tn), lambda i, j: (0, j)),
                  pl.BlockSpec((tm, tn), lambda i, j: (i, j))],
        out_specs=pl.BlockSpec((tm, tn), lambda i, j: (i, j)),
        out_shape=jax.ShapeDtypeStruct((M, N), F32),
        compiler_params=_params("parallel", "arbitrary"),
    )(a, w, x)


def _ple_kernel(a_ref, wg_ref, p_ref, wp_ref, x_ref, o_ref):
    gate = jax.nn.sigmoid(jnp.dot(a_ref[...], wg_ref[...], preferred_element_type=F32))
    emb = jnp.dot(p_ref[...], wp_ref[...], preferred_element_type=F32)
    o_ref[...] = x_ref[...] + gate * emb


def ple(a, wg, p, wp, x):
    M, K = a.shape
    N = wg.shape[1]
    P = p.shape[1]
    tm = _tile(M, 1024, BF16_SUBLANE)
    tn = _tile(N, 512, LANE)
    return pl.pallas_call(
        _ple_kernel,
        grid=(M // tm, N // tn),
        in_specs=[pl.BlockSpec((tm, K), lambda i, j: (i, 0)),
                  pl.BlockSpec((K, tn), lambda i, j: (0, j)),
                  pl.BlockSpec((tm, P), lambda i, j: (i, 0)),
                  pl.BlockSpec((P, tn), lambda i, j: (0, j)),
                  pl.BlockSpec((tm, tn), lambda i, j: (i, j))],
        out_specs=pl.BlockSpec((tm, tn), lambda i, j: (i, j)),
        out_shape=jax.ShapeDtypeStruct((M, N), F32),
        compiler_params=_params("parallel", "arbitrary"),
    )(a, wg, p, wp, x)


SCALE = HEAD_DIM ** -0.5
KEY_TILE = 256
QUERY_TILE = 256
SHORT_QUERY_KEY_TILE = 2048
ASSEMBLE_PAGES = 8
INT32_MIN = -2 ** 31

_NT = (((1,), (1,)), ((), ()))


def _qk(q, k):
    return lax.dot_general(q, k, _NT, preferred_element_type=F32)


def _softmax_init(m_ref, l_ref, acc_ref):
    m_ref[...] = jnp.full(m_ref.shape, NEG, F32)
    l_ref[...] = jnp.zeros(l_ref.shape, F32)
    acc_ref[...] = jnp.zeros(acc_ref.shape, F32)


def _softmax_step(s, mask, v, m_ref, l_ref, acc_ref, r):
    s = jnp.where(mask, s, NEG)
    m_prev = m_ref[r]
    m_new = jnp.maximum(m_prev, jnp.max(s, axis=-1, keepdims=True))
    p = jnp.where(mask, jnp.exp(s - m_new), 0.0)
    alpha = jnp.exp(m_prev - m_new)
    l_ref[r] = alpha * l_ref[r] + jnp.sum(p, axis=-1, keepdims=True)
    acc_ref[r] = alpha * acc_ref[r] + jnp.dot(p.astype(BF16), v, preferred_element_type=F32)
    m_ref[r] = m_new


def _near_or_far(near, step):
    @pl.when(near)
    def _():
        step(True)

    @pl.when(jnp.logical_not(near))
    def _():
        step(False)


def _softmax_out(l_ref, acc_ref, r):
    return acc_ref[r] / jnp.maximum(l_ref[r], 1e-30)


def _positions(qpos0, kpos0, tq, tk):
    qpos = qpos0 + lax.broadcasted_iota(jnp.int32, (tq, tk), 0)
    kpos = kpos0 + lax.broadcasted_iota(jnp.int32, (tq, tk), 1)
    return qpos, kpos


def _lane_column(x, idx):
    lane = lax.broadcasted_iota(jnp.int32, x.shape, 1)
    return jnp.sum(jnp.where(lane == idx, x, 0.0), axis=-1, keepdims=True)


def _t5_bucket(n):
    n = jnp.maximum(n, 0)
    exact = N_BUCKETS // 2
    nf = jnp.maximum(n, 1).astype(jnp.float32)
    big = exact + (jnp.log(nf / exact) * ((N_BUCKETS - exact) / math.log(MAX_DISTANCE / exact))).astype(jnp.int32)
    return jnp.where(n < exact, n, jnp.minimum(big, N_BUCKETS - 1))


def _bias_lookup(tab, dist):
    onehot = jax.nn.one_hot(_t5_bucket(dist), N_BUCKETS, dtype=F32)
    return jnp.einsum('...n,nh->h...', onehot, tab.astype(F32), precision=lax.Precision.HIGHEST)


def _toeplitz_bias(tab, tq, tk, offsets):
    i = jnp.arange(tq)[:, None]
    j = jnp.arange(tk)[None, :]
    d = jnp.stack([off + i - j for off in offsets])
    return _bias_lookup(tab, d) - tab[N_BUCKETS - 1].astype(F32)[:, None, None, None]


class _Geom:
    def __init__(self, b, T, past_len):
        self.b, self.T, self.past_len = b, T, past_len
        self.tq = min(QUERY_TILE, T)
        self.tk = KEY_TILE if T >= QUERY_TILE else _tile(past_len, SHORT_QUERY_KEY_TILE, KEY_TILE)
        assert T % self.tq == 0 and self.tk % self.tq == 0 and past_len % self.tk == 0
        assert self.tq == self.tk or T == self.tq, "a query tile must not straddle key tiles"
        assert MAX_DISTANCE <= self.tk, "bias must be constant two key tiles behind the query tile"
        self.nqt = T // self.tq
        self.L = past_len + T
        self.nkt = (self.L - 1) // self.tk + 1

    def q_tile(self, qt):
        return (self.past_len + qt * self.tq) // self.tk


def _bias_spec(geom, heads):
    return pl.BlockSpec((heads, 1, geom.tq, geom.tk),
                        lambda i, q, k: (0, jnp.clip(geom.q_tile(q) - k, 0, 1), 0, 0))


def _assemble_kernel(pt_ref, *refs, n_in, n_page_steps, T, R):
    ins, new_ref, o_ref = refs[:n_in], refs[n_in], refs[n_in + 1]
    s = pl.program_id(1)

    @pl.when(s < n_page_steps)
    def _():
        for p in range(n_in):
            for c in range(R):
                rows = ins[p][0, 0, pl.ds(c, PAGE_SIZE, stride=R), :] if R > 1 else ins[p][0, 0]
                o_ref[0, p * PAGE_SIZE:(p + 1) * PAGE_SIZE, c * HEAD_DIM:(c + 1) * HEAD_DIM] = rows.astype(o_ref.dtype)

    @pl.when(s >= n_page_steps)
    def _():
        o_ref[0] = jnp.zeros(o_ref.shape[1:], o_ref.dtype)

    @pl.when(s == n_page_steps)
    def _():
        o_ref[0, :T, :] = new_ref[0]


def assemble_cache(cache, layer, page_table, new, pad_rows):
    b, n_pages = page_table.shape
    T, C = new.shape[1], new.shape[2]
    R = C // HEAD_DIM
    cache = cache.reshape(cache.shape[0], cache.shape[1], PAGE_SIZE * R, HEAD_DIM)
    P = _tile(n_pages, ASSEMBLE_PAGES, 1)
    steps = n_pages // P
    assert pad_rows % (P * PAGE_SIZE) == 0 and T <= P * PAGE_SIZE
    pad_steps = pad_rows // (P * PAGE_SIZE)

    def page_spec(p):
        return pl.BlockSpec((1, 1, PAGE_SIZE * R, HEAD_DIM),
                            lambda i, s, pt: (layer, pt[i, jnp.minimum(s * P + p, n_pages - 1)], 0, 0))

    grid_spec = pltpu.PrefetchScalarGridSpec(
        num_scalar_prefetch=1,
        grid=(b, steps + pad_steps),
        in_specs=[page_spec(p) for p in range(P)] + [pl.BlockSpec((1, T, C), lambda i, s, pt: (i, 0, 0))],
        out_specs=pl.BlockSpec((1, P * PAGE_SIZE, C), lambda i, s, pt: (i, s, 0)),
    )
    return pl.pallas_call(
        functools.partial(_assemble_kernel, n_in=P, n_page_steps=steps, T=T, R=R),
        grid_spec=grid_spec,
        out_shape=jax.ShapeDtypeStruct((b, (steps + pad_steps) * P * PAGE_SIZE, C), BF16),
        compiler_params=_params("parallel", "arbitrary"),
    )(page_table, *([cache] * P), new)


def _indexer_kernel(qi_ref, wi_ref, ki_ref, sc_ref, thr_ref, key_ref, *, geom, topk):
    tq, tk = geom.tq, geom.tk
    qt, kt = pl.program_id(1), pl.program_id(2)
    qpos0 = geom.past_len + qt * tq
    last = geom.q_tile(qt)

    @pl.when(kt == 0)
    def _():
        sc_ref[0] = jnp.full(sc_ref.shape[1:], NEG, F32)
        key_ref[...] = jnp.full(key_ref.shape, INT32_MIN, jnp.int32)

    @pl.when(kt <= last)
    def _():
        ki = ki_ref[0]
        wi = wi_ref[0]
        acc = jnp.zeros((tq, tk), F32)
        for h in range(H_IDX):
            s = _qk(qi_ref[0, :, h * D_IDX:(h + 1) * D_IDX], ki)
            acc = acc + jnp.maximum(s, 0.0) * wi[:, h:h + 1]
        qpos, kpos = _positions(qpos0, kt * tk, tq, tk)
        score = jnp.where(kpos <= qpos, acc * (D_IDX ** -0.5 * H_IDX ** -0.5), NEG)
        sc_ref[0, kt] = score
        i = lax.bitcast_convert_type(score, jnp.int32)
        key_ref[kt] = i ^ ((i >> 31) & 0x7FFFFFFF)

    @pl.when(kt == geom.nkt - 1)
    def _():
        def body(bit, lo):
            cand = lo + jnp.left_shift(jnp.int32(1), 31 - bit)
            ge = (key_ref[...] >= cand[None]).astype(F32)
            cnt = jnp.sum(jnp.sum(ge, axis=0), axis=-1, keepdims=True)
            return jnp.where(cnt >= topk, cand, lo)

        lo = lax.fori_loop(0, 32, body, jnp.full((tq, 1), INT32_MIN, jnp.int32))
        thr = lax.bitcast_convert_type(lo ^ ((lo >> 31) & 0x7FFFFFFF), F32)
        thr_ref[0] = jnp.broadcast_to(thr, (tq, LANE))


def dsa_indexer(geom, qarr, qi_col, miscarr, misc_col, kiarr, ki_col):
    b, T, tq, tk, nkt = geom.b, geom.T, geom.tq, geom.tk, geom.nkt
    topk = min(TOPK_A, geom.L // 4)
    assert topk <= tk, "the first causal key tile alone must hold topk entries"
    return pl.pallas_call(
        functools.partial(_indexer_kernel, geom=geom, topk=topk),
        grid=(b, geom.nqt, nkt),
        in_specs=[pl.BlockSpec((1, tq, H_IDX * D_IDX), lambda i, q, k: (i, q, qi_col)),
                  pl.BlockSpec((1, tq, LANE), lambda i, q, k: (i, q, misc_col)),
                  pl.BlockSpec((1, tk, D_IDX), lambda i, q, k: (i, jnp.minimum(k, geom.q_tile(q)), ki_col))],
        out_specs=[pl.BlockSpec((1, nkt, tq, tk), lambda i, q, k: (i, 0, q, 0)),
                   pl.BlockSpec((1, tq, LANE), lambda i, q, k: (i, q, 0))],
        out_shape=[jax.ShapeDtypeStruct((b, nkt, T, tk), F32), jax.ShapeDtypeStruct((b, T, LANE), F32)],
        scratch_shapes=[pltpu.VMEM((nkt, tq, tk), jnp.int32)],
        compiler_params=_params("parallel", "parallel", "arbitrary"),
    )(qarr, miscarr, kiarr)


def _dsa_attn_kernel(q_ref, k_ref, v_ref, sc_ref, thr_ref, bias_ref, o_ref, m_ref, l_ref, acc_ref, *, geom, rep):
    tq, tk = geom.tq, geom.tk
    qt, kt = pl.program_id(1), pl.program_id(2)
    qpos0 = geom.past_len + qt * tq
    last = geom.q_tile(qt)

    @pl.when(kt == 0)
    def _():
        _softmax_init(m_ref, l_ref, acc_ref)

    @pl.when(kt <= last)
    def _():
        qpos, kpos = _positions(qpos0, kt * tk, tq, tk)
        mask = (sc_ref[0, 0] >= thr_ref[0][:, :1]) & (kpos <= qpos)

        def step(add_bias):
            for h in range(H_A):
                kv_cols = slice((h // rep) * HEAD_DIM, (h // rep + 1) * HEAD_DIM)
                s = _qk(q_ref[0, :, h * HEAD_DIM:(h + 1) * HEAD_DIM], k_ref[0, :, kv_cols]) * SCALE
                if add_bias:
                    s = s + bias_ref[h, 0]
                _softmax_step(s, mask, v_ref[0, :, kv_cols], m_ref, l_ref, acc_ref, h)

        _near_or_far((last - kt) <= 1, step)

    @pl.when(kt == geom.nkt - 1)
    def _():
        for h in range(H_A):
            o_ref[0, :, h * HEAD_DIM:(h + 1) * HEAD_DIM] = _softmax_out(l_ref, acc_ref, h).astype(o_ref.dtype)


def dsa_attention(geom, qarr, q_col, karr, k_col, v_col, scores, thr, bias):
    b, T, tq, tk, nkt = geom.b, geom.T, geom.tq, geom.tk, geom.nkt

    def kmap(col):
        return lambda i, q, k: (i, jnp.minimum(k, geom.q_tile(q)), col)

    return pl.pallas_call(
        functools.partial(_dsa_attn_kernel, geom=geom, rep=H_A // KV_A),
        grid=(b, geom.nqt, nkt),
        in_specs=[pl.BlockSpec((1, tq, H_A * HEAD_DIM), lambda i, q, k: (i, q, q_col)),
                  pl.BlockSpec((1, tk, KV_A * HEAD_DIM), kmap(k_col)),
                  pl.BlockSpec((1, tk, KV_A * HEAD_DIM), kmap(v_col)),
                  pl.BlockSpec((1, 1, tq, tk), lambda i, q, k: (i, jnp.minimum(k, geom.q_tile(q)), q, 0)),
                  pl.BlockSpec((1, tq, LANE), lambda i, q, k: (i, q, 0)),
                  _bias_spec(geom, H_A)],
        out_specs=pl.BlockSpec((1, tq, H_A * HEAD_DIM), lambda i, q, k: (i, q, 0)),
        out_shape=jax.ShapeDtypeStruct((b, T, H_A * HEAD_DIM), BF16),
        scratch_shapes=[pltpu.VMEM((H_A, tq, 1), F32), pltpu.VMEM((H_A, tq, 1), F32),
                        pltpu.VMEM((H_A, tq, HEAD_DIM), F32)],
        compiler_params=_params("parallel", "parallel", "arbitrary"),
    )(qarr, karr, karr, scores, thr, bias)


def _moba_select_kernel(q_ref, k_ref, avg_ref, sel_ref, *, geom, nblk, ksel):
    qt = pl.program_id(2)
    cur = (geom.past_len + qt * geom.tq) // MOBA_BLOCK
    kmean = jnp.dot(avg_ref[...], k_ref[0], preferred_element_type=F32)
    gs = _qk(q_ref[0], kmean.astype(BF16))
    lane = lax.broadcasted_iota(jnp.int32, gs.shape, 1)
    past = lane < cur
    g = jnp.where(past, gs, NEG)
    cnt = jnp.zeros_like(g)
    for m in range(nblk):
        col = g[:, m:m + 1]
        cnt = cnt + ((col > g) | ((col == g) & (lane > m))).astype(F32)
    sel_ref[0, 0] = (past & (cnt < ksel)).astype(F32)


def moba_select(geom, qarr, q_col, karr, k_col):
    b, T, tq = geom.b, geom.T, geom.tq
    nblk = geom.L // MOBA_BLOCK
    assert 0 < nblk <= LANE and MOBA_BLOCK % tq == 0 and geom.tk % MOBA_BLOCK == 0
    ksel = min(MOBA_TOPK, nblk)
    rows = nblk * MOBA_BLOCK
    avg = np.zeros((LANE, rows), np.float32)
    avg[np.arange(rows) // MOBA_BLOCK, np.arange(rows)] = 1.0 / MOBA_BLOCK
    return pl.pallas_call(
        functools.partial(_moba_select_kernel, geom=geom, nblk=nblk, ksel=ksel),
        grid=(b, H_C, geom.nqt),
        in_specs=[pl.BlockSpec((1, tq, HEAD_DIM), lambda i, h, q: (i, q, q_col + h)),
                  pl.BlockSpec((1, rows, HEAD_DIM), lambda i, h, q: (i, 0, k_col + h)),
                  pl.BlockSpec((LANE, rows), lambda i, h, q: (0, 0))],
        out_specs=pl.BlockSpec((1, 1, tq, LANE), lambda i, h, q: (i, h, q, 0)),
        out_shape=jax.ShapeDtypeStruct((b, H_C, T, LANE), F32),
        compiler_params=_params("parallel", "parallel", "arbitrary"),
    )(qarr, karr, jnp.asarray(avg, BF16))


def _moba_attn_kernel(q_ref, k_ref, v_ref, sel_ref, bias_ref, o_ref, m_ref, l_ref, acc_ref, *, geom):
    tq, tk = geom.tq, geom.tk
    qt, kt = pl.program_id(1), pl.program_id(2)
    qpos0 = geom.past_len + qt * tq
    cur = geom.q_tile(qt)

    @pl.when(kt == 0)
    def _():
        _softmax_init(m_ref, l_ref, acc_ref)

    @pl.when(kt <= cur)
    def _():
        qpos, kpos = _positions(qpos0, kt * tk, tq, tk)
        own_blk = qpos0 // MOBA_BLOCK
        blk = lax.broadcasted_iota(jnp.int32, (LANE, tk), 0)
        tok = lax.broadcasted_iota(jnp.int32, (LANE, tk), 1)
        expand = (blk == kt * (tk // MOBA_BLOCK) + tok // MOBA_BLOCK).astype(BF16)
        own = (kpos // MOBA_BLOCK == own_blk) & (kpos <= qpos)

        def step(add_bias):
            for h in range(H_C):
                cols = slice(h * HEAD_DIM, (h + 1) * HEAD_DIM)
                chosen = jnp.dot(sel_ref[0, h].astype(BF16), expand, preferred_element_type=F32) > 0.5
                s = _qk(q_ref[0, :, cols], k_ref[0, :, cols]) * SCALE
                if add_bias:
                    s = s + bias_ref[h, 0]
                _softmax_step(s, chosen | own, v_ref[0, :, cols], m_ref, l_ref, acc_ref, h)

        _near_or_far((cur - kt) <= 1, step)

    @pl.when(kt == geom.nkt - 1)
    def _():
        for h in range(H_C):
            o_ref[0, :, h * HEAD_DIM:(h + 1) * HEAD_DIM] = _softmax_out(l_ref, acc_ref, h).astype(o_ref.dtype)


def moba_attention(geom, qarr, q_col, karr, k_col, v_col, sel, bias):
    b, T, tq, tk, nkt = geom.b, geom.T, geom.tq, geom.tk, geom.nkt
    width = H_C * HEAD_DIM

    def kmap(col):
        return lambda i, q, k: (i, jnp.minimum(k, geom.q_tile(q)), col)

    return pl.pallas_call(
        functools.partial(_moba_attn_kernel, geom=geom),
        grid=(b, geom.nqt, nkt),
        in_specs=[pl.BlockSpec((1, tq, width), lambda i, q, k: (i, q, q_col)),
                  pl.BlockSpec((1, tk, width), kmap(k_col)),
                  pl.BlockSpec((1, tk, width), kmap(v_col)),
                  pl.BlockSpec((1, H_C, tq, LANE), lambda i, q, k: (i, 0, q, 0)),
                  _bias_spec(geom, H_C)],
        out_specs=pl.BlockSpec((1, tq, width), lambda i, q, k: (i, q, 0)),
        out_shape=jax.ShapeDtypeStruct((b, T, width), BF16),
        scratch_shapes=[pltpu.VMEM((H_C, tq, 1), F32), pltpu.VMEM((H_C, tq, 1), F32),
                        pltpu.VMEM((H_C, tq, HEAD_DIM), F32)],
        compiler_params=_params("parallel", "parallel", "arbitrary"),
    )(qarr, karr, karr, sel, bias)


CMP_ROW = CMP_STRIDE * 2 * KV_B * HEAD_DIM


def _compress_kernel(x_ref, xn_ref, wlo_ref, whi_ref, c_ref, w2_ref, g_ref, o_ref, *, tr):
    x = x_ref[0]
    lo = jnp.dot(x, wlo_ref[...], preferred_element_type=F32)
    hi = jnp.dot(x, whi_ref[...], preferred_element_type=F32)
    hi_next = jnp.dot(xn_ref[0], whi_ref[...], preferred_element_type=F32)
    row = lax.broadcasted_iota(jnp.int32, hi.shape, 0)
    hi = jnp.where(row == tr - 1, hi_next[0:1], pltpu.roll(hi, tr - 1, 0))
    act = jax.nn.gelu(lo + hi + c_ref[...]).astype(BF16)
    for c in range(2 * KV_B):
        cols = slice(c * HEAD_DIM, (c + 1) * HEAD_DIM)
        y = jnp.dot(act[:, cols], w2_ref[c // KV_B], preferred_element_type=F32)
        if c < KV_B:
            y = y * lax.rsqrt(jnp.mean(y * y, axis=-1, keepdims=True) + EPS) * g_ref[...]
        o_ref[0, :, cols] = y.astype(o_ref.dtype)


def nsa_compress(x, wlo, whi, const, w2, gain):
    b, R, _ = x.shape
    tr = _tile(R, 128, BF16_SUBLANE)
    nxt = tr // BF16_SUBLANE
    n_sub16 = R // BF16_SUBLANE
    C = 2 * KV_B * HEAD_DIM
    return pl.pallas_call(
        functools.partial(_compress_kernel, tr=tr),
        grid=(b, R // tr),
        in_specs=[pl.BlockSpec((1, tr, CMP_ROW), lambda i, r: (i, r, 0)),
                  pl.BlockSpec((1, BF16_SUBLANE, CMP_ROW), lambda i, r: (i, jnp.minimum((r + 1) * nxt, n_sub16 - 1), 0)),
                  pl.BlockSpec((CMP_ROW, C), lambda i, r: (0, 0)),
                  pl.BlockSpec((CMP_ROW, C), lambda i, r: (0, 0)),
                  pl.BlockSpec((1, C), lambda i, r: (0, 0)),
                  pl.BlockSpec((2, HEAD_DIM, HEAD_DIM), lambda i, r: (0, 0, 0)),
                  pl.BlockSpec((1, HEAD_DIM), lambda i, r: (0, 0))],
        out_specs=pl.BlockSpec((1, tr, C), lambda i, r: (i, r, 0)),
        out_shape=jax.ShapeDtypeStruct((b, R, C), BF16),
        compiler_params=_params("parallel", "arbitrary"),
    )(x, x, wlo, whi, const, w2, gain)


def _prep_compress(cmp_pos, cmp_w1, cmp_w2, kn_cmp):
    r = CMP_LEN // CMP_STRIDE
    assert r == 2
    w1 = cmp_w1.reshape(2, r, CMP_STRIDE, HEAD_DIM, HEAD_DIM)
    kv_of_col = np.repeat(np.arange(2), KV_B)
    eye = jnp.eye(2 * KV_B, dtype=cmp_w1.dtype)

    def expand(half):
        w = w1[kv_of_col, half]
        return jnp.einsum('crde,cf->rcdfe', w, eye).reshape(CMP_ROW, 2 * KV_B * HEAD_DIM).astype(BF16)

    const = jnp.einsum('krd,krde->ke', cmp_pos.reshape(2, CMP_LEN, HEAD_DIM),
                       cmp_w1.reshape(2, CMP_LEN, HEAD_DIM, HEAD_DIM), precision=lax.Precision.HIGHEST)
    const = const[kv_of_col].reshape(1, -1).astype(F32)
    return dict(wlo=expand(0), whi=expand(1), const=const, w2=cmp_w2.astype(BF16), gain=kn_cmp.reshape(1, -1).astype(F32))


def _nsa_cmp_kernel(q_ref, kc_ref, vc_ref, bias_ref, ov_ref, oc_ref, sel_ref, *, geom, rep, n_slc, nsel):
    tq = geom.tq
    qt = pl.program_id(2)
    qpos0 = geom.past_len + qt * tq
    kc, vc, ov = kc_ref[0], vc_ref[0], ov_ref[...]
    ncp, nsp = ov.shape
    qpos, n_idx = _positions(qpos0, 0, tq, ncp)
    valid = (n_idx * CMP_STRIDE + (CMP_LEN - 1)) <= qpos
    imp = jnp.zeros((tq, nsp), F32)
    for r in range(rep):
        cols = slice(r * HEAD_DIM, (r + 1) * HEAD_DIM)
        s = jnp.where(valid, _qk(q_ref[0, :, cols], kc) * SCALE + bias_ref[r], NEG)
        e = jnp.where(valid, jnp.exp(s - jnp.max(s, axis=-1, keepdims=True)), 0.0)
        p = (e / jnp.maximum(jnp.sum(e, axis=-1, keepdims=True), 1e-30)).astype(BF16)
        oc_ref[0, :, cols] = jnp.dot(p, vc, preferred_element_type=F32)
        imp = imp + jnp.dot(p, ov, preferred_element_type=F32)
    spos, jj = _positions(qpos0, 0, tq, nsp)
    cur = spos // SLC_LEN
    forced = (jj == 0) | (jj == cur) | (jj == cur - 1)
    imp = jnp.where(jj > cur, NEG, jnp.where(forced, -NEG, imp))
    cnt = jnp.zeros_like(imp)
    for s_blk in range(n_slc):
        col = imp[:, s_blk:s_blk + 1]
        cnt = cnt + ((col > imp) | ((col == imp) & (jj > s_blk))).astype(F32)
    sel_ref[0, 0] = ((cnt < nsel) & (jj < n_slc)).astype(F32)


def nsa_compressed(geom, qarr, q_col, cmpkv, bias_cmp):
    b, T, tq = geom.b, geom.T, geom.tq
    rep = H_B // KV_B
    ncp = cmpkv.shape[1]
    n_slc = -(-geom.L // SLC_LEN)
    nsel = min(N_SLC, n_slc)
    nsp = _round_up(n_slc, LANE)
    n_cmp = geom.L // CMP_STRIDE - CMP_LEN // CMP_STRIDE + 1
    cstart = np.arange(ncp) * CMP_STRIDE
    cend = cstart + CMP_LEN - 1
    sstart = np.arange(nsp) * SLC_LEN
    ov = ((cstart[:, None] < sstart[None, :] + SLC_LEN) & (cend[:, None] >= sstart[None, :])
          & (np.arange(ncp)[:, None] < n_cmp) & (np.arange(nsp)[None, :] < n_slc)).astype(np.float32)
    return pl.pallas_call(
        functools.partial(_nsa_cmp_kernel, geom=geom, rep=rep, n_slc=n_slc, nsel=nsel),
        grid=(b, KV_B, geom.nqt),
        in_specs=[pl.BlockSpec((1, tq, rep * HEAD_DIM), lambda i, g, q: (i, q, q_col + g)),
                  pl.BlockSpec((1, ncp, HEAD_DIM), lambda i, g, q: (i, 0, g)),
                  pl.BlockSpec((1, ncp, HEAD_DIM), lambda i, g, q: (i, 0, KV_B + g)),
                  pl.BlockSpec((rep, tq, ncp), lambda i, g, q: (g, q, 0)),
                  pl.BlockSpec((ncp, nsp), lambda i, g, q: (0, 0))],
        out_specs=[pl.BlockSpec((1, tq, rep * HEAD_DIM), lambda i, g, q: (i, q, g)),
                   pl.BlockSpec((1, 1, tq, nsp), lambda i, g, q: (i, g, q, 0))],
        out_shape=[jax.ShapeDtypeStruct((b, T, H_B * HEAD_DIM), F32), jax.ShapeDtypeStruct((b, KV_B, T, nsp), F32)],
        compiler_params=_params("parallel", "parallel", "arbitrary"),
    )(qarr, cmpkv, cmpkv, bias_cmp, jnp.asarray(ov, BF16))


class _Window:
    def __init__(self, arr, k_col, v_col, tw, n_back, tile0, pos0, bias, bias_always):
        self.arr, self.k_col, self.v_col, self.tw, self.n_back = arr, k_col, v_col, tw, n_back
        self.tile0, self.pos0, self.bias, self.bias_always = tile0, pos0, bias, bias_always


def _nsa_attn_kernel(q_ref, ks_ref, vs_ref, kw_ref, vw_ref, sel_ref, bias_ref, wbias_ref, oc_ref, misc_ref, o_ref,
                     m_ref, l_ref, acc_ref, *, geom, rep, win):
    tq, tk = geom.tq, geom.tk
    qt, kt = pl.program_id(1), pl.program_id(2)
    qpos0 = geom.past_len + qt * tq
    last = geom.q_tile(qt)

    @pl.when(kt == 0)
    def _():
        _softmax_init(m_ref, l_ref, acc_ref)

    def branch(g, k_ref, v_ref, mask, slot0, b_ref, add_bias):
        kv_cols = slice(g * HEAD_DIM, (g + 1) * HEAD_DIM)
        for h in range(g * rep, (g + 1) * rep):
            s = _qk(q_ref[0, :, h * HEAD_DIM:(h + 1) * HEAD_DIM], k_ref[0, :, kv_cols]) * SCALE
            if add_bias:
                s = s + b_ref[h, 0]
            _softmax_step(s, mask, v_ref[0, :, kv_cols], m_ref, l_ref, acc_ref, slot0 + h)

    near = (last - kt) <= 1

    @pl.when(kt <= last)
    def _():
        qpos, kpos = _positions(qpos0, kt * tk, tq, tk)
        nsp = sel_ref.shape[-1]
        blk = lax.broadcasted_iota(jnp.int32, (nsp, tk), 0)
        tok = lax.broadcasted_iota(jnp.int32, (nsp, tk), 1)
        expand = (blk == kt * (tk // SLC_LEN) + tok // SLC_LEN).astype(BF16)

        def step(add_bias):
            for g in range(KV_B):
                chosen = jnp.dot(sel_ref[0, g].astype(BF16), expand, preferred_element_type=F32) > 0.5
                branch(g, ks_ref, vs_ref, chosen & (kpos <= qpos), 0, bias_ref, add_bias)

        _near_or_far(near, step)

    @pl.when((kt <= last) & (kt >= last - win.n_back))
    def _():
        qpos, kpos = _positions(qpos0, win.pos0 + (kt - win.tile0) * win.tw, tq, win.tw)
        dist = qpos - kpos

        def step(add_bias):
            for g in range(KV_B):
                branch(g, kw_ref, vw_ref, (dist >= 0) & (dist < WINDOW), H_B, wbias_ref, add_bias)

        if win.bias_always:
            step(True)
        else:
            _near_or_far(near, step)

    @pl.when(kt == geom.nkt - 1)
    def _():
        gates = jax.nn.sigmoid(misc_ref[0])
        for h in range(H_B):
            cols = slice(h * HEAD_DIM, (h + 1) * HEAD_DIM)
            base = H_IDX + h * N_BRANCH
            o = (gates[:, base:base + 1] * oc_ref[0, :, cols]
                 + gates[:, base + 1:base + 2] * _softmax_out(l_ref, acc_ref, h)
                 + gates[:, base + 2:base + 3] * _softmax_out(l_ref, acc_ref, H_B + h))
            o_ref[0, :, cols] = o.astype(o_ref.dtype)


def nsa_attention(geom, qarr, q_col, sarr, ks_col, vs_col, win, sel, bias, oc, miscarr, misc_col):
    b, T, tq, tk, nkt = geom.b, geom.T, geom.tq, geom.tk, geom.nkt
    nsp = sel.shape[-1]
    tw = win.tw
    n_wt = win.arr.shape[1] // tw
    kv_width = KV_B * HEAD_DIM

    def smap(col):
        return lambda i, q, k: (i, jnp.minimum(k, geom.q_tile(q)), col)

    def wtile(q, k):
        last = geom.q_tile(q)
        return jnp.clip(jnp.clip(k, last - win.n_back, last) - win.tile0, 0, n_wt - 1)

    def wmap(col):
        return lambda i, q, k: (i, wtile(q, k), col)

    if win.bias_always:
        assert win.bias.shape[1] == 1 and n_wt == 1
        wbias_spec = pl.BlockSpec((H_B, 1, tq, tw), lambda i, q, k: (0, 0, 0, 0))
    else:
        wbias_spec = _bias_spec(geom, H_B)

    return pl.pallas_call(
        functools.partial(_nsa_attn_kernel, geom=geom, rep=H_B // KV_B, win=win),
        grid=(b, geom.nqt, nkt),
        in_specs=[pl.BlockSpec((1, tq, H_B * HEAD_DIM), lambda i, q, k: (i, q, q_col)),
                  pl.BlockSpec((1, tk, kv_width), smap(ks_col)),
                  pl.BlockSpec((1, tk, kv_width), smap(vs_col)),
                  pl.BlockSpec((1, tw, kv_width), wmap(win.k_col)),
                  pl.BlockSpec((1, tw, kv_width), wmap(win.v_col)),
                  pl.BlockSpec((1, KV_B, tq, nsp), lambda i, q, k: (i, 0, q, 0)),
                  _bias_spec(geom, H_B),
                  wbias_spec,
                  pl.BlockSpec((1, tq, H_B * HEAD_DIM), lambda i, q, k: (i, q, 0)),
                  pl.BlockSpec((1, tq, LANE), lambda i, q, k: (i, q, misc_col))],
        out_specs=pl.BlockSpec((1, tq, H_B * HEAD_DIM), lambda i, q, k: (i, q, 0)),
        out_shape=jax.ShapeDtypeStruct((b, T, H_B * HEAD_DIM), BF16),
        scratch_shapes=[pltpu.VMEM((2 * H_B, tq, 1), F32), pltpu.VMEM((2 * H_B, tq, 1), F32),
                        pltpu.VMEM((2 * H_B, tq, HEAD_DIM), F32)],
        compiler_params=_params("parallel", "parallel", "arbitrary"),
    )(qarr, sarr, sarr, win.arr, win.arr, sel, bias, win.bias, oc, miscarr)


_SEG = {}
_off = 0
for _name, _w in (("qa", H_A * HEAD_DIM), ("qb", H_B * HEAD_DIM), ("qi", H_IDX * D_IDX), ("qc", H_C * HEAD_DIM),
                  ("kvc", 2 * H_C * HEAD_DIM), ("kva", 2 * KV_A * HEAD_DIM), ("cmp", 2 * KV_B * HEAD_DIM),
                  ("slc", 2 * KV_B * HEAD_DIM), ("win", 2 * KV_B * HEAD_DIM), ("ki", D_IDX), ("misc", LANE)):
    _SEG[_name] = (_off, _off + _w)
    _off += _w
IN_COLS_ALIGNED = _off


def _col(name, width=LANE):
    assert _SEG[name][0] % width == 0
    return _SEG[name][0] // width


def _prep_in_proj(w_in, qn_a, kn_a, qn_b, kn_b, qn_c, kn_c):
    sizes = (H_A * HEAD_DIM, KV_A * HEAD_DIM, KV_A * HEAD_DIM, H_IDX * D_IDX, H_IDX, D_IDX,
             H_B * HEAD_DIM, N_BRANCH * H_B) + (KV_B * HEAD_DIM,) * 6 + (H_C * HEAD_DIM,) * 3
    splits = np.cumsum(sizes)[:-1].tolist()
    (qa, ka, va, qi, wi, ki, q_b, g_b, kbc, vbc, kbs, vbs, kbw, vbw, qc, kc, vc) = jnp.split(w_in, splits, axis=-1)
    D = w_in.shape[0]
    misc = jnp.concatenate([wi, g_b, jnp.zeros((D, LANE - H_IDX - N_BRANCH * H_B), w_in.dtype)], axis=-1)
    n_pad = _round_up(IN_COLS_ALIGNED, 512) - IN_COLS_ALIGNED
    w = jnp.concatenate([qa, q_b, qi, qc, kc, vc, ka, va, kbc, vbc, kbs, vbs, kbw, vbw, ki, misc,
                         jnp.zeros((D, n_pad), w_in.dtype)], axis=-1).astype(BF16)
    one = jnp.ones((HEAD_DIM,), F32)

    def rep(g, n):
        return jnp.tile(g.astype(F32), n)

    gains = jnp.concatenate([
        rep(qn_a, H_A), rep(qn_b, H_B), rep(one, H_IDX), rep(qn_c, H_C), rep(kn_c, H_C), rep(one, H_C),
        rep(kn_a, KV_A), rep(one, KV_A), rep(one, 2 * KV_B), rep(kn_b[1], KV_B), rep(one, KV_B),
        rep(kn_b[2], KV_B), rep(one, KV_B), one, one, jnp.ones((n_pad,), F32)]).reshape(1, -1)
    flags = np.concatenate([
        np.ones(H_A), np.ones(H_B), np.zeros(H_IDX), np.ones(H_C), np.ones(H_C), np.zeros(H_C),
        np.ones(KV_A), np.zeros(KV_A), np.zeros(2 * KV_B), np.ones(KV_B), np.zeros(KV_B),
        np.ones(KV_B), np.zeros(KV_B), np.zeros(1), np.zeros(1), np.zeros(n_pad // LANE)]).astype(np.int32)
    return w, gains, jnp.asarray(flags)


def _prep_layer(i, ffn1_norm, ffn1_wg, ffn1_wu, ffn1_wd, mix_norm, w_in, qn_a, kn_a, qn_b, kn_b, cmp_pos, cmp_w1,
                cmp_w2, qn_c, kn_c, wb_a, wb_b, wb_c, w_gate, b_gate, w_out, ffn2_norm, ffn2_wg, ffn2_wu, ffn2_wd,
                ple_norm, ple_wg, ple_wp):
    w_in_p, gains, flags = _prep_in_proj(w_in[i], qn_a[i], kn_a[i], qn_b[i], kn_b[i], qn_c[i], kn_c[i])
    return dict(
        layer=i, f1n=ffn1_norm[i], f1g=ffn1_wg, f1u=ffn1_wu, f1d=ffn1_wd, mn=mix_norm[i],
        w_in=w_in_p, gains=gains, flags=flags,
        cmp=_prep_compress(cmp_pos[i], cmp_w1[i], cmp_w2[i], kn_b[i, 0]),
        wb_a=wb_a[i].astype(BF16), wb_b=wb_b[i].astype(BF16), wb_c=wb_c[i].astype(BF16),
        w_gate=w_gate[i].astype(BF16), b_gate=b_gate[i].reshape(1, -1), w_out=w_out[i].astype(BF16),
        f2n=ffn2_norm[i], f2g=ffn2_wg, f2u=ffn2_wu, f2d=ffn2_wd,
        pn=ple_norm[i], pwg=ple_wg[i].astype(BF16), pwp=ple_wp[i].astype(BF16))


def _layer(x, p, geom, layer, caches, W, bias):
    b, T = geom.b, geom.T
    M = b * T
    h = ffn_down(ffn_up(rmsnorm_bf16(x, W["f1n"]), W["f1g"], W["f1u"], W["layer"]), W["f1d"], x, W["layer"])
    u = rmsnorm_bf16(h, W["mn"])
    proj32, proj16 = in_proj(u, W["w_in"], W["gains"], W["flags"])
    Np = proj32.shape[1]
    p32 = proj32.reshape(b, T, Np)
    p16 = proj16.reshape(b, T, Np)

    def seg32(name):
        lo, hi = _SEG[name]
        return p32[:, :, lo:hi]

    def seg16(name):
        lo, hi = _SEG[name]
        return p16[:, :, lo:hi]

    win_new = seg32("win")
    if caches is None:
        assert T >= WINDOW
        a_kv, a_ki, slc, c_kv = (p16,) * 4
        a_k, ki_c = _col("kva", KV_A * HEAD_DIM), _col("ki")
        s_k = _col("slc", KV_B * HEAD_DIM)
        c_k = _col("kvc", H_C * HEAD_DIM)
        w_k = _col("win", KV_B * HEAD_DIM)
        win = _Window(p16, w_k, w_k + 1, geom.tk, WINDOW // geom.tk, 0, 0, bias["b"], False)
        cmp_rows = seg16("cmp")
        win_state = win_new[:, T - WINDOW:]
    else:
        cache_a_kv, cache_a_kidx, cache_b_cmp, cache_b_slc, win_past, cache_c_kv, page_table = caches
        a_kv = assemble_cache(cache_a_kv, layer, page_table, seg16("kva"), geom.tk)
        a_ki = assemble_cache(cache_a_kidx, layer, page_table, seg16("ki"), geom.tk)
        cmp_rows = assemble_cache(cache_b_cmp, layer, page_table, seg16("cmp"), geom.tk)
        slc = assemble_cache(cache_b_slc, layer, page_table, seg16("slc"), geom.tk)
        c_kv = assemble_cache(cache_c_kv, layer, page_table, seg16("kvc"), geom.tk)
        a_k, ki_c, s_k, c_k = 0, 0, 0, 0
        n_buf = win_past.shape[2]
        win_past = win_past[layer].reshape(b, n_buf, -1)
        warr = jnp.concatenate([win_past.astype(BF16), seg16("win"),
                                jnp.zeros((b, KEY_TILE - T, win_past.shape[-1]), BF16)], axis=1)
        win = _Window(warr, 0, 1, warr.shape[1], 0, geom.q_tile(0), geom.past_len - n_buf, bias["b_win"], True)
        win_state = jnp.concatenate([win_past, win_new], axis=1)[:, -WINDOW:]

    scores, thr = dsa_indexer(geom, p16, _col("qi", H_IDX * D_IDX), p32, _col("misc"), a_ki, ki_c)
    o_a = dsa_attention(geom, p16, _col("qa", H_A * HEAD_DIM), a_kv, a_k, a_k + 1, scores, thr, bias["a"])
    cmpkv = nsa_compress(cmp_rows.reshape(b, -1, CMP_ROW), **W["cmp"])
    qpos = geom.past_len + jnp.arange(T)
    cend = jnp.arange(cmpkv.shape[1]) * CMP_STRIDE + CMP_LEN - 1
    bias_cmp = _bias_lookup(bias["tab_b"], qpos[:, None] - cend[None, :])
    oc, sel_b = nsa_compressed(geom, p16, _col("qb", 4 * HEAD_DIM), cmpkv, bias_cmp)
    o_b = nsa_attention(geom, p16, _col("qb", H_B * HEAD_DIM), slc, s_k, s_k + 1, win, sel_b, bias["b"], oc, p32,
                        _col("misc"))
    sel_c = moba_select(geom, p16, _col("qc"), c_kv, c_k * H_C)
    o_c = moba_attention(geom, p16, _col("qc", H_C * HEAD_DIM), c_kv, c_k, c_k + 1, sel_c, bias["c"])

    merged = gate_merge(u, o_a.reshape(M, -1), o_b.reshape(M, -1), o_c.reshape(M, -1),
                        W["w_gate"], W["b_gate"], W["wb_a"], W["wb_b"], W["wb_c"])
    h = out_proj(merged, W["w_out"], h)
    h = ffn_down(ffn_up(rmsnorm_bf16(h, W["f2n"]), W["f2g"], W["f2u"], W["layer"]), W["f2d"], h, W["layer"])
    h = ple(rmsnorm_bf16(h, W["pn"]), W["pwg"], p.astype(BF16), W["pwp"], h)
    state = (seg32("kva").reshape(b, T, 2, KV_A, HEAD_DIM), seg32("ki"), seg32("cmp").reshape(b, T, 2, KV_B, HEAD_DIM),
             seg32("slc").reshape(b, T, 2, KV_B, HEAD_DIM), win_state.reshape(b, WINDOW, 2, KV_B, HEAD_DIM),
             seg32("kvc").reshape(b, T, 2, H_C, HEAD_DIM))
    return h, state


def kernel(x_prompt, x_sample, cache_a_kv, cache_a_kidx, cache_b_cmp_kv, cache_b_slc_kv, state_b_win_kv, cache_c_kv, page_table, p_prompt, p_sample, rel_bias, ffn1_norm, ffn1_wg, ffn1_wu, ffn1_wd, mix_norm, w_in, qn_a, kn_a, qn_b, kn_b, cmp_pos, cmp_w1, cmp_w2, qn_c, kn_c, wb_a, wb_b, wb_c, w_gate, b_gate, w_out, ffn2_norm, ffn2_wg, ffn2_wu, ffn2_wd, ple_norm, ple_wg, ple_wp):
    depth = ffn1_norm.shape[0]
    bp, Tp, D = x_prompt.shape
    bs, Ts, _ = x_sample.shape
    geom_p = _Geom(bp, Tp, 0)
    geom_s = _Geom(bs, Ts, page_table.shape[1] * PAGE_SIZE)
    n_buf = state_b_win_kv.shape[2]
    assert n_buf >= WINDOW and Ts <= KEY_TILE
    tab_a, tab_b, tab_c = rel_bias[:, :H_A], rel_bias[:, H_A:H_A + H_B], rel_bias[:, H_A + H_B:]

    def tile_bias(geom):
        near = (0, geom.tk)
        return dict(a=_toeplitz_bias(tab_a, geom.tq, geom.tk, near), b=_toeplitz_bias(tab_b, geom.tq, geom.tk, near),
                    c=_toeplitz_bias(tab_c, geom.tq, geom.tk, near), tab_b=tab_b)

    bias_p = tile_bias(geom_p)
    bias_s = tile_bias(geom_s)
    bias_s["b_win"] = _toeplitz_bias(tab_b, geom_s.tq, n_buf + KEY_TILE, (n_buf,))
    caches = (cache_a_kv, cache_a_kidx, cache_b_cmp_kv, cache_b_slc_kv, state_b_win_kv, cache_c_kv, page_table)

    y_p = x_prompt.reshape(bp * Tp, D)
    y_s = x_sample.reshape(bs * Ts, D)
    sp_list, ss_list = [], []
    for i in range(depth):
        W = _prep_layer(i, ffn1_norm, ffn1_wg, ffn1_wu, ffn1_wd, mix_norm, w_in, qn_a, kn_a, qn_b, kn_b, cmp_pos,
                        cmp_w1, cmp_w2, qn_c, kn_c, wb_a, wb_b, wb_c, w_gate, b_gate, w_out, ffn2_norm, ffn2_wg,
                        ffn2_wu, ffn2_wd, ple_norm, ple_wg, ple_wp)
        y_p, sp = _layer(y_p, p_prompt[i].reshape(bp * Tp, -1), geom_p, i, None, W, bias_p)
        y_s, ss = _layer(y_s, p_sample[i].reshape(bs * Ts, -1), geom_s, i, caches, W, bias_s)
        sp_list.append(sp)
        ss_list.append(ss)

    outs = [y_p.reshape(bp, Tp, D), y_s.reshape(bs, Ts, D)]
    for j in range(6):
        outs.append(jnp.stack([s[j] for s in sp_list]))
        outs.append(jnp.stack([s[j] for s in ss_list]))
    return tuple(outs)
```

```python
import functools
import math

import numpy as np
import jax
import jax.numpy as jnp
from jax import lax
from jax.experimental import pallas as pl
from jax.experimental.pallas import tpu as pltpu

PAGE_SIZE = 128
HEAD_DIM = 128
H_A = 8
KV_A = 2
H_IDX = 16
D_IDX = 128
TOPK_A = 256
H_B = 8
KV_B = 2
CMP_LEN = 32
CMP_STRIDE = 16
SLC_LEN = 64
N_SLC = 16
WINDOW = 512
H_C = 8
MOBA_BLOCK = 256
MOBA_TOPK = 3
N_BUCKETS = 32
MAX_DISTANCE = 128
N_BRANCH = 3
EPS = 1e-6
NEG = -1e30

LANE = 128
BF16_SUBLANE = 16
V7X_VMEM_LIMIT = 56 * 1024 * 1024

F32 = jnp.float32
BF16 = jnp.bfloat16


def _tile(n, target, align):
    best = None
    for t in range(align, min(n, target) + 1, align):
        if n % t == 0:
            best = t
    return best if best is not None else n


def _round_up(n, m):
    return -(-n // m) * m


def _params(*sem):
    return pltpu.CompilerParams(dimension_semantics=sem, vmem_limit_bytes=V7X_VMEM_LIMIT)


def _rmsnorm_kernel(x_ref, g_ref, o_ref):
    x = x_ref[...]
    y = x * lax.rsqrt(jnp.mean(x * x, axis=-1, keepdims=True) + EPS)
    o_ref[...] = (y * g_ref[...]).astype(o_ref.dtype)


def rmsnorm_bf16(x, g):
    M, D = x.shape
    tm = _tile(M, 256, BF16_SUBLANE)
    return pl.pallas_call(
        _rmsnorm_kernel,
        grid=(M // tm,),
        in_specs=[pl.BlockSpec((tm, D), lambda i: (i, 0)), pl.BlockSpec((1, D), lambda i: (0, 0))],
        out_specs=pl.BlockSpec((tm, D), lambda i: (i, 0)),
        out_shape=jax.ShapeDtypeStruct((M, D), BF16),
        compiler_params=_params("parallel"),
    )(x, g.reshape(1, D))


def _ffn_up_kernel(a_ref, wg_ref, wu_ref, o_ref):
    a = a_ref[...]
    g = jnp.dot(a, wg_ref[...].astype(BF16), preferred_element_type=F32)
    u = jnp.dot(a, wu_ref[...].astype(BF16), preferred_element_type=F32)
    o_ref[...] = (g * jax.nn.sigmoid(g) * u).astype(o_ref.dtype)


def ffn_up(a, wg, wu, layer):
    M, D = a.shape
    F = wg.shape[2]
    tm = _tile(M, 1024, BF16_SUBLANE)
    tn = _tile(F, 256, LANE)
    return pl.pallas_call(
        _ffn_up_kernel,
        grid=(M // tm, F // tn),
        in_specs=[pl.BlockSpec((tm, D), lambda i, j: (i, 0)),
                  pl.BlockSpec((None, D, tn), lambda i, j: (layer, 0, j)),
                  pl.BlockSpec((None, D, tn), lambda i, j: (layer, 0, j))],
        out_specs=pl.BlockSpec((tm, tn), lambda i, j: (i, j)),
        out_shape=jax.ShapeDtypeStruct((M, F), BF16),
        compiler_params=_params("parallel", "arbitrary"),
    )(a, wg, wu)


def _ffn_down_kernel(a_ref, w_ref, x_ref, o_ref, acc_ref):
    k = pl.program_id(2)

    @pl.when(k == 0)
    def _():
        acc_ref[...] = jnp.zeros_like(acc_ref)

    acc_ref[...] += jnp.dot(a_ref[...], w_ref[...], preferred_element_type=F32)

    @pl.when(k == pl.num_programs(2) - 1)
    def _():
        o_ref[...] = x_ref[...] + 0.5 * acc_ref[...]


def ffn_down(a, w, x):
    M, F = a.shape
    D = w.shape[1]
    tm = _tile(M, 1024, BF16_SUBLANE)
    tn = _tile(D, 512, LANE)
    tk = _tile(F, 5504, LANE)
    return pl.pallas_call(
        _ffn_down_kernel,
        grid=(M // tm, D // tn, F // tk),
        in_specs=[pl.BlockSpec((tm, tk), lambda i, j, k: (i, k)),
                  pl.BlockSpec((tk, tn), lambda i, j, k: (k, j)),
                  pl.BlockSpec((tm, tn), lambda i, j, k: (i, j))],
        out_specs=pl.BlockSpec((tm, tn), lambda i, j, k: (i, j)),
        out_shape=jax.ShapeDtypeStruct((M, D), F32),
        scratch_shapes=[pltpu.VMEM((tm, tn), F32)],
        compiler_params=_params("parallel", "arbitrary", "arbitrary"),
    )(a, w, x)


def _w_in_kernel(flags_ref, a_ref, w_ref, g_ref, o_ref, o16_ref, *, heads_per_tile):
    j = pl.program_id(1)
    acc = jnp.dot(a_ref[...], w_ref[...], preferred_element_type=F32)
    for c in range(heads_per_tile):
        cols = slice(c * HEAD_DIM, (c + 1) * HEAD_DIM)
        y = acc[:, cols]
        yn = y * lax.rsqrt(jnp.mean(y * y, axis=-1, keepdims=True) + EPS) * g_ref[:, cols]
        y = jnp.where(flags_ref[j * heads_per_tile + c] > 0, yn, y)
        o_ref[:, cols] = y
        o16_ref[:, cols] = y.astype(o16_ref.dtype)


def in_proj(a, w, gains, flags):
    M, D = a.shape
    N = w.shape[1]
    tm = _tile(M, 1024, BF16_SUBLANE)
    tn = _tile(N, 512, LANE)
    grid_spec = pltpu.PrefetchScalarGridSpec(
        num_scalar_prefetch=1,
        grid=(M // tm, N // tn),
        in_specs=[pl.BlockSpec((tm, D), lambda i, j, f: (i, 0)),
                  pl.BlockSpec((D, tn), lambda i, j, f: (0, j)),
                  pl.BlockSpec((1, tn), lambda i, j, f: (0, j))],
        out_specs=[pl.BlockSpec((tm, tn), lambda i, j, f: (i, j)), pl.BlockSpec((tm, tn), lambda i, j, f: (i, j))],
    )
    return pl.pallas_call(
        functools.partial(_w_in_kernel, heads_per_tile=tn // HEAD_DIM),
        grid_spec=grid_spec,
        out_shape=[jax.ShapeDtypeStruct((M, N), F32), jax.ShapeDtypeStruct((M, N), BF16)],
        compiler_params=_params("parallel", "arbitrary"),
    )(flags, a, w, gains)


def _gate_merge_kernel(u_ref, oa_ref, ob_ref, oc_ref, wg0_ref, wg1_ref, wg2_ref, bg0_ref, bg1_ref, bg2_ref,
                       wba_ref, wbb_ref, wbc_ref, o_ref):
    u = u_ref[...]
    out = None
    for o_r, wg_r, bg_r, wb_r in ((oa_ref, wg0_ref, bg0_ref, wba_ref),
                                  (ob_ref, wg1_ref, bg1_ref, wbb_ref),
                                  (oc_ref, wg2_ref, bg2_ref, wbc_ref)):
        gate = jax.nn.sigmoid(jnp.dot(u, wg_r[...], preferred_element_type=F32) + bg_r[...])
        branch = jnp.dot(o_r[...], wb_r[...], preferred_element_type=F32)
        out = gate * branch if out is None else out + gate * branch
    o_ref[...] = out.astype(o_ref.dtype)


def gate_merge(u, o_a, o_b, o_c, w_gate, b_gate, wb_a, wb_b, wb_c):
    M, D = u.shape
    Ho = o_a.shape[1]
    tm = _tile(M, 1024, BF16_SUBLANE)
    tn = _tile(D, 256, LANE)
    nj = D // tn
    a_spec = pl.BlockSpec((tm, D), lambda i, j: (i, 0))
    o_spec = pl.BlockSpec((tm, Ho), lambda i, j: (i, 0))
    wb_spec = pl.BlockSpec((Ho, tn), lambda i, j: (0, j))

    def branch_spec(rows, br):
        return pl.BlockSpec((rows, tn), lambda i, j: (0, br * nj + j))

    return pl.pallas_call(
        _gate_merge_kernel,
        grid=(M // tm, nj),
        in_specs=[a_spec, o_spec, o_spec, o_spec,
                  branch_spec(D, 0), branch_spec(D, 1), branch_spec(D, 2),
                  branch_spec(1, 0), branch_spec(1, 1), branch_spec(1, 2),
                  wb_spec, wb_spec, wb_spec],
        out_specs=pl.BlockSpec((tm, tn), lambda i, j: (i, j)),
        out_shape=jax.ShapeDtypeStruct((M, D), BF16),
        compiler_params=_params("parallel", "arbitrary"),
    )(u, o_a, o_b, o_c, w_gate, w_gate, w_gate, b_gate, b_gate, b_gate, wb_a, wb_b, wb_c)


def _out_proj_kernel(a_ref, w_ref, x_ref, o_ref):
    o_ref[...] = x_ref[...] + jnp.dot(a_ref[...], w_ref[...], preferred_element_type=F32)


def out_proj(a, w, x):
    M, K = a.shape
    N = w.shape[1]
    tm = _tile(M, 1024, BF16_SUBLANE)
    tn = _tile(N, 512, LANE)
    return pl.pallas_call(
        _out_proj_kernel,
        grid=(M // tm, N // tn),
        in_specs=[pl.BlockSpec((tm, K), lambda i, j: (i, 0)),
                  pl.BlockSpec((K, tn), lambda i, j: (0, j)),
                  pl.BlockSpec((tm, tn), lambda i, j: (i, j))],
        out_specs=pl.BlockSpec((tm, tn), lambda i, j: (i, j)),
        out_shape=jax.ShapeDtypeStruct((M, N), F32),
        compiler_params=_params("parallel", "arbitrary"),
    )(a, w, x)


def _ple_kernel(a_ref, wg_ref, p_ref, wp_ref, x_ref, o_ref):
    gate = jax.nn.sigmoid(jnp.dot(a_ref[...], wg_ref[...], preferred_element_type=F32))
    emb = jnp.dot(p_ref[...], wp_ref[...], preferred_element_type=F32)
    o_ref[...] = x_ref[...] + gate * emb


def ple(a, wg, p, wp, x):
    M, K = a.shape
    N = wg.shape[1]
    P = p.shape[1]
    tm = _tile(M, 1024, BF16_SUBLANE)
    tn = _tile(N, 512, LANE)
    return pl.pallas_call(
        _ple_kernel,
        grid=(M // tm, N // tn),
        in_specs=[pl.BlockSpec((tm, K), lambda i, j: (i, 0)),
                  pl.BlockSpec((K, tn), lambda i, j: (0, j)),
                  pl.BlockSpec((tm, P), lambda i, j: (i, 0)),
                  pl.BlockSpec((P, tn), lambda i, j: (0, j)),
                  pl.BlockSpec((tm, tn), lambda i, j: (i, j))],
        out_specs=pl.BlockSpec((tm, tn), lambda i, j: (i, j)),
        out_shape=jax.ShapeDtypeStruct((M, N), F32),
        compiler_params=_params("parallel", "arbitrary"),
    )(a, wg, p, wp, x)


SCALE = HEAD_DIM ** -0.5
KEY_TILE = 256
QUERY_TILE = 256
SHORT_QUERY_KEY_TILE = 2048
ASSEMBLE_PAGES = 8
INT32_MIN = -2 ** 31

_NT = (((1,), (1,)), ((), ()))


def _qk(q, k):
    return lax.dot_general(q, k, _NT, preferred_element_type=F32)


def _softmax_init(m_ref, l_ref, acc_ref):
    m_ref[...] = jnp.full(m_ref.shape, NEG, F32)
    l_ref[...] = jnp.zeros(l_ref.shape, F32)
    acc_ref[...] = jnp.zeros(acc_ref.shape, F32)


def _softmax_step(s, mask, v, m_ref, l_ref, acc_ref, r):
    s = jnp.where(mask, s, NEG)
    m_prev = m_ref[r]
    m_new = jnp.maximum(m_prev, jnp.max(s, axis=-1, keepdims=True))
    p = jnp.where(mask, jnp.exp(s - m_new), 0.0)
    alpha = jnp.exp(m_prev - m_new)
    l_ref[r] = alpha * l_ref[r] + jnp.sum(p, axis=-1, keepdims=True)
    acc_ref[r] = alpha * acc_ref[r] + jnp.dot(p.astype(BF16), v, preferred_element_type=F32)
    m_ref[r] = m_new


def _near_or_far(near, step):
    @pl.when(near)
    def _():
        step(True)

    @pl.when(jnp.logical_not(near))
    def _():
        step(False)


def _softmax_out(l_ref, acc_ref, r):
    return acc_ref[r] / jnp.maximum(l_ref[r], 1e-30)


def _positions(qpos0, kpos0, tq, tk):
    qpos = qpos0 + lax.broadcasted_iota(jnp.int32, (tq, tk), 0)
    kpos = kpos0 + lax.broadcasted_iota(jnp.int32, (tq, tk), 1)
    return qpos, kpos


def _lane_column(x, idx):
    lane = lax.broadcasted_iota(jnp.int32, x.shape, 1)
    return jnp.sum(jnp.where(lane == idx, x, 0.0), axis=-1, keepdims=True)


def _t5_bucket(n):
    n = jnp.maximum(n, 0)
    exact = N_BUCKETS // 2
    nf = jnp.maximum(n, 1).astype(jnp.float32)
    big = exact + (jnp.log(nf / exact) * ((N_BUCKETS - exact) / math.log(MAX_DISTANCE / exact))).astype(jnp.int32)
    return jnp.where(n < exact, n, jnp.minimum(big, N_BUCKETS - 1))


def _bias_lookup(tab, dist):
    onehot = jax.nn.one_hot(_t5_bucket(dist), N_BUCKETS, dtype=F32)
    return jnp.einsum('...n,nh->h...', onehot, tab.astype(F32), precision=lax.Precision.HIGHEST)


def _toeplitz_bias(tab, tq, tk, offsets):
    i = jnp.arange(tq)[:, None]
    j = jnp.arange(tk)[None, :]
    d = jnp.stack([off + i - j for off in offsets])
    return _bias_lookup(tab, d) - tab[N_BUCKETS - 1].astype(F32)[:, None, None, None]


class _Geom:
    def __init__(self, b, T, past_len):
        self.b, self.T, self.past_len = b, T, past_len
        self.tq = min(QUERY_TILE, T)
        self.tk = KEY_TILE if T >= QUERY_TILE else _tile(past_len, SHORT_QUERY_KEY_TILE, KEY_TILE)
        assert T % self.tq == 0 and self.tk % self.tq == 0 and past_len % self.tk == 0
        assert self.tq == self.tk or T == self.tq, "a query tile must not straddle key tiles"
        assert MAX_DISTANCE <= self.tk, "bias must be constant two key tiles behind the query tile"
        self.nqt = T // self.tq
        self.L = past_len + T
        self.nkt = (self.L - 1) // self.tk + 1

    def q_tile(self, qt):
        return (self.past_len + qt * self.tq) // self.tk


def _bias_spec(geom, heads):
    return pl.BlockSpec((heads, 1, geom.tq, geom.tk),
                        lambda i, q, k: (0, jnp.clip(geom.q_tile(q) - k, 0, 1), 0, 0))


def _assemble_kernel(pt_ref, *refs, n_in, n_page_steps, T, R):
    ins, new_ref, o_ref = refs[:n_in], refs[n_in], refs[n_in + 1]
    s = pl.program_id(1)

    @pl.when(s < n_page_steps)
    def _():
        for p in range(n_in):
            for c in range(R):
                rows = ins[p][0, 0, pl.ds(c, PAGE_SIZE, stride=R), :] if R > 1 else ins[p][0, 0]
                o_ref[0, p * PAGE_SIZE:(p + 1) * PAGE_SIZE, c * HEAD_DIM:(c + 1) * HEAD_DIM] = rows.astype(o_ref.dtype)

    @pl.when(s >= n_page_steps)
    def _():
        o_ref[0] = jnp.zeros(o_ref.shape[1:], o_ref.dtype)

    @pl.when(s == n_page_steps)
    def _():
        o_ref[0, :T, :] = new_ref[0]


def assemble_cache(cache, layer, page_table, new, pad_rows):
    b, n_pages = page_table.shape
    T, C = new.shape[1], new.shape[2]
    R = C // HEAD_DIM
    cache = cache.reshape(cache.shape[0], cache.shape[1], PAGE_SIZE * R, HEAD_DIM)
    P = _tile(n_pages, ASSEMBLE_PAGES, 1)
    steps = n_pages // P
    assert pad_rows % (P * PAGE_SIZE) == 0 and T <= P * PAGE_SIZE
    pad_steps = pad_rows // (P * PAGE_SIZE)

    def page_spec(p):
        return pl.BlockSpec((1, 1, PAGE_SIZE * R, HEAD_DIM),
                            lambda i, s, pt: (layer, pt[i, jnp.minimum(s * P + p, n_pages - 1)], 0, 0))

    grid_spec = pltpu.PrefetchScalarGridSpec(
        num_scalar_prefetch=1,
        grid=(b, steps + pad_steps),
        in_specs=[page_spec(p) for p in range(P)] + [pl.BlockSpec((1, T, C), lambda i, s, pt: (i, 0, 0))],
        out_specs=pl.BlockSpec((1, P * PAGE_SIZE, C), lambda i, s, pt: (i, s, 0)),
    )
    return pl.pallas_call(
        functools.partial(_assemble_kernel, n_in=P, n_page_steps=steps, T=T, R=R),
        grid_spec=grid_spec,
        out_shape=jax.ShapeDtypeStruct((b, (steps + pad_steps) * P * PAGE_SIZE, C), BF16),
        compiler_params=_params("parallel", "arbitrary"),
    )(page_table, *([cache] * P), new)


def _indexer_kernel(qi_ref, wi_ref, ki_ref, sc_ref, thr_ref, key_ref, *, geom, topk):
    tq, tk = geom.tq, geom.tk
    qt, kt = pl.program_id(1), pl.program_id(2)
    qpos0 = geom.past_len + qt * tq
    last = geom.q_tile(qt)

    @pl.when(kt == 0)
    def _():
        sc_ref[0] = jnp.full(sc_ref.shape[1:], NEG, F32)
        key_ref[...] = jnp.full(key_ref.shape, INT32_MIN, jnp.int32)

    @pl.when(kt <= last)
    def _():
        ki = ki_ref[0]
        wi = wi_ref[0]
        acc = jnp.zeros((tq, tk), F32)
        for h in range(H_IDX):
            s = _qk(qi_ref[0, :, h * D_IDX:(h + 1) * D_IDX], ki)
            acc = acc + jnp.maximum(s, 0.0) * wi[:, h:h + 1]
        qpos, kpos = _positions(qpos0, kt * tk, tq, tk)
        score = jnp.where(kpos <= qpos, acc * (D_IDX ** -0.5 * H_IDX ** -0.5), NEG)
        sc_ref[0, kt] = score
        i = lax.bitcast_convert_type(score, jnp.int32)
        key_ref[kt] = i ^ ((i >> 31) & 0x7FFFFFFF)

    @pl.when(kt == geom.nkt - 1)
    def _():
        def body(bit, lo):
            cand = lo + jnp.left_shift(jnp.int32(1), 31 - bit)
            ge = (key_ref[...] >= cand[None]).astype(F32)
            cnt = jnp.sum(jnp.sum(ge, axis=0), axis=-1, keepdims=True)
            return jnp.where(cnt >= topk, cand, lo)

        lo = lax.fori_loop(0, 32, body, jnp.full((tq, 1), INT32_MIN, jnp.int32))
        thr = lax.bitcast_convert_type(lo ^ ((lo >> 31) & 0x7FFFFFFF), F32)
        thr_ref[0] = jnp.broadcast_to(thr, (tq, LANE))


def dsa_indexer(geom, qarr, qi_col, miscarr, misc_col, kiarr, ki_col):
    b, T, tq, tk, nkt = geom.b, geom.T, geom.tq, geom.tk, geom.nkt
    topk = min(TOPK_A, geom.L // 4)
    assert topk <= tk, "the first causal key tile alone must hold topk entries"
    return pl.pallas_call(
        functools.partial(_indexer_kernel, geom=geom, topk=topk),
        grid=(b, geom.nqt, nkt),
        in_specs=[pl.BlockSpec((1, tq, H_IDX * D_IDX), lambda i, q, k: (i, q, qi_col)),
                  pl.BlockSpec((1, tq, LANE), lambda i, q, k: (i, q, misc_col)),
                  pl.BlockSpec((1, tk, D_IDX), lambda i, q, k: (i, jnp.minimum(k, geom.q_tile(q)), ki_col))],
        out_specs=[pl.BlockSpec((1, nkt, tq, tk), lambda i, q, k: (i, 0, q, 0)),
                   pl.BlockSpec((1, tq, LANE), lambda i, q, k: (i, q, 0))],
        out_shape=[jax.ShapeDtypeStruct((b, nkt, T, tk), F32), jax.ShapeDtypeStruct((b, T, LANE), F32)],
        scratch_shapes=[pltpu.VMEM((nkt, tq, tk), jnp.int32)],
        compiler_params=_params("parallel", "parallel", "arbitrary"),
    )(qarr, miscarr, kiarr)


def _dsa_attn_kernel(q_ref, k_ref, v_ref, sc_ref, thr_ref, bias_ref, o_ref, m_ref, l_ref, acc_ref, *, geom, rep):
    tq, tk = geom.tq, geom.tk
    qt, kt = pl.program_id(1), pl.program_id(2)
    qpos0 = geom.past_len + qt * tq
    last = geom.q_tile(qt)

    @pl.when(kt == 0)
    def _():
        _softmax_init(m_ref, l_ref, acc_ref)

    @pl.when(kt <= last)
    def _():
        qpos, kpos = _positions(qpos0, kt * tk, tq, tk)
        mask = (sc_ref[0, 0] >= thr_ref[0][:, :1]) & (kpos <= qpos)

        def step(add_bias):
            for h in range(H_A):
                kv_cols = slice((h // rep) * HEAD_DIM, (h // rep + 1) * HEAD_DIM)
                s = _qk(q_ref[0, :, h * HEAD_DIM:(h + 1) * HEAD_DIM], k_ref[0, :, kv_cols]) * SCALE
                if add_bias:
                    s = s + bias_ref[h, 0]
                _softmax_step(s, mask, v_ref[0, :, kv_cols], m_ref, l_ref, acc_ref, h)

        _near_or_far((last - kt) <= 1, step)

    @pl.when(kt == geom.nkt - 1)
    def _():
        for h in range(H_A):
            o_ref[0, :, h * HEAD_DIM:(h + 1) * HEAD_DIM] = _softmax_out(l_ref, acc_ref, h).astype(o_ref.dtype)


def dsa_attention(geom, qarr, q_col, karr, k_col, v_col, scores, thr, bias):
    b, T, tq, tk, nkt = geom.b, geom.T, geom.tq, geom.tk, geom.nkt

    def kmap(col):
        return lambda i, q, k: (i, jnp.minimum(k, geom.q_tile(q)), col)

    return pl.pallas_call(
        functools.partial(_dsa_attn_kernel, geom=geom, rep=H_A // KV_A),
        grid=(b, geom.nqt, nkt),
        in_specs=[pl.BlockSpec((1, tq, H_A * HEAD_DIM), lambda i, q, k: (i, q, q_col)),
                  pl.BlockSpec((1, tk, KV_A * HEAD_DIM), kmap(k_col)),
                  pl.BlockSpec((1, tk, KV_A * HEAD_DIM), kmap(v_col)),
                  pl.BlockSpec((1, 1, tq, tk), lambda i, q, k: (i, jnp.minimum(k, geom.q_tile(q)), q, 0)),
                  pl.BlockSpec((1, tq, LANE), lambda i, q, k: (i, q, 0)),
                  _bias_spec(geom, H_A)],
        out_specs=pl.BlockSpec((1, tq, H_A * HEAD_DIM), lambda i, q, k: (i, q, 0)),
        out_shape=jax.ShapeDtypeStruct((b, T, H_A * HEAD_DIM), BF16),
        scratch_shapes=[pltpu.VMEM((H_A, tq, 1), F32), pltpu.VMEM((H_A, tq, 1), F32),
                        pltpu.VMEM((H_A, tq, HEAD_DIM), F32)],
        compiler_params=_params("parallel", "parallel", "arbitrary"),
    )(qarr, karr, karr, scores, thr, bias)


def _moba_select_kernel(q_ref, k_ref, avg_ref, sel_ref, *, geom, nblk, ksel):
    qt = pl.program_id(2)
    cur = (geom.past_len + qt * geom.tq) // MOBA_BLOCK
    kmean = jnp.dot(avg_ref[...], k_ref[0], preferred_element_type=F32)
    gs = _qk(q_ref[0], kmean.astype(BF16))
    lane = lax.broadcasted_iota(jnp.int32, gs.shape, 1)
    past = lane < cur
    g = jnp.where(past, gs, NEG)
    cnt = jnp.zeros_like(g)
    for m in range(nblk):
        col = g[:, m:m + 1]
        cnt = cnt + ((col > g) | ((col == g) & (lane > m))).astype(F32)
    sel_ref[0, 0] = (past & (cnt < ksel)).astype(F32)


def moba_select(geom, qarr, q_col, karr, k_col):
    b, T, tq = geom.b, geom.T, geom.tq
    nblk = geom.L // MOBA_BLOCK
    assert 0 < nblk <= LANE and MOBA_BLOCK % tq == 0 and geom.tk % MOBA_BLOCK == 0
    ksel = min(MOBA_TOPK, nblk)
    rows = nblk * MOBA_BLOCK
    avg = np.zeros((LANE, rows), np.float32)
    avg[np.arange(rows) // MOBA_BLOCK, np.arange(rows)] = 1.0 / MOBA_BLOCK
    return pl.pallas_call(
        functools.partial(_moba_select_kernel, geom=geom, nblk=nblk, ksel=ksel),
        grid=(b, H_C, geom.nqt),
        in_specs=[pl.BlockSpec((1, tq, HEAD_DIM), lambda i, h, q: (i, q, q_col + h)),
                  pl.BlockSpec((1, rows, HEAD_DIM), lambda i, h, q: (i, 0, k_col + h)),
                  pl.BlockSpec((LANE, rows), lambda i, h, q: (0, 0))],
        out_specs=pl.BlockSpec((1, 1, tq, LANE), lambda i, h, q: (i, h, q, 0)),
        out_shape=jax.ShapeDtypeStruct((b, H_C, T, LANE), F32),
        compiler_params=_params("parallel", "parallel", "arbitrary"),
    )(qarr, karr, jnp.asarray(avg, BF16))


def _moba_attn_kernel(q_ref, k_ref, v_ref, sel_ref, bias_ref, o_ref, m_ref, l_ref, acc_ref, *, geom):
    tq, tk = geom.tq, geom.tk
    qt, kt = pl.program_id(1), pl.program_id(2)
    qpos0 = geom.past_len + qt * tq
    cur = geom.q_tile(qt)

    @pl.when(kt == 0)
    def _():
        _softmax_init(m_ref, l_ref, acc_ref)

    @pl.when(kt <= cur)
    def _():
        qpos, kpos = _positions(qpos0, kt * tk, tq, tk)
        own_blk = qpos0 // MOBA_BLOCK
        blk = lax.broadcasted_iota(jnp.int32, (LANE, tk), 0)
        tok = lax.broadcasted_iota(jnp.int32, (LANE, tk), 1)
        expand = (blk == kt * (tk // MOBA_BLOCK) + tok // MOBA_BLOCK).astype(BF16)
        own = (kpos // MOBA_BLOCK == own_blk) & (kpos <= qpos)

        def step(add_bias):
            for h in range(H_C):
                cols = slice(h * HEAD_DIM, (h + 1) * HEAD_DIM)
                chosen = jnp.dot(sel_ref[0, h].astype(BF16), expand, preferred_element_type=F32) > 0.5
                s = _qk(q_ref[0, :, cols], k_ref[0, :, cols]) * SCALE
                if add_bias:
                    s = s + bias_ref[h, 0]
                _softmax_step(s, chosen | own, v_ref[0, :, cols], m_ref, l_ref, acc_ref, h)

        _near_or_far((cur - kt) <= 1, step)

    @pl.when(kt == geom.nkt - 1)
    def _():
        for h in range(H_C):
            o_ref[0, :, h * HEAD_DIM:(h + 1) * HEAD_DIM] = _softmax_out(l_ref, acc_ref, h).astype(o_ref.dtype)


def moba_attention(geom, qarr, q_col, karr, k_col, v_col, sel, bias):
    b, T, tq, tk, nkt = geom.b, geom.T, geom.tq, geom.tk, geom.nkt
    width = H_C * HEAD_DIM

    def kmap(col):
        return lambda i, q, k: (i, jnp.minimum(k, geom.q_tile(q)), col)

    return pl.pallas_call(
        functools.partial(_moba_attn_kernel, geom=geom),
        grid=(b, geom.nqt, nkt),
        in_specs=[pl.BlockSpec((1, tq, width), lambda i, q, k: (i, q, q_col)),
                  pl.BlockSpec((1, tk, width), kmap(k_col)),
                  pl.BlockSpec((1, tk, width), kmap(v_col)),
                  pl.BlockSpec((1, H_C, tq, LANE), lambda i, q, k: (i, 0, q, 0)),
                  _bias_spec(geom, H_C)],
        out_specs=pl.BlockSpec((1, tq, width), lambda i, q, k: (i, q, 0)),
        out_shape=jax.ShapeDtypeStruct((b, T, width), BF16),
        scratch_shapes=[pltpu.VMEM((H_C, tq, 1), F32), pltpu.VMEM((H_C, tq, 1), F32),
                        pltpu.VMEM((H_C, tq, HEAD_DIM), F32)],
        compiler_params=_params("parallel", "parallel", "arbitrary"),
    )(qarr, karr, karr, sel, bias)


CMP_ROW = CMP_STRIDE * 2 * KV_B * HEAD_DIM


def _compress_kernel(x_ref, xn_ref, wlo_ref, whi_ref, c_ref, w2_ref, g_ref, o_ref, *, tr):
    x = x_ref[0]
    lo = jnp.dot(x, wlo_ref[...], preferred_element_type=F32)
    hi = jnp.dot(x, whi_ref[...], preferred_element_type=F32)
    hi_next = jnp.dot(xn_ref[0], whi_ref[...], preferred_element_type=F32)
    row = lax.broadcasted_iota(jnp.int32, hi.shape, 0)
    hi = jnp.where(row == tr - 1, hi_next[0:1], pltpu.roll(hi, tr - 1, 0))
    act = jax.nn.gelu(lo + hi + c_ref[...]).astype(BF16)
    for c in range(2 * KV_B):
        cols = slice(c * HEAD_DIM, (c + 1) * HEAD_DIM)
        y = jnp.dot(act[:, cols], w2_ref[c // KV_B], preferred_element_type=F32)
        if c < KV_B:
            y = y * lax.rsqrt(jnp.mean(y * y, axis=-1, keepdims=True) + EPS) * g_ref[...]
        o_ref[0, :, cols] = y.astype(o_ref.dtype)


def nsa_compress(x, wlo, whi, const, w2, gain):
    b, R, _ = x.shape
    tr = _tile(R, 128, BF16_SUBLANE)
    nxt = tr // BF16_SUBLANE
    n_sub16 = R // BF16_SUBLANE
    C = 2 * KV_B * HEAD_DIM
    return pl.pallas_call(
        functools.partial(_compress_kernel, tr=tr),
        grid=(b, R // tr),
        in_specs=[pl.BlockSpec((1, tr, CMP_ROW), lambda i, r: (i, r, 0)),
                  pl.BlockSpec((1, BF16_SUBLANE, CMP_ROW), lambda i, r: (i, jnp.minimum((r + 1) * nxt, n_sub16 - 1), 0)),
                  pl.BlockSpec((CMP_ROW, C), lambda i, r: (0, 0)),
                  pl.BlockSpec((CMP_ROW, C), lambda i, r: (0, 0)),
                  pl.BlockSpec((1, C), lambda i, r: (0, 0)),
                  pl.BlockSpec((2, HEAD_DIM, HEAD_DIM), lambda i, r: (0, 0, 0)),
                  pl.BlockSpec((1, HEAD_DIM), lambda i, r: (0, 0))],
        out_specs=pl.BlockSpec((1, tr, C), lambda i, r: (i, r, 0)),
        out_shape=jax.ShapeDtypeStruct((b, R, C), BF16),
        compiler_params=_params("parallel", "arbitrary"),
    )(x, x, wlo, whi, const, w2, gain)


def _prep_compress(cmp_pos, cmp_w1, cmp_w2, kn_cmp):
    r = CMP_LEN // CMP_STRIDE
    assert r == 2
    w1 = cmp_w1.reshape(2, r, CMP_STRIDE, HEAD_DIM, HEAD_DIM)
    kv_of_col = np.repeat(np.arange(2), KV_B)
    eye = jnp.eye(2 * KV_B, dtype=cmp_w1.dtype)

    def expand(half):
        w = w1[kv_of_col, half]
        return jnp.einsum('crde,cf->rcdfe', w, eye).reshape(CMP_ROW, 2 * KV_B * HEAD_DIM).astype(BF16)

    const = jnp.einsum('krd,krde->ke', cmp_pos.reshape(2, CMP_LEN, HEAD_DIM),
                       cmp_w1.reshape(2, CMP_LEN, HEAD_DIM, HEAD_DIM), precision=lax.Precision.HIGHEST)
    const = const[kv_of_col].reshape(1, -1).astype(F32)
    return dict(wlo=expand(0), whi=expand(1), const=const, w2=cmp_w2.astype(BF16), gain=kn_cmp.reshape(1, -1).astype(F32))


def _nsa_cmp_kernel(q_ref, kc_ref, vc_ref, bias_ref, ov_ref, oc_ref, sel_ref, *, geom, rep, n_slc, nsel):
    tq = geom.tq
    qt = pl.program_id(2)
    qpos0 = geom.past_len + qt * tq
    kc, vc, ov = kc_ref[0], vc_ref[0], ov_ref[...]
    ncp, nsp = ov.shape
    qpos, n_idx = _positions(qpos0, 0, tq, ncp)
    valid = (n_idx * CMP_STRIDE + (CMP_LEN - 1)) <= qpos
    imp = jnp.zeros((tq, nsp), F32)
    for r in range(rep):
        cols = slice(r * HEAD_DIM, (r + 1) * HEAD_DIM)
        s = jnp.where(valid, _qk(q_ref[0, :, cols], kc) * SCALE + bias_ref[r], NEG)
        e = jnp.where(valid, jnp.exp(s - jnp.max(s, axis=-1, keepdims=True)), 0.0)
        p = (e / jnp.maximum(jnp.sum(e, axis=-1, keepdims=True), 1e-30)).astype(BF16)
        oc_ref[0, :, cols] = jnp.dot(p, vc, preferred_element_type=F32)
        imp = imp + jnp.dot(p, ov, preferred_element_type=F32)
    spos, jj = _positions(qpos0, 0, tq, nsp)
    cur = spos // SLC_LEN
    forced = (jj == 0) | (jj == cur) | (jj == cur - 1)
    imp = jnp.where(jj > cur, NEG, jnp.where(forced, -NEG, imp))
    cnt = jnp.zeros_like(imp)
    for s_blk in range(n_slc):
        col = imp[:, s_blk:s_blk + 1]
        cnt = cnt + ((col > imp) | ((col == imp) & (jj > s_blk))).astype(F32)
    sel_ref[0, 0] = ((cnt < nsel) & (jj < n_slc)).astype(F32)


def nsa_compressed(geom, qarr, q_col, cmpkv, bias_cmp):
    b, T, tq = geom.b, geom.T, geom.tq
    rep = H_B // KV_B
    ncp = cmpkv.shape[1]
    n_slc = -(-geom.L // SLC_LEN)
    nsel = min(N_SLC, n_slc)
    nsp = _round_up(n_slc, LANE)
    n_cmp = geom.L // CMP_STRIDE - CMP_LEN // CMP_STRIDE + 1
    cstart = np.arange(ncp) * CMP_STRIDE
    cend = cstart + CMP_LEN - 1
    sstart = np.arange(nsp) * SLC_LEN
    ov = ((cstart[:, None] < sstart[None, :] + SLC_LEN) & (cend[:, None] >= sstart[None, :])
          & (np.arange(ncp)[:, None] < n_cmp) & (np.arange(nsp)[None, :] < n_slc)).astype(np.float32)
    return pl.pallas_call(
        functools.partial(_nsa_cmp_kernel, geom=geom, rep=rep, n_slc=n_slc, nsel=nsel),
        grid=(b, KV_B, geom.nqt),
        in_specs=[pl.BlockSpec((1, tq, rep * HEAD_DIM), lambda i, g, q: (i, q, q_col + g)),
                  pl.BlockSpec((1, ncp, HEAD_DIM), lambda i, g, q: (i, 0, g)),
                  pl.BlockSpec((1, ncp, HEAD_DIM), lambda i, g, q: (i, 0, KV_B + g)),
                  pl.BlockSpec((rep, tq, ncp), lambda i, g, q: (g, q, 0)),
                  pl.BlockSpec((ncp, nsp), lambda i, g, q: (0, 0))],
        out_specs=[pl.BlockSpec((1, tq, rep * HEAD_DIM), lambda i, g, q: (i, q, g)),
                   pl.BlockSpec((1, 1, tq, nsp), lambda i, g, q: (i, g, q, 0))],
        out_shape=[jax.ShapeDtypeStruct((b, T, H_B * HEAD_DIM), F32), jax.ShapeDtypeStruct((b, KV_B, T, nsp), F32)],
        compiler_params=_params("parallel", "parallel", "arbitrary"),
    )(qarr, cmpkv, cmpkv, bias_cmp, jnp.asarray(ov, BF16))


class _Window:
    def __init__(self, arr, k_col, v_col, tw, n_back, tile0, pos0, bias, bias_always):
        self.arr, self.k_col, self.v_col, self.tw, self.n_back = arr, k_col, v_col, tw, n_back
        self.tile0, self.pos0, self.bias, self.bias_always = tile0, pos0, bias, bias_always


def _nsa_attn_kernel(q_ref, ks_ref, vs_ref, kw_ref, vw_ref, sel_ref, bias_ref, wbias_ref, oc_ref, misc_ref, o_ref,
                     m_ref, l_ref, acc_ref, *, geom, rep, win):
    tq, tk = geom.tq, geom.tk
    qt, kt = pl.program_id(1), pl.program_id(2)
    qpos0 = geom.past_len + qt * tq
    last = geom.q_tile(qt)

    @pl.when(kt == 0)
    def _():
        _softmax_init(m_ref, l_ref, acc_ref)

    def branch(g, k_ref, v_ref, mask, slot0, b_ref, add_bias):
        kv_cols = slice(g * HEAD_DIM, (g + 1) * HEAD_DIM)
        for h in range(g * rep, (g + 1) * rep):
            s = _qk(q_ref[0, :, h * HEAD_DIM:(h + 1) * HEAD_DIM], k_ref[0, :, kv_cols]) * SCALE
            if add_bias:
                s = s + b_ref[h, 0]
            _softmax_step(s, mask, v_ref[0, :, kv_cols], m_ref, l_ref, acc_ref, slot0 + h)

    near = (last - kt) <= 1

    @pl.when(kt <= last)
    def _():
        qpos, kpos = _positions(qpos0, kt * tk, tq, tk)
        nsp = sel_ref.shape[-1]
        blk = lax.broadcasted_iota(jnp.int32, (nsp, tk), 0)
        tok = lax.broadcasted_iota(jnp.int32, (nsp, tk), 1)
        expand = (blk == kt * (tk // SLC_LEN) + tok // SLC_LEN).astype(BF16)

        def step(add_bias):
            for g in range(KV_B):
                chosen = jnp.dot(sel_ref[0, g].astype(BF16), expand, preferred_element_type=F32) > 0.5
                branch(g, ks_ref, vs_ref, chosen & (kpos <= qpos), 0, bias_ref, add_bias)

        _near_or_far(near, step)

    @pl.when((kt <= last) & (kt >= last - win.n_back))
    def _():
        qpos, kpos = _positions(qpos0, win.pos0 + (kt - win.tile0) * win.tw, tq, win.tw)
        dist = qpos - kpos

        def step(add_bias):
            for g in range(KV_B):
                branch(g, kw_ref, vw_ref, (dist >= 0) & (dist < WINDOW), H_B, wbias_ref, add_bias)

        if win.bias_always:
            step(True)
        else:
            _near_or_far(near, step)

    @pl.when(kt == geom.nkt - 1)
    def _():
        gates = jax.nn.sigmoid(misc_ref[0])
        for h in range(H_B):
            cols = slice(h * HEAD_DIM, (h + 1) * HEAD_DIM)
            base = H_IDX + h * N_BRANCH
            o = (gates[:, base:base + 1] * oc_ref[0, :, cols]
                 + gates[:, base + 1:base + 2] * _softmax_out(l_ref, acc_ref, h)
                 + gates[:, base + 2:base + 3] * _softmax_out(l_ref, acc_ref, H_B + h))
            o_ref[0, :, cols] = o.astype(o_ref.dtype)


def nsa_attention(geom, qarr, q_col, sarr, ks_col, vs_col, win, sel, bias, oc, miscarr, misc_col):
    b, T, tq, tk, nkt = geom.b, geom.T, geom.tq, geom.tk, geom.nkt
    nsp = sel.shape[-1]
    tw = win.tw
    n_wt = win.arr.shape[1] // tw
    kv_width = KV_B * HEAD_DIM

    def smap(col):
        return lambda i, q, k: (i, jnp.minimum(k, geom.q_tile(q)), col)

    def wtile(q, k):
        last = geom.q_tile(q)
        return jnp.clip(jnp.clip(k, last - win.n_back, last) - win.tile0, 0, n_wt - 1)

    def wmap(col):
        return lambda i, q, k: (i, wtile(q, k), col)

    if win.bias_always:
        assert win.bias.shape[1] == 1 and n_wt == 1
        wbias_spec = pl.BlockSpec((H_B, 1, tq, tw), lambda i, q, k: (0, 0, 0, 0))
    else:
        wbias_spec = _bias_spec(geom, H_B)

    return pl.pallas_call(
        functools.partial(_nsa_attn_kernel, geom=geom, rep=H_B // KV_B, win=win),
        grid=(b, geom.nqt, nkt),
        in_specs=[pl.BlockSpec((1, tq, H_B * HEAD_DIM), lambda i, q, k: (i, q, q_col)),
                  pl.BlockSpec((1, tk, kv_width), smap(ks_col)),
                  pl.BlockSpec((1, tk, kv_width), smap(vs_col)),
                  pl.BlockSpec((1, tw, kv_width), wmap(win.k_col)),
                  pl.BlockSpec((1, tw, kv_width), wmap(win.v_col)),
                  pl.BlockSpec((1, KV_B, tq, nsp), lambda i, q, k: (i, 0, q, 0)),
                  _bias_spec(geom, H_B),
                  wbias_spec,
                  pl.BlockSpec((1, tq, H_B * HEAD_DIM), lambda i, q, k: (i, q, 0)),
                  pl.BlockSpec((1, tq, LANE), lambda i, q, k: (i, q, misc_col))],
        out_specs=pl.BlockSpec((1, tq, H_B * HEAD_DIM), lambda i, q, k: (i, q, 0)),
        out_shape=jax.ShapeDtypeStruct((b, T, H_B * HEAD_DIM), BF16),
        scratch_shapes=[pltpu.VMEM((2 * H_B, tq, 1), F32), pltpu.VMEM((2 * H_B, tq, 1), F32),
                        pltpu.VMEM((2 * H_B, tq, HEAD_DIM), F32)],
        compiler_params=_params("parallel", "parallel", "arbitrary"),
    )(qarr, sarr, sarr, win.arr, win.arr, sel, bias, win.bias, oc, miscarr)


_SEG = {}
_off = 0
for _name, _w in (("qa", H_A * HEAD_DIM), ("qb", H_B * HEAD_DIM), ("qi", H_IDX * D_IDX), ("qc", H_C * HEAD_DIM),
                  ("kvc", 2 * H_C * HEAD_DIM), ("kva", 2 * KV_A * HEAD_DIM), ("cmp", 2 * KV_B * HEAD_DIM),
                  ("slc", 2 * KV_B * HEAD_DIM), ("win", 2 * KV_B * HEAD_DIM), ("ki", D_IDX), ("misc", LANE)):
    _SEG[_name] = (_off, _off + _w)
    _off += _w
IN_COLS_ALIGNED = _off


def _col(name, width=LANE):
    assert _SEG[name][0] % width == 0
    return _SEG[name][0] // width


def _prep_in_proj(w_in, qn_a, kn_a, qn_b, kn_b, qn_c, kn_c):
    sizes = (H_A * HEAD_DIM, KV_A * HEAD_DIM, KV_A * HEAD_DIM, H_IDX * D_IDX, H_IDX, D_IDX,
             H_B * HEAD_DIM, N_BRANCH * H_B) + (KV_B * HEAD_DIM,) * 6 + (H_C * HEAD_DIM,) * 3
    splits = np.cumsum(sizes)[:-1].tolist()
    (qa, ka, va, qi, wi, ki, q_b, g_b, kbc, vbc, kbs, vbs, kbw, vbw, qc, kc, vc) = jnp.split(w_in, splits, axis=-1)
    D = w_in.shape[0]
    misc = jnp.concatenate([wi, g_b, jnp.zeros((D, LANE - H_IDX - N_BRANCH * H_B), w_in.dtype)], axis=-1)
    n_pad = _round_up(IN_COLS_ALIGNED, 512) - IN_COLS_ALIGNED
    w = jnp.concatenate([qa, q_b, qi, qc, kc, vc, ka, va, kbc, vbc, kbs, vbs, kbw, vbw, ki, misc,
                         jnp.zeros((D, n_pad), w_in.dtype)], axis=-1).astype(BF16)
    one = jnp.ones((HEAD_DIM,), F32)

    def rep(g, n):
        return jnp.tile(g.astype(F32), n)

    gains = jnp.concatenate([
        rep(qn_a, H_A), rep(qn_b, H_B), rep(one, H_IDX), rep(qn_c, H_C), rep(kn_c, H_C), rep(one, H_C),
        rep(kn_a, KV_A), rep(one, KV_A), rep(one, 2 * KV_B), rep(kn_b[1], KV_B), rep(one, KV_B),
        rep(kn_b[2], KV_B), rep(one, KV_B), one, one, jnp.ones((n_pad,), F32)]).reshape(1, -1)
    flags = np.concatenate([
        np.ones(H_A), np.ones(H_B), np.zeros(H_IDX), np.ones(H_C), np.ones(H_C), np.zeros(H_C),
        np.ones(KV_A), np.zeros(KV_A), np.zeros(2 * KV_B), np.ones(KV_B), np.zeros(KV_B),
        np.ones(KV_B), np.zeros(KV_B), np.zeros(1), np.zeros(1), np.zeros(n_pad // LANE)]).astype(np.int32)
    return w, gains, jnp.asarray(flags)


def _prep_layer(i, ffn1_norm, ffn1_wg, ffn1_wu, ffn1_wd, mix_norm, w_in, qn_a, kn_a, qn_b, kn_b, cmp_pos, cmp_w1,
                cmp_w2, qn_c, kn_c, wb_a, wb_b, wb_c, w_gate, b_gate, w_out, ffn2_norm, ffn2_wg, ffn2_wu, ffn2_wd,
                ple_norm, ple_wg, ple_wp):
    w_in_p, gains, flags = _prep_in_proj(w_in[i], qn_a[i], kn_a[i], qn_b[i], kn_b[i], qn_c[i], kn_c[i])
    return dict(
        layer=i, f1n=ffn1_norm[i], f1g=ffn1_wg, f1u=ffn1_wu, f1d=ffn1_wd[i].astype(BF16), mn=mix_norm[i],
        w_in=w_in_p, gains=gains, flags=flags,
        cmp=_prep_compress(cmp_pos[i], cmp_w1[i], cmp_w2[i], kn_b[i, 0]),
        wb_a=wb_a[i].astype(BF16), wb_b=wb_b[i].astype(BF16), wb_c=wb_c[i].astype(BF16),
        w_gate=w_gate[i].astype(BF16), b_gate=b_gate[i].reshape(1, -1), w_out=w_out[i].astype(BF16),
        f2n=ffn2_norm[i], f2g=ffn2_wg, f2u=ffn2_wu, f2d=ffn2_wd[i].astype(BF16),
        pn=ple_norm[i], pwg=ple_wg[i].astype(BF16), pwp=ple_wp[i].astype(BF16))


def _layer(x, p, geom, layer, caches, W, bias):
    b, T = geom.b, geom.T
    M = b * T
    h = ffn_down(ffn_up(rmsnorm_bf16(x, W["f1n"]), W["f1g"], W["f1u"], W["layer"]), W["f1d"], x)
    u = rmsnorm_bf16(h, W["mn"])
    proj32, proj16 = in_proj(u, W["w_in"], W["gains"], W["flags"])
    Np = proj32.shape[1]
    p32 = proj32.reshape(b, T, Np)
    p16 = proj16.reshape(b, T, Np)

    def seg32(name):
        lo, hi = _SEG[name]
        return p32[:, :, lo:hi]

    def seg16(name):
        lo, hi = _SEG[name]
        return p16[:, :, lo:hi]

    win_new = seg32("win")
    if caches is None:
        assert T >= WINDOW
        a_kv, a_ki, slc, c_kv = (p16,) * 4
        a_k, ki_c = _col("kva", KV_A * HEAD_DIM), _col("ki")
        s_k = _col("slc", KV_B * HEAD_DIM)
        c_k = _col("kvc", H_C * HEAD_DIM)
        w_k = _col("win", KV_B * HEAD_DIM)
        win = _Window(p16, w_k, w_k + 1, geom.tk, WINDOW // geom.tk, 0, 0, bias["b"], False)
        cmp_rows = seg16("cmp")
        win_state = win_new[:, T - WINDOW:]
    else:
        cache_a_kv, cache_a_kidx, cache_b_cmp, cache_b_slc, win_past, cache_c_kv, page_table = caches
        a_kv = assemble_cache(cache_a_kv, layer, page_table, seg16("kva"), geom.tk)
        a_ki = assemble_cache(cache_a_kidx, layer, page_table, seg16("ki"), geom.tk)
        cmp_rows = assemble_cache(cache_b_cmp, layer, page_table, seg16("cmp"), geom.tk)
        slc = assemble_cache(cache_b_slc, layer, page_table, seg16("slc"), geom.tk)
        c_kv = assemble_cache(cache_c_kv, layer, page_table, seg16("kvc"), geom.tk)
        a_k, ki_c, s_k, c_k = 0, 0, 0, 0
        n_buf = win_past.shape[2]
        win_past = win_past[layer].reshape(b, n_buf, -1)
        warr = jnp.concatenate([win_past.astype(BF16), seg16("win"),
                                jnp.zeros((b, KEY_TILE - T, win_past.shape[-1]), BF16)], axis=1)
        win = _Window(warr, 0, 1, warr.shape[1], 0, geom.q_tile(0), geom.past_len - n_buf, bias["b_win"], True)
        win_state = jnp.concatenate([win_past, win_new], axis=1)[:, -WINDOW:]

    scores, thr = dsa_indexer(geom, p16, _col("qi", H_IDX * D_IDX), p32, _col("misc"), a_ki, ki_c)
    o_a = dsa_attention(geom, p16, _col("qa", H_A * HEAD_DIM), a_kv, a_k, a_k + 1, scores, thr, bias["a"])
    cmpkv = nsa_compress(cmp_rows.reshape(b, -1, CMP_ROW), **W["cmp"])
    qpos = geom.past_len + jnp.arange(T)
    cend = jnp.arange(cmpkv.shape[1]) * CMP_STRIDE + CMP_LEN - 1
    bias_cmp = _bias_lookup(bias["tab_b"], qpos[:, None] - cend[None, :])
    oc, sel_b = nsa_compressed(geom, p16, _col("qb", 4 * HEAD_DIM), cmpkv, bias_cmp)
    o_b = nsa_attention(geom, p16, _col("qb", H_B * HEAD_DIM), slc, s_k, s_k + 1, win, sel_b, bias["b"], oc, p32,
                        _col("misc"))
    sel_c = moba_select(geom, p16, _col("qc"), c_kv, c_k * H_C)
    o_c = moba_attention(geom, p16, _col("qc", H_C * HEAD_DIM), c_kv, c_k, c_k + 1, sel_c, bias["c"])

    merged = gate_merge(u, o_a.reshape(M, -1), o_b.reshape(M, -1), o_c.reshape(M, -1),
                        W["w_gate"], W["b_gate"], W["wb_a"], W["wb_b"], W["wb_c"])
    h = out_proj(merged, W["w_out"], h)
    h = ffn_down(ffn_up(rmsnorm_bf16(h, W["f2n"]), W["f2g"], W["f2u"], W["layer"]), W["f2d"], h)
    h = ple(rmsnorm_bf16(h, W["pn"]), W["pwg"], p.astype(BF16), W["pwp"], h)
    state = (seg32("kva").reshape(b, T, 2, KV_A, HEAD_DIM), seg32("ki"), seg32("cmp").reshape(b, T, 2, KV_B, HEAD_DIM),
             seg32("slc").reshape(b, T, 2, KV_B, HEAD_DIM), win_state.reshape(b, WINDOW, 2, KV_B, HEAD_DIM),
             seg32("kvc").reshape(b, T, 2, H_C, HEAD_DIM))
    return h, state


def kernel(x_prompt, x_sample, cache_a_kv, cache_a_kidx, cache_b_cmp_kv, cache_b_slc_kv, state_b_win_kv, cache_c_kv, page_table, p_prompt, p_sample, rel_bias, ffn1_norm, ffn1_wg, ffn1_wu, ffn1_wd, mix_norm, w_in, qn_a, kn_a, qn_b, kn_b, cmp_pos, cmp_w1, cmp_w2, qn_c, kn_c, wb_a, wb_b, wb_c, w_gate, b_gate, w_out, ffn2_norm, ffn2_wg, ffn2_wu, ffn2_wd, ple_norm, ple_wg, ple_wp):
    depth = ffn1_norm.shape[0]
    bp, Tp, D = x_prompt.shape
    bs, Ts, _ = x_sample.shape
    geom_p = _Geom(bp, Tp, 0)
    geom_s = _Geom(bs, Ts, page_table.shape[1] * PAGE_SIZE)
    n_buf = state_b_win_kv.shape[2]
    assert n_buf >= WINDOW and Ts <= KEY_TILE
    tab_a, tab_b, tab_c = rel_bias[:, :H_A], rel_bias[:, H_A:H_A + H_B], rel_bias[:, H_A + H_B:]

    def tile_bias(geom):
        near = (0, geom.tk)
        return dict(a=_toeplitz_bias(tab_a, geom.tq, geom.tk, near), b=_toeplitz_bias(tab_b, geom.tq, geom.tk, near),
                    c=_toeplitz_bias(tab_c, geom.tq, geom.tk, near), tab_b=tab_b)

    bias_p = tile_bias(geom_p)
    bias_s = tile_bias(geom_s)
    bias_s["b_win"] = _toeplitz_bias(tab_b, geom_s.tq, n_buf + KEY_TILE, (n_buf,))
    caches = (cache_a_kv, cache_a_kidx, cache_b_cmp_kv, cache_b_slc_kv, state_b_win_kv, cache_c_kv, page_table)

    y_p = x_prompt.reshape(bp * Tp, D)
    y_s = x_sample.reshape(bs * Ts, D)
    sp_list, ss_list = [], []
    for i in range(depth):
        W = _prep_layer(i, ffn1_norm, ffn1_wg, ffn1_wu, ffn1_wd, mix_norm, w_in, qn_a, kn_a, qn_b, kn_b, cmp_pos,
                        cmp_w1, cmp_w2, qn_c, kn_c, wb_a, wb_b, wb_c, w_gate, b_gate, w_out, ffn2_norm, ffn2_wg,
                        ffn2_wu, ffn2_wd, ple_norm, ple_wg, ple_wp)
        y_p, sp = _layer(y_p, p_prompt[i].reshape(bp * Tp, -1), geom_p, i, None, W, bias_p)
        y_s, ss = _layer(y_s, p_sample[i].reshape(bs * Ts, -1), geom_s, i, caches, W, bias_s)
        sp_list.append(sp)
        ss_list.append(ss)

    outs = [y_p.reshape(bp, Tp, D), y_s.reshape(bs, Ts, D)]
    for j in range(6):
        outs.append(jnp.stack([s[j] for s in sp_list]))
        outs.append(jnp.stack([s[j] for s in ss_list]))
    return tuple(outs)
```

```python
import functools
import math

import numpy as np
import jax
import jax.numpy as jnp
from jax import lax
from jax.experimental import pallas as pl
from jax.experimental.pallas import tpu as pltpu

PAGE_SIZE = 128
HEAD_DIM = 128
H_A = 8
KV_A = 2
H_IDX = 16
D_IDX = 128
TOPK_A = 256
H_B = 8
KV_B = 2
CMP_LEN = 32
CMP_STRIDE = 16
SLC_LEN = 64
N_SLC = 16
WINDOW = 512
H_C = 8
MOBA_BLOCK = 256
MOBA_TOPK = 3
N_BUCKETS = 32
MAX_DISTANCE = 128
N_BRANCH = 3
EPS = 1e-6
NEG = -1e30

LANE = 128
BF16_SUBLANE = 16
V7X_VMEM_LIMIT = 56 * 1024 * 1024

F32 = jnp.float32
BF16 = jnp.bfloat16


def _tile(n, target, align):
    best = None
    for t in range(align, min(n, target) + 1, align):
        if n % t == 0:
            best = t
    return best if best is not None else n


def _round_up(n, m):
    return -(-n // m) * m


def _params(*sem):
    return pltpu.CompilerParams(dimension_semantics=sem, vmem_limit_bytes=V7X_VMEM_LIMIT)


def _rmsnorm_kernel(x_ref, g_ref, o_ref):
    x = x_ref[...]
    y = x * lax.rsqrt(jnp.mean(x * x, axis=-1, keepdims=True) + EPS)
    o_ref[...] = (y * g_ref[...]).astype(o_ref.dtype)


def rmsnorm_bf16(x, g):
    M, D = x.shape
    tm = _tile(M, 256, BF16_SUBLANE)
    return pl.pallas_call(
        _rmsnorm_kernel,
        grid=(M // tm,),
        in_specs=[pl.BlockSpec((tm, D), lambda i: (i, 0)), pl.BlockSpec((1, D), lambda i: (0, 0))],
        out_specs=pl.BlockSpec((tm, D), lambda i: (i, 0)),
        out_shape=jax.ShapeDtypeStruct((M, D), BF16),
        compiler_params=_params("parallel"),
    )(x, g.reshape(1, D))


def _ffn_up_kernel(a_ref, wg_ref, wu_ref, o_ref):
    a = a_ref[...]
    g = jnp.dot(a, wg_ref[...].astype(BF16), preferred_element_type=F32)
    u = jnp.dot(a, wu_ref[...].astype(BF16), preferred_element_type=F32)
    o_ref[...] = (g * jax.nn.sigmoid(g) * u).astype(o_ref.dtype)


def ffn_up(a, wg, wu, layer):
    M, D = a.shape
    F = wg.shape[2]
    tm = _tile(M, 1024, BF16_SUBLANE)
    tn = _tile(F, 256, LANE)
    return pl.pallas_call(
        _ffn_up_kernel,
        grid=(M // tm, F // tn),
        in_specs=[pl.BlockSpec((tm, D), lambda i, j: (i, 0)),
                  pl.BlockSpec((None, D, tn), lambda i, j: (layer, 0, j)),
                  pl.BlockSpec((None, D, tn), lambda i, j: (layer, 0, j))],
        out_specs=pl.BlockSpec((tm, tn), lambda i, j: (i, j)),
        out_shape=jax.ShapeDtypeStruct((M, F), BF16),
        compiler_params=_params("parallel", "arbitrary"),
    )(a, wg, wu)


def _ffn_down_kernel(a_ref, w_ref, x_ref, o_ref, acc_ref):
    k = pl.program_id(2)

    @pl.when(k == 0)
    def _():
        acc_ref[...] = jnp.zeros_like(acc_ref)

    acc_ref[...] += jnp.dot(a_ref[...], w_ref[...], preferred_element_type=F32)

    @pl.when(k == pl.num_programs(2) - 1)
    def _():
        o_ref[...] = x_ref[...] + 0.5 * acc_ref[...]


def ffn_down(a, w, x):
    M, F = a.shape
    D = w.shape[1]
    tm = _tile(M, 1024, BF16_SUBLANE)
    tn = _tile(D, 512, LANE)
    tk = _tile(F, 5504, LANE)
    return pl.pallas_call(
        _ffn_down_kernel,
        grid=(M // tm, D // tn, F // tk),
        in_specs=[pl.BlockSpec((tm, tk), lambda i, j, k: (i, k)),
                  pl.BlockSpec((tk, tn), lambda i, j, k: (k, j)),
                  pl.BlockSpec((tm, tn), lambda i, j, k: (i, j))],
        out_specs=pl.BlockSpec((tm, tn), lambda i, j, k: (i, j)),
        out_shape=jax.ShapeDtypeStruct((M, D), F32),
        scratch_shapes=[pltpu.VMEM((tm, tn), F32)],
        compiler_params=_params("parallel", "arbitrary", "arbitrary"),
    )(a, w, x)


def _w_in_kernel(flags_ref, a_ref, w_ref, g_ref, o_ref, o16_ref, *, heads_per_tile):
    j = pl.program_id(1)
    acc = jnp.dot(a_ref[...], w_ref[...], preferred_element_type=F32)
    for c in range(heads_per_tile):
        cols = slice(c * HEAD_DIM, (c + 1) * HEAD_DIM)
        y = acc[:, cols]
        yn = y * lax.rsqrt(jnp.mean(y * y, axis=-1, keepdims=True) + EPS) * g_ref[:, cols]
        y = jnp.where(flags_ref[j * heads_per_tile + c] > 0, yn, y)
        o_ref[:, cols] = y
        o16_ref[:, cols] = y.astype(o16_ref.dtype)


def in_proj(a, w, gains, flags):
    M, D = a.shape
    N = w.shape[1]
    tm = _tile(M, 1024, BF16_SUBLANE)
    tn = _tile(N, 512, LANE)
    grid_spec = pltpu.PrefetchScalarGridSpec(
        num_scalar_prefetch=1,
        grid=(M // tm, N // tn),
        in_specs=[pl.BlockSpec((tm, D), lambda i, j, f: (i, 0)),
                  pl.BlockSpec((D, tn), lambda i, j, f: (0, j)),
                  pl.BlockSpec((1, tn), lambda i, j, f: (0, j))],
        out_specs=[pl.BlockSpec((tm, tn), lambda i, j, f: (i, j)), pl.BlockSpec((tm, tn), lambda i, j, f: (i, j))],
    )
    return pl.pallas_call(
        functools.partial(_w_in_kernel, heads_per_tile=tn // HEAD_DIM),
        grid_spec=grid_spec,
        out_shape=[jax.ShapeDtypeStruct((M, N), F32), jax.ShapeDtypeStruct((M, N), BF16)],
        compiler_params=_params("parallel", "arbitrary"),
    )(flags, a, w, gains)


def _gate_merge_kernel(u_ref, oa_ref, ob_ref, oc_ref, wg0_ref, wg1_ref, wg2_ref, bg0_ref, bg1_ref, bg2_ref,
                       wba_ref, wbb_ref, wbc_ref, o_ref):
    u = u_ref[...]
    out = None
    for o_r, wg_r, bg_r, wb_r in ((oa_ref, wg0_ref, bg0_ref, wba_ref),
                                  (ob_ref, wg1_ref, bg1_ref, wbb_ref),
                                  (oc_ref, wg2_ref, bg2_ref, wbc_ref)):
        gate = jax.nn.sigmoid(jnp.dot(u, wg_r[...], preferred_element_type=F32) + bg_r[...])
        branch = jnp.dot(o_r[...], wb_r[...], preferred_element_type=F32)
        out = gate * branch if out is None else out + gate * branch
    o_ref[...] = out.astype(o_ref.dtype)


def gate_merge(u, o_a, o_b, o_c, w_gate, b_gate, wb_a, wb_b, wb_c):
    M, D = u.shape
    Ho = o_a.shape[1]
    tm = _tile(M, 1024, BF16_SUBLANE)
    tn = _tile(D, 256, LANE)
    nj = D // tn
    a_spec = pl.BlockSpec((tm, D), lambda i, j: (i, 0))
    o_spec = pl.BlockSpec((tm, Ho), lambda i, j: (i, 0))
    wb_spec = pl.BlockSpec((Ho, tn), lambda i, j: (0, j))

    def branch_spec(rows, br):
        return pl.BlockSpec((rows, tn), lambda i, j: (0, br * nj + j))

    return pl.pallas_call(
        _gate_merge_kernel,
        grid=(M // tm, nj),
        in_specs=[a_spec, o_spec, o_spec, o_spec,
                  branch_spec(D, 0), branch_spec(D, 1), branch_spec(D, 2),
                  branch_spec(1, 0), branch_spec(1, 1), branch_spec(1, 2),
                  wb_spec, wb_spec, wb_spec],
        out_specs=pl.BlockSpec((tm, tn), lambda i, j: (i, j)),
        out_shape=jax.ShapeDtypeStruct((M, D), BF16),
        compiler_params=_params("parallel", "arbitrary"),
    )(u, o_a, o_b, o_c, w_gate, w_gate, w_gate, b_gate, b_gate, b_gate, wb_a, wb_b, wb_c)


def _out_proj_kernel(a_ref, w_ref, x_ref, o_ref):
    o_ref[...] = x_ref[...] + jnp.dot(a_ref[...], w_ref[...].astype(BF16), preferred_element_type=F32)


def out_proj(a, w, x, layer):
    M, K = a.shape
    N = w.shape[2]
    tm = _tile(M, 1024, BF16_SUBLANE)
    tn = _tile(N, 512, LANE)
    return pl.pallas_call(
        _out_proj_kernel,
        grid=(M // tm, N // tn),
        in_specs=[pl.BlockSpec((tm, K), lambda i, j: (i, 0)),
                  pl.BlockSpec((None, K, tn), lambda i, j: (layer, 0, j)),
                  pl.BlockSpec((tm, tn), lambda i, j: (i, j))],
        out_specs=pl.BlockSpec((tm, tn), lambda i, j: (i, j)),
        out_shape=jax.ShapeDtypeStruct((M, N), F32),
        compiler_params=_params("parallel", "arbitrary"),
    )(a, w, x)


def _ple_kernel(a_ref, wg_ref, p_ref, wp_ref, x_ref, o_ref):
    gate = jax.nn.sigmoid(jnp.dot(a_ref[...], wg_ref[...].astype(BF16), preferred_element_type=F32))
    emb = jnp.dot(p_ref[...], wp_ref[...], preferred_element_type=F32)
    o_ref[...] = x_ref[...] + gate * emb


def ple(a, wg, p, wp, x, layer):
    M, K = a.shape
    N = wg.shape[2]
    P = p.shape[1]
    tm = _tile(M, 1024, BF16_SUBLANE)
    tn = _tile(N, 512, LANE)
    return pl.pallas_call(
        _ple_kernel,
        grid=(M // tm, N // tn),
        in_specs=[pl.BlockSpec((tm, K), lambda i, j: (i, 0)),
                  pl.BlockSpec((None, K, tn), lambda i, j: (layer, 0, j)),
                  pl.BlockSpec((tm, P), lambda i, j: (i, 0)),
                  pl.BlockSpec((P, tn), lambda i, j: (0, j)),
                  pl.BlockSpec((tm, tn), lambda i, j: (i, j))],
        out_specs=pl.BlockSpec((tm, tn), lambda i, j: (i, j)),
        out_shape=jax.ShapeDtypeStruct((M, N), F32),
        compiler_params=_params("parallel", "arbitrary"),
    )(a, wg, p, wp, x)


SCALE = HEAD_DIM ** -0.5
KEY_TILE = 256
QUERY_TILE = 256
SHORT_QUERY_KEY_TILE = 2048
ASSEMBLE_PAGES = 8
INT32_MIN = -2 ** 31

_NT = (((1,), (1,)), ((), ()))


def _qk(q, k):
    return lax.dot_general(q, k, _NT, preferred_element_type=F32)


def _softmax_init(m_ref, l_ref, acc_ref):
    m_ref[...] = jnp.full(m_ref.shape, NEG, F32)
    l_ref[...] = jnp.zeros(l_ref.shape, F32)
    acc_ref[...] = jnp.zeros(acc_ref.shape, F32)


def _softmax_step(s, mask, v, m_ref, l_ref, acc_ref, r):
    s = jnp.where(mask, s, NEG)
    m_prev = m_ref[r]
    m_new = jnp.maximum(m_prev, jnp.max(s, axis=-1, keepdims=True))
    p = jnp.where(mask, jnp.exp(s - m_new), 0.0)
    alpha = jnp.exp(m_prev - m_new)
    l_ref[r] = alpha * l_ref[r] + jnp.sum(p, axis=-1, keepdims=True)
    acc_ref[r] = alpha * acc_ref[r] + jnp.dot(p.astype(BF16), v, preferred_element_type=F32)
    m_ref[r] = m_new


def _near_or_far(near, step):
    @pl.when(near)
    def _():
        step(True)

    @pl.when(jnp.logical_not(near))
    def _():
        step(False)


def _softmax_out(l_ref, acc_ref, r):
    return acc_ref[r] / jnp.maximum(l_ref[r], 1e-30)


def _positions(qpos0, kpos0, tq, tk):
    qpos = qpos0 + lax.broadcasted_iota(jnp.int32, (tq, tk), 0)
    kpos = kpos0 + lax.broadcasted_iota(jnp.int32, (tq, tk), 1)
    return qpos, kpos


def _lane_column(x, idx):
    lane = lax.broadcasted_iota(jnp.int32, x.shape, 1)
    return jnp.sum(jnp.where(lane == idx, x, 0.0), axis=-1, keepdims=True)


def _t5_bucket(n):
    n = jnp.maximum(n, 0)
    exact = N_BUCKETS // 2
    nf = jnp.maximum(n, 1).astype(jnp.float32)
    big = exact + (jnp.log(nf / exact) * ((N_BUCKETS - exact) / math.log(MAX_DISTANCE / exact))).astype(jnp.int32)
    return jnp.where(n < exact, n, jnp.minimum(big, N_BUCKETS - 1))


def _bias_lookup(tab, dist):
    onehot = jax.nn.one_hot(_t5_bucket(dist), N_BUCKETS, dtype=F32)
    return jnp.einsum('...n,nh->h...', onehot, tab.astype(F32), precision=lax.Precision.HIGHEST)


def _toeplitz_bias(tab, tq, tk, offsets):
    i = jnp.arange(tq)[:, None]
    j = jnp.arange(tk)[None, :]
    d = jnp.stack([off + i - j for off in offsets])
    return _bias_lookup(tab, d) - tab[N_BUCKETS - 1].astype(F32)[:, None, None, None]


class _Geom:
    def __init__(self, b, T, past_len):
        self.b, self.T, self.past_len = b, T, past_len
        self.tq = min(QUERY_TILE, T)
        self.tk = KEY_TILE if T >= QUERY_TILE else _tile(past_len, SHORT_QUERY_KEY_TILE, KEY_TILE)
        assert T % self.tq == 0 and self.tk % self.tq == 0 and past_len % self.tk == 0
        assert self.tq == self.tk or T == self.tq, "a query tile must not straddle key tiles"
        assert MAX_DISTANCE <= self.tk, "bias must be constant two key tiles behind the query tile"
        self.nqt = T // self.tq
        self.L = past_len + T
        self.nkt = (self.L - 1) // self.tk + 1

    def q_tile(self, qt):
        return (self.past_len + qt * self.tq) // self.tk


def _bias_spec(geom, heads):
    return pl.BlockSpec((heads, 1, geom.tq, geom.tk),
                        lambda i, q, k: (0, jnp.clip(geom.q_tile(q) - k, 0, 1), 0, 0))


def _assemble_kernel(pt_ref, *refs, n_in, n_page_steps, T, R):
    ins, new_ref, o_ref = refs[:n_in], refs[n_in], refs[n_in + 1]
    s = pl.program_id(1)

    @pl.when(s < n_page_steps)
    def _():
        for p in range(n_in):
            for c in range(R):
                rows = ins[p][0, 0, pl.ds(c, PAGE_SIZE, stride=R), :] if R > 1 else ins[p][0, 0]
                o_ref[0, p * PAGE_SIZE:(p + 1) * PAGE_SIZE, c * HEAD_DIM:(c + 1) * HEAD_DIM] = rows.astype(o_ref.dtype)

    @pl.when(s >= n_page_steps)
    def _():
        o_ref[0] = jnp.zeros(o_ref.shape[1:], o_ref.dtype)

    @pl.when(s == n_page_steps)
    def _():
        o_ref[0, :T, :] = new_ref[0]


def assemble_cache(cache, layer, page_table, new, pad_rows):
    b, n_pages = page_table.shape
    T, C = new.shape[1], new.shape[2]
    R = C // HEAD_DIM
    cache = cache.reshape(cache.shape[0], cache.shape[1], PAGE_SIZE * R, HEAD_DIM)
    P = _tile(n_pages, ASSEMBLE_PAGES, 1)
    steps = n_pages // P
    assert pad_rows % (P * PAGE_SIZE) == 0 and T <= P * PAGE_SIZE
    pad_steps = pad_rows // (P * PAGE_SIZE)

    def page_spec(p):
        return pl.BlockSpec((1, 1, PAGE_SIZE * R, HEAD_DIM),
                            lambda i, s, pt: (layer, pt[i, jnp.minimum(s * P + p, n_pages - 1)], 0, 0))

    grid_spec = pltpu.PrefetchScalarGridSpec(
        num_scalar_prefetch=1,
        grid=(b, steps + pad_steps),
        in_specs=[page_spec(p) for p in range(P)] + [pl.BlockSpec((1, T, C), lambda i, s, pt: (i, 0, 0))],
        out_specs=pl.BlockSpec((1, P * PAGE_SIZE, C), lambda i, s, pt: (i, s, 0)),
    )
    return pl.pallas_call(
        functools.partial(_assemble_kernel, n_in=P, n_page_steps=steps, T=T, R=R),
        grid_spec=grid_spec,
        out_shape=jax.ShapeDtypeStruct((b, (steps + pad_steps) * P * PAGE_SIZE, C), BF16),
        compiler_params=_params("parallel", "arbitrary"),
    )(page_table, *([cache] * P), new)


def _indexer_kernel(qi_ref, wi_ref, ki_ref, sc_ref, thr_ref, key_ref, *, geom, topk):
    tq, tk = geom.tq, geom.tk
    qt, kt = pl.program_id(1), pl.program_id(2)
    qpos0 = geom.past_len + qt * tq
    last = geom.q_tile(qt)

    @pl.when(kt == 0)
    def _():
        sc_ref[0] = jnp.full(sc_ref.shape[1:], NEG, F32)
        key_ref[...] = jnp.full(key_ref.shape, INT32_MIN, jnp.int32)

    @pl.when(kt <= last)
    def _():
        ki = ki_ref[0]
        wi = wi_ref[0]
        acc = jnp.zeros((tq, tk), F32)
        for h in range(H_IDX):
            s = _qk(qi_ref[0, :, h * D_IDX:(h + 1) * D_IDX], ki)
            acc = acc + jnp.maximum(s, 0.0) * wi[:, h:h + 1]
        qpos, kpos = _positions(qpos0, kt * tk, tq, tk)
        score = jnp.where(kpos <= qpos, acc * (D_IDX ** -0.5 * H_IDX ** -0.5), NEG)
        sc_ref[0, kt] = score
        i = lax.bitcast_convert_type(score, jnp.int32)
        key_ref[kt] = i ^ ((i >> 31) & 0x7FFFFFFF)

    @pl.when(kt == geom.nkt - 1)
    def _():
        def body(bit, lo):
            cand = lo + jnp.left_shift(jnp.int32(1), 31 - bit)
            ge = (key_ref[...] >= cand[None]).astype(F32)
            cnt = jnp.sum(jnp.sum(ge, axis=0), axis=-1, keepdims=True)
            return jnp.where(cnt >= topk, cand, lo)

        lo = lax.fori_loop(0, 32, body, jnp.full((tq, 1), INT32_MIN, jnp.int32))
        thr = lax.bitcast_convert_type(lo ^ ((lo >> 31) & 0x7FFFFFFF), F32)
        thr_ref[0] = jnp.broadcast_to(thr, (tq, LANE))


def dsa_indexer(geom, qarr, qi_col, miscarr, misc_col, kiarr, ki_col):
    b, T, tq, tk, nkt = geom.b, geom.T, geom.tq, geom.tk, geom.nkt
    topk = min(TOPK_A, geom.L // 4)
    assert topk <= tk, "the first causal key tile alone must hold topk entries"
    return pl.pallas_call(
        functools.partial(_indexer_kernel, geom=geom, topk=topk),
        grid=(b, geom.nqt, nkt),
        in_specs=[pl.BlockSpec((1, tq, H_IDX * D_IDX), lambda i, q, k: (i, q, qi_col)),
                  pl.BlockSpec((1, tq, LANE), lambda i, q, k: (i, q, misc_col)),
                  pl.BlockSpec((1, tk, D_IDX), lambda i, q, k: (i, jnp.minimum(k, geom.q_tile(q)), ki_col))],
        out_specs=[pl.BlockSpec((1, nkt, tq, tk), lambda i, q, k: (i, 0, q, 0)),
                   pl.BlockSpec((1, tq, LANE), lambda i, q, k: (i, q, 0))],
        out_shape=[jax.ShapeDtypeStruct((b, nkt, T, tk), F32), jax.ShapeDtypeStruct((b, T, LANE), F32)],
        scratch_shapes=[pltpu.VMEM((nkt, tq, tk), jnp.int32)],
        compiler_params=_params("parallel", "parallel", "arbitrary"),
    )(qarr, miscarr, kiarr)


def _dsa_attn_kernel(q_ref, k_ref, v_ref, sc_ref, thr_ref, bias_ref, o_ref, m_ref, l_ref, acc_ref, *, geom, rep):
    tq, tk = geom.tq, geom.tk
    qt, kt = pl.program_id(1), pl.program_id(2)
    qpos0 = geom.past_len + qt * tq
    last = geom.q_tile(qt)

    @pl.when(kt == 0)
    def _():
        _softmax_init(m_ref, l_ref, acc_ref)

    @pl.when(kt <= last)
    def _():
        qpos, kpos = _positions(qpos0, kt * tk, tq, tk)
        mask = (sc_ref[0, 0] >= thr_ref[0][:, :1]) & (kpos <= qpos)

        def step(add_bias):
            for h in range(H_A):
                kv_cols = slice((h // rep) * HEAD_DIM, (h // rep + 1) * HEAD_DIM)
                s = _qk(q_ref[0, :, h * HEAD_DIM:(h + 1) * HEAD_DIM], k_ref[0, :, kv_cols]) * SCALE
                if add_bias:
                    s = s + bias_ref[h, 0]
                _softmax_step(s, mask, v_ref[0, :, kv_cols], m_ref, l_ref, acc_ref, h)

        _near_or_far((last - kt) <= 1, step)

    @pl.when(kt == geom.nkt - 1)
    def _():
        for h in range(H_A):
            o_ref[0, :, h * HEAD_DIM:(h + 1) * HEAD_DIM] = _softmax_out(l_ref, acc_ref, h).astype(o_ref.dtype)


def dsa_attention(geom, qarr, q_col, karr, k_col, v_col, scores, thr, bias):
    b, T, tq, tk, nkt = geom.b, geom.T, geom.tq, geom.tk, geom.nkt

    def kmap(col):
        return lambda i, q, k: (i, jnp.minimum(k, geom.q_tile(q)), col)

    return pl.pallas_call(
        functools.partial(_dsa_attn_kernel, geom=geom, rep=H_A // KV_A),
        grid=(b, geom.nqt, nkt),
        in_specs=[pl.BlockSpec((1, tq, H_A * HEAD_DIM), lambda i, q, k: (i, q, q_col)),
                  pl.BlockSpec((1, tk, KV_A * HEAD_DIM), kmap(k_col)),
                  pl.BlockSpec((1, tk, KV_A * HEAD_DIM), kmap(v_col)),
                  pl.BlockSpec((1, 1, tq, tk), lambda i, q, k: (i, jnp.minimum(k, geom.q_tile(q)), q, 0)),
                  pl.BlockSpec((1, tq, LANE), lambda i, q, k: (i, q, 0)),
                  _bias_spec(geom, H_A)],
        out_specs=pl.BlockSpec((1, tq, H_A * HEAD_DIM), lambda i, q, k: (i, q, 0)),
        out_shape=jax.ShapeDtypeStruct((b, T, H_A * HEAD_DIM), BF16),
        scratch_shapes=[pltpu.VMEM((H_A, tq, 1), F32), pltpu.VMEM((H_A, tq, 1), F32),
                        pltpu.VMEM((H_A, tq, HEAD_DIM), F32)],
        compiler_params=_params("parallel", "parallel", "arbitrary"),
    )(qarr, karr, karr, scores, thr, bias)


def _moba_select_kernel(q_ref, k_ref, avg_ref, sel_ref, *, geom, nblk, ksel):
    qt = pl.program_id(2)
    cur = (geom.past_len + qt * geom.tq) // MOBA_BLOCK
    kmean = jnp.dot(avg_ref[...], k_ref[0], preferred_element_type=F32)
    gs = _qk(q_ref[0], kmean.astype(BF16))
    lane = lax.broadcasted_iota(jnp.int32, gs.shape, 1)
    past = lane < cur
    g = jnp.where(past, gs, NEG)
    cnt = jnp.zeros_like(g)
    for m in range(nblk):
        col = g[:, m:m + 1]
        cnt = cnt + ((col > g) | ((col == g) & (lane > m))).astype(F32)
    sel_ref[0, 0] = (past & (cnt < ksel)).astype(F32)


def moba_select(geom, qarr, q_col, karr, k_col):
    b, T, tq = geom.b, geom.T, geom.tq
    nblk = geom.L // MOBA_BLOCK
    assert 0 < nblk <= LANE and MOBA_BLOCK % tq == 0 and geom.tk % MOBA_BLOCK == 0
    ksel = min(MOBA_TOPK, nblk)
    rows = nblk * MOBA_BLOCK
    avg = np.zeros((LANE, rows), np.float32)
    avg[np.arange(rows) // MOBA_BLOCK, np.arange(rows)] = 1.0 / MOBA_BLOCK
    return pl.pallas_call(
        functools.partial(_moba_select_kernel, geom=geom, nblk=nblk, ksel=ksel),
        grid=(b, H_C, geom.nqt),
        in_specs=[pl.BlockSpec((1, tq, HEAD_DIM), lambda i, h, q: (i, q, q_col + h)),
                  pl.BlockSpec((1, rows, HEAD_DIM), lambda i, h, q: (i, 0, k_col + h)),
                  pl.BlockSpec((LANE, rows), lambda i, h, q: (0, 0))],
        out_specs=pl.BlockSpec((1, 1, tq, LANE), lambda i, h, q: (i, h, q, 0)),
        out_shape=jax.ShapeDtypeStruct((b, H_C, T, LANE), F32),
        compiler_params=_params("parallel", "parallel", "arbitrary"),
    )(qarr, karr, jnp.asarray(avg, BF16))


def _moba_attn_kernel(q_ref, k_ref, v_ref, sel_ref, bias_ref, o_ref, m_ref, l_ref, acc_ref, *, geom):
    tq, tk = geom.tq, geom.tk
    qt, kt = pl.program_id(1), pl.program_id(2)
    qpos0 = geom.past_len + qt * tq
    cur = geom.q_tile(qt)

    @pl.when(kt == 0)
    def _():
        _softmax_init(m_ref, l_ref, acc_ref)

    @pl.when(kt <= cur)
    def _():
        qpos, kpos = _positions(qpos0, kt * tk, tq, tk)
        own_blk = qpos0 // MOBA_BLOCK
        blk = lax.broadcasted_iota(jnp.int32, (LANE, tk), 0)
        tok = lax.broadcasted_iota(jnp.int32, (LANE, tk), 1)
        expand = (blk == kt * (tk // MOBA_BLOCK) + tok // MOBA_BLOCK).astype(BF16)
        own = (kpos // MOBA_BLOCK == own_blk) & (kpos <= qpos)
        own_f = own.astype(F32)

        def block_mask(h):
            if tk == MOBA_BLOCK:
                return (_lane_column(sel_ref[0, h], kt) + own_f) > 0.5
            return (jnp.dot(sel_ref[0, h].astype(BF16), expand, preferred_element_type=F32) + own_f) > 0.5

        def step(add_bias):
            for h in range(H_C):
                cols = slice(h * HEAD_DIM, (h + 1) * HEAD_DIM)
                s = _qk(q_ref[0, :, cols], k_ref[0, :, cols]) * SCALE
                if add_bias:
                    s = s + bias_ref[h, 0]
                _softmax_step(s, block_mask(h), v_ref[0, :, cols], m_ref, l_ref, acc_ref, h)

        _near_or_far((cur - kt) <= 1, step)

    @pl.when(kt == geom.nkt - 1)
    def _():
        for h in range(H_C):
            o_ref[0, :, h * HEAD_DIM:(h + 1) * HEAD_DIM] = _softmax_out(l_ref, acc_ref, h).astype(o_ref.dtype)


def moba_attention(geom, qarr, q_col, karr, k_col, v_col, sel, bias):
    b, T, tq, tk, nkt = geom.b, geom.T, geom.tq, geom.tk, geom.nkt
    width = H_C * HEAD_DIM

    def kmap(col):
        return lambda i, q, k: (i, jnp.minimum(k, geom.q_tile(q)), col)

    return pl.pallas_call(
        functools.partial(_moba_attn_kernel, geom=geom),
        grid=(b, geom.nqt, nkt),
        in_specs=[pl.BlockSpec((1, tq, width), lambda i, q, k: (i, q, q_col)),
                  pl.BlockSpec((1, tk, width), kmap(k_col)),
                  pl.BlockSpec((1, tk, width), kmap(v_col)),
                  pl.BlockSpec((1, H_C, tq, LANE), lambda i, q, k: (i, 0, q, 0)),
                  _bias_spec(geom, H_C)],
        out_specs=pl.BlockSpec((1, tq, width), lambda i, q, k: (i, q, 0)),
        out_shape=jax.ShapeDtypeStruct((b, T, width), BF16),
        scratch_shapes=[pltpu.VMEM((H_C, tq, 1), F32), pltpu.VMEM((H_C, tq, 1), F32),
                        pltpu.VMEM((H_C, tq, HEAD_DIM), F32)],
        compiler_params=_params("parallel", "parallel", "arbitrary"),
    )(qarr, karr, karr, sel, bias)


CMP_ROW = CMP_STRIDE * 2 * KV_B * HEAD_DIM


def _compress_kernel(x_ref, xn_ref, wlo_ref, whi_ref, c_ref, w2_ref, g_ref, o_ref, *, tr):
    x = x_ref[0]
    lo = jnp.dot(x, wlo_ref[...], preferred_element_type=F32)
    hi = jnp.dot(x, whi_ref[...], preferred_element_type=F32)
    hi_next = jnp.dot(xn_ref[0], whi_ref[...], preferred_element_type=F32)
    row = lax.broadcasted_iota(jnp.int32, hi.shape, 0)
    hi = jnp.where(row == tr - 1, hi_next[0:1], pltpu.roll(hi, tr - 1, 0))
    act = jax.nn.gelu(lo + hi + c_ref[...]).astype(BF16)
    for c in range(2 * KV_B):
        cols = slice(c * HEAD_DIM, (c + 1) * HEAD_DIM)
        y = jnp.dot(act[:, cols], w2_ref[c // KV_B], preferred_element_type=F32)
        if c < KV_B:
            y = y * lax.rsqrt(jnp.mean(y * y, axis=-1, keepdims=True) + EPS) * g_ref[...]
        o_ref[0, :, cols] = y.astype(o_ref.dtype)


def nsa_compress(x, wlo, whi, const, w2, gain):
    b, R, _ = x.shape
    tr = _tile(R, 128, BF16_SUBLANE)
    nxt = tr // BF16_SUBLANE
    n_sub16 = R // BF16_SUBLANE
    C = 2 * KV_B * HEAD_DIM
    return pl.pallas_call(
        functools.partial(_compress_kernel, tr=tr),
        grid=(b, R // tr),
        in_specs=[pl.BlockSpec((1, tr, CMP_ROW), lambda i, r: (i, r, 0)),
                  pl.BlockSpec((1, BF16_SUBLANE, CMP_ROW), lambda i, r: (i, jnp.minimum((r + 1) * nxt, n_sub16 - 1), 0)),
                  pl.BlockSpec((CMP_ROW, C), lambda i, r: (0, 0)),
                  pl.BlockSpec((CMP_ROW, C), lambda i, r: (0, 0)),
                  pl.BlockSpec((1, C), lambda i, r: (0, 0)),
                  pl.BlockSpec((2, HEAD_DIM, HEAD_DIM), lambda i, r: (0, 0, 0)),
                  pl.BlockSpec((1, HEAD_DIM), lambda i, r: (0, 0))],
        out_specs=pl.BlockSpec((1, tr, C), lambda i, r: (i, r, 0)),
        out_shape=jax.ShapeDtypeStruct((b, R, C), BF16),
        compiler_params=_params("parallel", "arbitrary"),
    )(x, x, wlo, whi, const, w2, gain)


def _prep_compress(cmp_pos, cmp_w1, cmp_w2, kn_cmp):
    r = CMP_LEN // CMP_STRIDE
    assert r == 2
    w1 = cmp_w1.reshape(2, r, CMP_STRIDE, HEAD_DIM, HEAD_DIM)
    kv_of_col = np.repeat(np.arange(2), KV_B)
    eye = jnp.eye(2 * KV_B, dtype=cmp_w1.dtype)

    def expand(half):
        w = w1[kv_of_col, half]
        return jnp.einsum('crde,cf->rcdfe', w, eye).reshape(CMP_ROW, 2 * KV_B * HEAD_DIM).astype(BF16)

    const = jnp.einsum('krd,krde->ke', cmp_pos.reshape(2, CMP_LEN, HEAD_DIM),
                       cmp_w1.reshape(2, CMP_LEN, HEAD_DIM, HEAD_DIM), precision=lax.Precision.HIGHEST)
    const = const[kv_of_col].reshape(1, -1).astype(F32)
    return dict(wlo=expand(0), whi=expand(1), const=const, w2=cmp_w2.astype(BF16), gain=kn_cmp.reshape(1, -1).astype(F32))


def _nsa_cmp_kernel(q_ref, kc_ref, vc_ref, bias_ref, ov_ref, oc_ref, sel_ref, *, geom, rep, n_slc, nsel):
    tq = geom.tq
    qt = pl.program_id(2)
    qpos0 = geom.past_len + qt * tq
    kc, vc, ov = kc_ref[0], vc_ref[0], ov_ref[...]
    ncp, nsp = ov.shape
    qpos, n_idx = _positions(qpos0, 0, tq, ncp)
    valid = (n_idx * CMP_STRIDE + (CMP_LEN - 1)) <= qpos
    imp = jnp.zeros((tq, nsp), F32)
    for r in range(rep):
        cols = slice(r * HEAD_DIM, (r + 1) * HEAD_DIM)
        s = jnp.where(valid, _qk(q_ref[0, :, cols], kc) * SCALE + bias_ref[r], NEG)
        e = jnp.where(valid, jnp.exp(s - jnp.max(s, axis=-1, keepdims=True)), 0.0)
        p = (e / jnp.maximum(jnp.sum(e, axis=-1, keepdims=True), 1e-30)).astype(BF16)
        oc_ref[0, :, cols] = jnp.dot(p, vc, preferred_element_type=F32)
        imp = imp + jnp.dot(p, ov, preferred_element_type=F32)
    spos, jj = _positions(qpos0, 0, tq, nsp)
    cur = spos // SLC_LEN
    forced = (jj == 0) | (jj == cur) | (jj == cur - 1)
    imp = jnp.where(jj > cur, NEG, jnp.where(forced, -NEG, imp))
    cnt = jnp.zeros_like(imp)
    for s_blk in range(n_slc):
        col = imp[:, s_blk:s_blk + 1]
        cnt = cnt + ((col > imp) | ((col == imp) & (jj > s_blk))).astype(F32)
    sel_ref[0, 0] = ((cnt < nsel) & (jj < n_slc)).astype(F32)


def nsa_compressed(geom, qarr, q_col, cmpkv, bias_cmp):
    b, T, tq = geom.b, geom.T, geom.tq
    rep = H_B // KV_B
    ncp = cmpkv.shape[1]
    n_slc = -(-geom.L // SLC_LEN)
    nsel = min(N_SLC, n_slc)
    nsp = _round_up(n_slc, LANE)
    n_cmp = geom.L // CMP_STRIDE - CMP_LEN // CMP_STRIDE + 1
    cstart = np.arange(ncp) * CMP_STRIDE
    cend = cstart + CMP_LEN - 1
    sstart = np.arange(nsp) * SLC_LEN
    ov = ((cstart[:, None] < sstart[None, :] + SLC_LEN) & (cend[:, None] >= sstart[None, :])
          & (np.arange(ncp)[:, None] < n_cmp) & (np.arange(nsp)[None, :] < n_slc)).astype(np.float32)
    return pl.pallas_call(
        functools.partial(_nsa_cmp_kernel, geom=geom, rep=rep, n_slc=n_slc, nsel=nsel),
        grid=(b, KV_B, geom.nqt),
        in_specs=[pl.BlockSpec((1, tq, rep * HEAD_DIM), lambda i, g, q: (i, q, q_col + g)),
                  pl.BlockSpec((1, ncp, HEAD_DIM), lambda i, g, q: (i, 0, g)),
                  pl.BlockSpec((1, ncp, HEAD_DIM), lambda i, g, q: (i, 0, KV_B + g)),
                  pl.BlockSpec((rep, tq, ncp), lambda i, g, q: (g, q, 0)),
                  pl.BlockSpec((ncp, nsp), lambda i, g, q: (0, 0))],
        out_specs=[pl.BlockSpec((1, tq, rep * HEAD_DIM), lambda i, g, q: (i, q, g)),
                   pl.BlockSpec((1, 1, tq, nsp), lambda i, g, q: (i, g, q, 0))],
        out_shape=[jax.ShapeDtypeStruct((b, T, H_B * HEAD_DIM), F32), jax.ShapeDtypeStruct((b, KV_B, T, nsp), F32)],
        compiler_params=_params("parallel", "parallel", "arbitrary"),
    )(qarr, cmpkv, cmpkv, bias_cmp, jnp.asarray(ov, BF16))


class _Window:
    def __init__(self, arr, k_col, v_col, tw, n_back, tile0, pos0, bias, bias_always):
        self.arr, self.k_col, self.v_col, self.tw, self.n_back = arr, k_col, v_col, tw, n_back
        self.tile0, self.pos0, self.bias, self.bias_always = tile0, pos0, bias, bias_always


def _nsa_attn_kernel(q_ref, ks_ref, vs_ref, kw_ref, vw_ref, sel_ref, bias_ref, wbias_ref, oc_ref, misc_ref, o_ref,
                     m_ref, l_ref, acc_ref, *, geom, rep, win):
    tq, tk = geom.tq, geom.tk
    qt, kt = pl.program_id(1), pl.program_id(2)
    qpos0 = geom.past_len + qt * tq
    last = geom.q_tile(qt)

    @pl.when(kt == 0)
    def _():
        _softmax_init(m_ref, l_ref, acc_ref)

    def branch(g, k_ref, v_ref, mask, slot0, b_ref, add_bias):
        kv_cols = slice(g * HEAD_DIM, (g + 1) * HEAD_DIM)
        for h in range(g * rep, (g + 1) * rep):
            s = _qk(q_ref[0, :, h * HEAD_DIM:(h + 1) * HEAD_DIM], k_ref[0, :, kv_cols]) * SCALE
            if add_bias:
                s = s + b_ref[h, 0]
            _softmax_step(s, mask, v_ref[0, :, kv_cols], m_ref, l_ref, acc_ref, slot0 + h)

    near = (last - kt) <= 1

    @pl.when(kt <= last)
    def _():
        qpos, kpos = _positions(qpos0, kt * tk, tq, tk)
        nsp = sel_ref.shape[-1]
        blk = lax.broadcasted_iota(jnp.int32, (nsp, tk), 0)
        tok = lax.broadcasted_iota(jnp.int32, (nsp, tk), 1)
        expand = (blk == kt * (tk // SLC_LEN) + tok // SLC_LEN).astype(BF16)

        def step(add_bias):
            for g in range(KV_B):
                chosen = jnp.dot(sel_ref[0, g].astype(BF16), expand, preferred_element_type=F32) > 0.5
                branch(g, ks_ref, vs_ref, chosen & (kpos <= qpos), 0, bias_ref, add_bias)

        _near_or_far(near, step)

    @pl.when((kt <= last) & (kt >= last - win.n_back))
    def _():
        qpos, kpos = _positions(qpos0, win.pos0 + (kt - win.tile0) * win.tw, tq, win.tw)
        dist = qpos - kpos

        def step(add_bias):
            for g in range(KV_B):
                branch(g, kw_ref, vw_ref, (dist >= 0) & (dist < WINDOW), H_B, wbias_ref, add_bias)

        if win.bias_always:
            step(True)
        else:
            _near_or_far(near, step)

    @pl.when(kt == geom.nkt - 1)
    def _():
        gates = jax.nn.sigmoid(misc_ref[0])
        for h in range(H_B):
            cols = slice(h * HEAD_DIM, (h + 1) * HEAD_DIM)
            base = H_IDX + h * N_BRANCH
            o = (gates[:, base:base + 1] * oc_ref[0, :, cols]
                 + gates[:, base + 1:base + 2] * _softmax_out(l_ref, acc_ref, h)
                 + gates[:, base + 2:base + 3] * _softmax_out(l_ref, acc_ref, H_B + h))
            o_ref[0, :, cols] = o.astype(o_ref.dtype)


def nsa_attention(geom, qarr, q_col, sarr, ks_col, vs_col, win, sel, bias, oc, miscarr, misc_col):
    b, T, tq, tk, nkt = geom.b, geom.T, geom.tq, geom.tk, geom.nkt
    nsp = sel.shape[-1]
    tw = win.tw
    n_wt = win.arr.shape[1] // tw
    kv_width = KV_B * HEAD_DIM

    def smap(col):
        return lambda i, q, k: (i, jnp.minimum(k, geom.q_tile(q)), col)

    def wtile(q, k):
        last = geom.q_tile(q)
        return jnp.clip(jnp.clip(k, last - win.n_back, last) - win.tile0, 0, n_wt - 1)

    def wmap(col):
        return lambda i, q, k: (i, wtile(q, k), col)

    if win.bias_always:
        assert win.bias.shape[1] == 1 and n_wt == 1
        wbias_spec = pl.BlockSpec((H_B, 1, tq, tw), lambda i, q, k: (0, 0, 0, 0))
    else:
        wbias_spec = _bias_spec(geom, H_B)

    return pl.pallas_call(
        functools.partial(_nsa_attn_kernel, geom=geom, rep=H_B // KV_B, win=win),
        grid=(b, geom.nqt, nkt),
        in_specs=[pl.BlockSpec((1, tq, H_B * HEAD_DIM), lambda i, q, k: (i, q, q_col)),
                  pl.BlockSpec((1, tk, kv_width), smap(ks_col)),
                  pl.BlockSpec((1, tk, kv_width), smap(vs_col)),
                  pl.BlockSpec((1, tw, kv_width), wmap(win.k_col)),
                  pl.BlockSpec((1, tw, kv_width), wmap(win.v_col)),
                  pl.BlockSpec((1, KV_B, tq, nsp), lambda i, q, k: (i, 0, q, 0)),
                  _bias_spec(geom, H_B),
                  wbias_spec,
                  pl.BlockSpec((1, tq, H_B * HEAD_DIM), lambda i, q, k: (i, q, 0)),
                  pl.BlockSpec((1, tq, LANE), lambda i, q, k: (i, q, misc_col))],
        out_specs=pl.BlockSpec((1, tq, H_B * HEAD_DIM), lambda i, q, k: (i, q, 0)),
        out_shape=jax.ShapeDtypeStruct((b, T, H_B * HEAD_DIM), BF16),
        scratch_shapes=[pltpu.VMEM((2 * H_B, tq, 1), F32), pltpu.VMEM((2 * H_B, tq, 1), F32),
                        pltpu.VMEM((2 * H_B, tq, HEAD_DIM), F32)],
        compiler_params=_params("parallel", "parallel", "arbitrary"),
    )(qarr, sarr, sarr, win.arr, win.arr, sel, bias, win.bias, oc, miscarr)


_SEG = {}
_off = 0
for _name, _w in (("qa", H_A * HEAD_DIM), ("qb", H_B * HEAD_DIM), ("qi", H_IDX * D_IDX), ("qc", H_C * HEAD_DIM),
                  ("kvc", 2 * H_C * HEAD_DIM), ("kva", 2 * KV_A * HEAD_DIM), ("cmp", 2 * KV_B * HEAD_DIM),
                  ("slc", 2 * KV_B * HEAD_DIM), ("win", 2 * KV_B * HEAD_DIM), ("ki", D_IDX), ("misc", LANE)):
    _SEG[_name] = (_off, _off + _w)
    _off += _w
IN_COLS_ALIGNED = _off


def _col(name, width=LANE):
    assert _SEG[name][0] % width == 0
    return _SEG[name][0] // width


def _prep_in_proj(w_in, qn_a, kn_a, qn_b, kn_b, qn_c, kn_c):
    sizes = (H_A * HEAD_DIM, KV_A * HEAD_DIM, KV_A * HEAD_DIM, H_IDX * D_IDX, H_IDX, D_IDX,
             H_B * HEAD_DIM, N_BRANCH * H_B) + (KV_B * HEAD_DIM,) * 6 + (H_C * HEAD_DIM,) * 3
    splits = np.cumsum(sizes)[:-1].tolist()
    (qa, ka, va, qi, wi, ki, q_b, g_b, kbc, vbc, kbs, vbs, kbw, vbw, qc, kc, vc) = jnp.split(w_in, splits, axis=-1)
    D = w_in.shape[0]
    misc = jnp.concatenate([wi, g_b, jnp.zeros((D, LANE - H_IDX - N_BRANCH * H_B), w_in.dtype)], axis=-1)
    n_pad = _round_up(IN_COLS_ALIGNED, 512) - IN_COLS_ALIGNED
    w = jnp.concatenate([qa, q_b, qi, qc, kc, vc, ka, va, kbc, vbc, kbs, vbs, kbw, vbw, ki, misc,
                         jnp.zeros((D, n_pad), w_in.dtype)], axis=-1).astype(BF16)
    one = jnp.ones((HEAD_DIM,), F32)

    def rep(g, n):
        return jnp.tile(g.astype(F32), n)

    gains = jnp.concatenate([
        rep(qn_a, H_A), rep(qn_b, H_B), rep(one, H_IDX), rep(qn_c, H_C), rep(kn_c, H_C), rep(one, H_C),
        rep(kn_a, KV_A), rep(one, KV_A), rep(one, 2 * KV_B), rep(kn_b[1], KV_B), rep(one, KV_B),
        rep(kn_b[2], KV_B), rep(one, KV_B), one, one, jnp.ones((n_pad,), F32)]).reshape(1, -1)
    flags = np.concatenate([
        np.ones(H_A), np.ones(H_B), np.zeros(H_IDX), np.ones(H_C), np.ones(H_C), np.zeros(H_C),
        np.ones(KV_A), np.zeros(KV_A), np.zeros(2 * KV_B), np.ones(KV_B), np.zeros(KV_B),
        np.ones(KV_B), np.zeros(KV_B), np.zeros(1), np.zeros(1), np.zeros(n_pad // LANE)]).astype(np.int32)
    return w, gains, jnp.asarray(flags)


def _prep_layer(i, ffn1_norm, ffn1_wg, ffn1_wu, ffn1_wd, mix_norm, w_in, qn_a, kn_a, qn_b, kn_b, cmp_pos, cmp_w1,
                cmp_w2, qn_c, kn_c, wb_a, wb_b, wb_c, w_gate, b_gate, w_out, ffn2_norm, ffn2_wg, ffn2_wu, ffn2_wd,
                ple_norm, ple_wg, ple_wp):
    w_in_p, gains, flags = _prep_in_proj(w_in[i], qn_a[i], kn_a[i], qn_b[i], kn_b[i], qn_c[i], kn_c[i])
    return dict(
        layer=i, f1n=ffn1_norm[i], f1g=ffn1_wg, f1u=ffn1_wu, f1d=ffn1_wd[i].astype(BF16), mn=mix_norm[i],
        w_in=w_in_p, gains=gains, flags=flags,
        cmp=_prep_compress(cmp_pos[i], cmp_w1[i], cmp_w2[i], kn_b[i, 0]),
        wb_a=wb_a[i].astype(BF16), wb_b=wb_b[i].astype(BF16), wb_c=wb_c[i].astype(BF16),
        w_gate=w_gate[i].astype(BF16), b_gate=b_gate[i].reshape(1, -1), w_out=w_out,
        f2n=ffn2_norm[i], f2g=ffn2_wg, f2u=ffn2_wu, f2d=ffn2_wd[i].astype(BF16),
        pn=ple_norm[i], pwg=ple_wg, pwp=ple_wp[i].astype(BF16))


def _layer(x, p, geom, layer, caches, W, bias):
    b, T = geom.b, geom.T
    M = b * T
    h = ffn_down(ffn_up(rmsnorm_bf16(x, W["f1n"]), W["f1g"], W["f1u"], W["layer"]), W["f1d"], x)
    u = rmsnorm_bf16(h, W["mn"])
    proj32, proj16 = in_proj(u, W["w_in"], W["gains"], W["flags"])
    Np = proj32.shape[1]
    p32 = proj32.reshape(b, T, Np)
    p16 = proj16.reshape(b, T, Np)

    def seg32(name):
        lo, hi = _SEG[name]
        return p32[:, :, lo:hi]

    def seg16(name):
        lo, hi = _SEG[name]
        return p16[:, :, lo:hi]

    win_new = seg32("win")
    if caches is None:
        assert T >= WINDOW
        a_kv, a_ki, slc, c_kv = (p16,) * 4
        a_k, ki_c = _col("kva", KV_A * HEAD_DIM), _col("ki")
        s_k = _col("slc", KV_B * HEAD_DIM)
        c_k = _col("kvc", H_C * HEAD_DIM)
        w_k = _col("win", KV_B * HEAD_DIM)
        win = _Window(p16, w_k, w_k + 1, geom.tk, WINDOW // geom.tk, 0, 0, bias["b"], False)
        cmp_rows = seg16("cmp")
        win_state = win_new[:, T - WINDOW:]
    else:
        cache_a_kv, cache_a_kidx, cache_b_cmp, cache_b_slc, win_past, cache_c_kv, page_table = caches
        a_kv = assemble_cache(cache_a_kv, layer, page_table, seg16("kva"), geom.tk)
        a_ki = assemble_cache(cache_a_kidx, layer, page_table, seg16("ki"), geom.tk)
        cmp_rows = assemble_cache(cache_b_cmp, layer, page_table, seg16("cmp"), geom.tk)
        slc = assemble_cache(cache_b_slc, layer, page_table, seg16("slc"), geom.tk)
        c_kv = assemble_cache(cache_c_kv, layer, page_table, seg16("kvc"), geom.tk)
        a_k, ki_c, s_k, c_k = 0, 0, 0, 0
        n_buf = win_past.shape[2]
        win_past = win_past[layer].reshape(b, n_buf, -1)
        warr = jnp.concatenate([win_past.astype(BF16), seg16("win"),
                                jnp.zeros((b, KEY_TILE - T, win_past.shape[-1]), BF16)], axis=1)
        win = _Window(warr, 0, 1, warr.shape[1], 0, geom.q_tile(0), geom.past_len - n_buf, bias["b_win"], True)
        win_state = jnp.concatenate([win_past, win_new], axis=1)[:, -WINDOW:]

    scores, thr = dsa_indexer(geom, p16, _col("qi", H_IDX * D_IDX), p32, _col("misc"), a_ki, ki_c)
    o_a = dsa_attention(geom, p16, _col("qa", H_A * HEAD_DIM), a_kv, a_k, a_k + 1, scores, thr, bias["a"])
    cmpkv = nsa_compress(cmp_rows.reshape(b, -1, CMP_ROW), **W["cmp"])
    qpos = geom.past_len + jnp.arange(T)
    cend = jnp.arange(cmpkv.shape[1]) * CMP_STRIDE + CMP_LEN - 1
    bias_cmp = _bias_lookup(bias["tab_b"], qpos[:, None] - cend[None, :])
    oc, sel_b = nsa_compressed(geom, p16, _col("qb", 4 * HEAD_DIM), cmpkv, bias_cmp)
    o_b = nsa_attention(geom, p16, _col("qb", H_B * HEAD_DIM), slc, s_k, s_k + 1, win, sel_b, bias["b"], oc, p32,
                        _col("misc"))
    sel_c = moba_select(geom, p16, _col("qc"), c_kv, c_k * H_C)
    o_c = moba_attention(geom, p16, _col("qc", H_C * HEAD_DIM), c_kv, c_k, c_k + 1, sel_c, bias["c"])

    merged = gate_merge(u, o_a.reshape(M, -1), o_b.reshape(M, -1), o_c.reshape(M, -1),
                        W["w_gate"], W["b_gate"], W["wb_a"], W["wb_b"], W["wb_c"])
    h = out_proj(merged, W["w_out"], h, W["layer"])
    h = ffn_down(ffn_up(rmsnorm_bf16(h, W["f2n"]), W["f2g"], W["f2u"], W["layer"]), W["f2d"], h)
    h = ple(rmsnorm_bf16(h, W["pn"]), W["pwg"], p.astype(BF16), W["pwp"], h, W["layer"])
    state = (seg32("kva").reshape(b, T, 2, KV_A, HEAD_DIM), seg32("ki"), seg32("cmp").reshape(b, T, 2, KV_B, HEAD_DIM),
             seg32("slc").reshape(b, T, 2, KV_B, HEAD_DIM), win_state.reshape(b, WINDOW, 2, KV_B, HEAD_DIM),
             seg32("kvc").reshape(b, T, 2, H_C, HEAD_DIM))
    return h, state


def kernel(x_prompt, x_sample, cache_a_kv, cache_a_kidx, cache_b_cmp_kv, cache_b_slc_kv, state_b_win_kv, cache_c_kv, page_table, p_prompt, p_sample, rel_bias, ffn1_norm, ffn1_wg, ffn1_wu, ffn1_wd, mix_norm, w_in, qn_a, kn_a, qn_b, kn_b, cmp_pos, cmp_w1, cmp_w2, qn_c, kn_c, wb_a, wb_b, wb_c, w_gate, b_gate, w_out, ffn2_norm, ffn2_wg, ffn2_wu, ffn2_wd, ple_norm, ple_wg, ple_wp):
    depth = ffn1_norm.shape[0]
    bp, Tp, D = x_prompt.shape
    bs, Ts, _ = x_sample.shape
    geom_p = _Geom(bp, Tp, 0)
    geom_s = _Geom(bs, Ts, page_table.shape[1] * PAGE_SIZE)
    n_buf = state_b_win_kv.shape[2]
    assert n_buf >= WINDOW and Ts <= KEY_TILE
    tab_a, tab_b, tab_c = rel_bias[:, :H_A], rel_bias[:, H_A:H_A + H_B], rel_bias[:, H_A + H_B:]

    def tile_bias(geom):
        near = (0, geom.tk)
        return dict(a=_toeplitz_bias(tab_a, geom.tq, geom.tk, near), b=_toeplitz_bias(tab_b, geom.tq, geom.tk, near),
                    c=_toeplitz_bias(tab_c, geom.tq, geom.tk, near), tab_b=tab_b)

    bias_p = tile_bias(geom_p)
    bias_s = tile_bias(geom_s)
    bias_s["b_win"] = _toeplitz_bias(tab_b, geom_s.tq, n_buf + KEY_TILE, (n_buf,))
    caches = (cache_a_kv, cache_a_kidx, cache_b_cmp_kv, cache_b_slc_kv, state_b_win_kv, cache_c_kv, page_table)

    y_p = x_prompt.reshape(bp * Tp, D)
    y_s = x_sample.reshape(bs * Ts, D)
    sp_list, ss_list = [], []
    for i in range(depth):
        W = _prep_layer(i, ffn1_norm, ffn1_wg, ffn1_wu, ffn1_wd, mix_norm, w_in, qn_a, kn_a, qn_b, kn_b, cmp_pos,
                        cmp_w1, cmp_w2, qn_c, kn_c, wb_a, wb_b, wb_c, w_gate, b_gate, w_out, ffn2_norm, ffn2_wg,
                        ffn2_wu, ffn2_wd, ple_norm, ple_wg, ple_wp)
        y_p, sp = _layer(y_p, p_prompt[i].reshape(bp * Tp, -1), geom_p, i, None, W, bias_p)
        y_s, ss = _layer(y_s, p_sample[i].reshape(bs * Ts, -1), geom_s, i, caches, W, bias_s)
        sp_list.append(sp)
        ss_list.append(ss)

    outs = [y_p.reshape(bp, Tp, D), y_s.reshape(bs, Ts, D)]
    for j in range(6):
        outs.append(jnp.stack([s[j] for s in sp_list]))
        outs.append(jnp.stack([s[j] for s in ss_list]))
    return tuple(outs)
```

```python
import functools
import math

import numpy as np
import jax
import jax.numpy as jnp
from jax import lax
from jax.experimental import pallas as pl
from jax.experimental.pallas import tpu as pltpu

PAGE_SIZE = 128
HEAD_DIM = 128
H_A = 8
KV_A = 2
H_IDX = 16
D_IDX = 128
TOPK_A = 256
H_B = 8
KV_B = 2
CMP_LEN = 32
CMP_STRIDE = 16
SLC_LEN = 64
N_SLC = 16
WINDOW = 512
H_C = 8
MOBA_BLOCK = 256
MOBA_TOPK = 3
N_BUCKETS = 32
MAX_DISTANCE = 128
N_BRANCH = 3
EPS = 1e-6
NEG = -1e30

LANE = 128
BF16_SUBLANE = 16
V7X_VMEM_LIMIT = 56 * 1024 * 1024

F32 = jnp.float32
BF16 = jnp.bfloat16


def _tile(n, target, align):
    best = None
    for t in range(align, min(n, target) + 1, align):
        if n % t == 0:
            best = t
    return best if best is not None else n


def _round_up(n, m):
    return -(-n // m) * m


def _params(*sem):
    return pltpu.CompilerParams(dimension_semantics=sem, vmem_limit_bytes=V7X_VMEM_LIMIT)


def _rmsnorm_kernel(x_ref, g_ref, o_ref):
    x = x_ref[...]
    y = x * lax.rsqrt(jnp.mean(x * x, axis=-1, keepdims=True) + EPS)
    o_ref[...] = (y * g_ref[...]).astype(o_ref.dtype)


def rmsnorm_bf16(x, g):
    M, D = x.shape
    tm = _tile(M, 256, BF16_SUBLANE)
    return pl.pallas_call(
        _rmsnorm_kernel,
        grid=(M // tm,),
        in_specs=[pl.BlockSpec((tm, D), lambda i: (i, 0)), pl.BlockSpec((1, D), lambda i: (0, 0))],
        out_specs=pl.BlockSpec((tm, D), lambda i: (i, 0)),
        out_shape=jax.ShapeDtypeStruct((M, D), BF16),
        compiler_params=_params("parallel"),
    )(x, g.reshape(1, D))


def _ffn_up_kernel(a_ref, wg_ref, wu_ref, o_ref):
    a = a_ref[...]
    g = jnp.dot(a, wg_ref[...].astype(BF16), preferred_element_type=F32)
    u = jnp.dot(a, wu_ref[...].astype(BF16), preferred_element_type=F32)
    o_ref[...] = (g * jax.nn.sigmoid(g) * u).astype(o_ref.dtype)


def ffn_up(a, wg, wu, layer):
    M, D = a.shape
    F = wg.shape[2]
    tm = _tile(M, 1024, BF16_SUBLANE)
    tn = _tile(F, 256, LANE)
    return pl.pallas_call(
        _ffn_up_kernel,
        grid=(M // tm, F // tn),
        in_specs=[pl.BlockSpec((tm, D), lambda i, j: (i, 0)),
                  pl.BlockSpec((None, D, tn), lambda i, j: (layer, 0, j)),
                  pl.BlockSpec((None, D, tn), lambda i, j: (layer, 0, j))],
        out_specs=pl.BlockSpec((tm, tn), lambda i, j: (i, j)),
        out_shape=jax.ShapeDtypeStruct((M, F), BF16),
        compiler_params=_params("parallel", "arbitrary"),
    )(a, wg, wu)


def _ffn_down_kernel(a_ref, w_ref, x_ref, o_ref, acc_ref):
    k = pl.program_id(2)

    @pl.when(k == 0)
    def _():
        acc_ref[...] = jnp.zeros_like(acc_ref)

    acc_ref[...] += jnp.dot(a_ref[...], w_ref[...], preferred_element_type=F32)

    @pl.when(k == pl.num_programs(2) - 1)
    def _():
        o_ref[...] = x_ref[...] + 0.5 * acc_ref[...]


def ffn_down(a, w, x):
    M, F = a.shape
    D = w.shape[1]
    tm = _tile(M, 1024, BF16_SUBLANE)
    tn = _tile(D, 512, LANE)
    tk = _tile(F, 5504, LANE)
    return pl.pallas_call(
        _ffn_down_kernel,
        grid=(M // tm, D // tn, F // tk),
        in_specs=[pl.BlockSpec((tm, tk), lambda i, j, k: (i, k)),
                  pl.BlockSpec((tk, tn), lambda i, j, k: (k, j)),
                  pl.BlockSpec((tm, tn), lambda i, j, k: (i, j))],
        out_specs=pl.BlockSpec((tm, tn), lambda i, j, k: (i, j)),
        out_shape=jax.ShapeDtypeStruct((M, D), F32),
        scratch_shapes=[pltpu.VMEM((tm, tn), F32)],
        compiler_params=_params("parallel", "arbitrary", "arbitrary"),
    )(a, w, x)


def _w_in_kernel(flags_ref, a_ref, w_ref, g_ref, o_ref, o16_ref, *, heads_per_tile):
    j = pl.program_id(1)
    acc = jnp.dot(a_ref[...], w_ref[...], preferred_element_type=F32)
    for c in range(heads_per_tile):
        cols = slice(c * HEAD_DIM, (c + 1) * HEAD_DIM)
        y = acc[:, cols]
        yn = y * lax.rsqrt(jnp.mean(y * y, axis=-1, keepdims=True) + EPS) * g_ref[:, cols]
        y = jnp.where(flags_ref[j * heads_per_tile + c] > 0, yn, y)
        o_ref[:, cols] = y
        o16_ref[:, cols] = y.astype(o16_ref.dtype)


def in_proj(a, w, gains, flags):
    M, D = a.shape
    N = w.shape[1]
    tm = _tile(M, 1024, BF16_SUBLANE)
    tn = _tile(N, 512, LANE)
    grid_spec = pltpu.PrefetchScalarGridSpec(
        num_scalar_prefetch=1,
        grid=(M // tm, N // tn),
        in_specs=[pl.BlockSpec((tm, D), lambda i, j, f: (i, 0)),
                  pl.BlockSpec((D, tn), lambda i, j, f: (0, j)),
                  pl.BlockSpec((1, tn), lambda i, j, f: (0, j))],
        out_specs=[pl.BlockSpec((tm, tn), lambda i, j, f: (i, j)), pl.BlockSpec((tm, tn), lambda i, j, f: (i, j))],
    )
    return pl.pallas_call(
        functools.partial(_w_in_kernel, heads_per_tile=tn // HEAD_DIM),
        grid_spec=grid_spec,
        out_shape=[jax.ShapeDtypeStruct((M, N), F32), jax.ShapeDtypeStruct((M, N), BF16)],
        compiler_params=_params("parallel", "arbitrary"),
    )(flags, a, w, gains)


def _gate_merge_kernel(u_ref, oa_ref, ob_ref, oc_ref, wg0_ref, wg1_ref, wg2_ref, bg0_ref, bg1_ref, bg2_ref,
                       wba_ref, wbb_ref, wbc_ref, o_ref):
    u = u_ref[...]
    out = None
    for o_r, wg_r, bg_r, wb_r in ((oa_ref, wg0_ref, bg0_ref, wba_ref),
                                  (ob_ref, wg1_ref, bg1_ref, wbb_ref),
                                  (oc_ref, wg2_ref, bg2_ref, wbc_ref)):
        gate = jax.nn.sigmoid(jnp.dot(u, wg_r[...], preferred_element_type=F32) + bg_r[...])
        branch = jnp.dot(o_r[...], wb_r[...], preferred_element_type=F32)
        out = gate * branch if out is None else out + gate * branch
    o_ref[...] = out.astype(o_ref.dtype)


def gate_merge(u, o_a, o_b, o_c, w_gate, b_gate, wb_a, wb_b, wb_c):
    M, D = u.shape
    Ho = o_a.shape[1]
    tm = _tile(M, 1024, BF16_SUBLANE)
    tn = _tile(D, 256, LANE)
    nj = D // tn
    a_spec = pl.BlockSpec((tm, D), lambda i, j: (i, 0))
    o_spec = pl.BlockSpec((tm, Ho), lambda i, j: (i, 0))
    wb_spec = pl.BlockSpec((Ho, tn), lambda i, j: (0, j))

    def branch_spec(rows, br):
        return pl.BlockSpec((rows, tn), lambda i, j: (0, br * nj + j))

    return pl.pallas_call(
        _gate_merge_kernel,
        grid=(M // tm, nj),
        in_specs=[a_spec, o_spec, o_spec, o_spec,
                  branch_spec(D, 0), branch_spec(D, 1), branch_spec(D, 2),
                  branch_spec(1, 0), branch_spec(1, 1), branch_spec(1, 2),
                  wb_spec, wb_spec, wb_spec],
        out_specs=pl.BlockSpec((tm, tn), lambda i, j: (i, j)),
        out_shape=jax.ShapeDtypeStruct((M, D), BF16),
        compiler_params=_params("parallel", "arbitrary"),
    )(u, o_a, o_b, o_c, w_gate, w_gate, w_gate, b_gate, b_gate, b_gate, wb_a, wb_b, wb_c)


def _out_proj_kernel(a_ref, w_ref, x_ref, o_ref):
    o_ref[...] = x_ref[...] + jnp.dot(a_ref[...], w_ref[...].astype(BF16), preferred_element_type=F32)


def out_proj(a, w, x, layer):
    M, K = a.shape
    N = w.shape[2]
    tm = _tile(M, 1024, BF16_SUBLANE)
    tn = _tile(N, 512, LANE)
    return pl.pallas_call(
        _out_proj_kernel,
        grid=(M // tm, N // tn),
        in_specs=[pl.BlockSpec((tm, K), lambda i, j: (i, 0)),
                  pl.BlockSpec((None, K, tn), lambda i, j: (layer, 0, j)),
                  pl.BlockSpec((tm, tn), lambda i, j: (i, j))],
        out_specs=pl.BlockSpec((tm, tn), lambda i, j: (i, j)),
        out_shape=jax.ShapeDtypeStruct((M, N), F32),
        compiler_params=_params("parallel", "arbitrary"),
    )(a, w, x)


def _ple_kernel(a_ref, wg_ref, p_ref, wp_ref, x_ref, o_ref):
    gate = jax.nn.sigmoid(jnp.dot(a_ref[...], wg_ref[...].astype(BF16), preferred_element_type=F32))
    emb = jnp.dot(p_ref[...], wp_ref[...], preferred_element_type=F32)
    o_ref[...] = x_ref[...] + gate * emb


def ple(a, wg, p, wp, x, layer):
    M, K = a.shape
    N = wg.shape[2]
    P = p.shape[1]
    tm = _tile(M, 1024, BF16_SUBLANE)
    tn = _tile(N, 512, LANE)
    return pl.pallas_call(
        _ple_kernel,
        grid=(M // tm, N // tn),
        in_specs=[pl.BlockSpec((tm, K), lambda i, j: (i, 0)),
                  pl.BlockSpec((None, K, tn), lambda i, j: (layer, 0, j)),
                  pl.BlockSpec((tm, P), lambda i, j: (i, 0)),
                  pl.BlockSpec((P, tn), lambda i, j: (0, j)),
                  pl.BlockSpec((tm, tn), lambda i, j: (i, j))],
        out_specs=pl.BlockSpec((tm, tn), lambda i, j: (i, j)),
        out_shape=jax.ShapeDtypeStruct((M, N), F32),
        compiler_params=_params("parallel", "arbitrary"),
    )(a, wg, p, wp, x)


SCALE = HEAD_DIM ** -0.5
KEY_TILE = 256
QUERY_TILE = 256
SHORT_QUERY_KEY_TILE = 2048
ASSEMBLE_PAGES = 8
INT32_MIN = -2 ** 31

_NT = (((1,), (1,)), ((), ()))


def _qk(q, k):
    return lax.dot_general(q, k, _NT, preferred_element_type=F32)


def _softmax_init(m_ref, l_ref, acc_ref):
    m_ref[...] = jnp.full(m_ref.shape, NEG, F32)
    l_ref[...] = jnp.zeros(l_ref.shape, F32)
    acc_ref[...] = jnp.zeros(acc_ref.shape, F32)


def _softmax_step(s, mask, v, m_ref, l_ref, acc_ref, r):
    s = jnp.where(mask, s, NEG)
    m_prev = m_ref[r]
    m_new = jnp.maximum(m_prev, jnp.max(s, axis=-1, keepdims=True))
    p = jnp.where(mask, jnp.exp(s - m_new), 0.0)
    alpha = jnp.exp(m_prev - m_new)
    l_ref[r] = alpha * l_ref[r] + jnp.sum(p, axis=-1, keepdims=True)
    acc_ref[r] = alpha * acc_ref[r] + jnp.dot(p.astype(BF16), v, preferred_element_type=F32)
    m_ref[r] = m_new


def _near_or_far(near, step):
    @pl.when(near)
    def _():
        step(True)

    @pl.when(jnp.logical_not(near))
    def _():
        step(False)


def _softmax_out(l_ref, acc_ref, r):
    return acc_ref[r] / jnp.maximum(l_ref[r], 1e-30)


def _positions(qpos0, kpos0, tq, tk):
    qpos = qpos0 + lax.broadcasted_iota(jnp.int32, (tq, tk), 0)
    kpos = kpos0 + lax.broadcasted_iota(jnp.int32, (tq, tk), 1)
    return qpos, kpos


def _lane_column(x, idx):
    lane = lax.broadcasted_iota(jnp.int32, x.shape, 1)
    return jnp.sum(jnp.where(lane == idx, x, 0.0), axis=-1, keepdims=True)


def _t5_bucket(n):
    n = jnp.maximum(n, 0)
    exact = N_BUCKETS // 2
    nf = jnp.maximum(n, 1).astype(jnp.float32)
    big = exact + (jnp.log(nf / exact) * ((N_BUCKETS - exact) / math.log(MAX_DISTANCE / exact))).astype(jnp.int32)
    return jnp.where(n < exact, n, jnp.minimum(big, N_BUCKETS - 1))


def _bias_lookup(tab, dist):
    onehot = jax.nn.one_hot(_t5_bucket(dist), N_BUCKETS, dtype=F32)
    return jnp.einsum('...n,nh->h...', onehot, tab.astype(F32), precision=lax.Precision.HIGHEST)


def _toeplitz_bias(tab, tq, tk, offsets):
    i = jnp.arange(tq)[:, None]
    j = jnp.arange(tk)[None, :]
    d = jnp.stack([off + i - j for off in offsets])
    return _bias_lookup(tab, d) - tab[N_BUCKETS - 1].astype(F32)[:, None, None, None]


class _Geom:
    def __init__(self, b, T, past_len):
        self.b, self.T, self.past_len = b, T, past_len
        self.tq = min(QUERY_TILE, T)
        self.tk = KEY_TILE if T >= QUERY_TILE else _tile(past_len, SHORT_QUERY_KEY_TILE, KEY_TILE)
        assert T % self.tq == 0 and self.tk % self.tq == 0 and past_len % self.tk == 0
        assert self.tq == self.tk or T == self.tq, "a query tile must not straddle key tiles"
        assert MAX_DISTANCE <= self.tk, "bias must be constant two key tiles behind the query tile"
        self.nqt = T // self.tq
        self.L = past_len + T
        self.nkt = (self.L - 1) // self.tk + 1

    def q_tile(self, qt):
        return (self.past_len + qt * self.tq) // self.tk


def _bias_spec(geom, heads):
    return pl.BlockSpec((heads, 1, geom.tq, geom.tk),
                        lambda i, q, k: (0, jnp.clip(geom.q_tile(q) - k, 0, 1), 0, 0))


def _assemble_kernel(pt_ref, *refs, n_in, n_page_steps, T, R):
    ins, new_ref, o_ref = refs[:n_in], refs[n_in], refs[n_in + 1]
    s = pl.program_id(1)

    @pl.when(s < n_page_steps)
    def _():
        for p in range(n_in):
            for c in range(R):
                rows = ins[p][0, 0, pl.ds(c, PAGE_SIZE, stride=R), :] if R > 1 else ins[p][0, 0]
                o_ref[0, p * PAGE_SIZE:(p + 1) * PAGE_SIZE, c * HEAD_DIM:(c + 1) * HEAD_DIM] = rows.astype(o_ref.dtype)

    @pl.when(s >= n_page_steps)
    def _():
        o_ref[0] = jnp.zeros(o_ref.shape[1:], o_ref.dtype)

    @pl.when(s == n_page_steps)
    def _():
        o_ref[0, :T, :] = new_ref[0]


def assemble_cache(cache, layer, page_table, new, pad_rows):
    b, n_pages = page_table.shape
    T, C = new.shape[1], new.shape[2]
    R = C // HEAD_DIM
    cache = cache.reshape(cache.shape[0], cache.shape[1], PAGE_SIZE * R, HEAD_DIM)
    P = _tile(n_pages, ASSEMBLE_PAGES, 1)
    steps = n_pages // P
    assert pad_rows % (P * PAGE_SIZE) == 0 and T <= P * PAGE_SIZE
    pad_steps = pad_rows // (P * PAGE_SIZE)

    def page_spec(p):
        return pl.BlockSpec((1, 1, PAGE_SIZE * R, HEAD_DIM),
                            lambda i, s, pt: (layer, pt[i, jnp.minimum(s * P + p, n_pages - 1)], 0, 0))

    grid_spec = pltpu.PrefetchScalarGridSpec(
        num_scalar_prefetch=1,
        grid=(b, steps + pad_steps),
        in_specs=[page_spec(p) for p in range(P)] + [pl.BlockSpec((1, T, C), lambda i, s, pt: (i, 0, 0))],
        out_specs=pl.BlockSpec((1, P * PAGE_SIZE, C), lambda i, s, pt: (i, s, 0)),
    )
    return pl.pallas_call(
        functools.partial(_assemble_kernel, n_in=P, n_page_steps=steps, T=T, R=R),
        grid_spec=grid_spec,
        out_shape=jax.ShapeDtypeStruct((b, (steps + pad_steps) * P * PAGE_SIZE, C), BF16),
        compiler_params=_params("parallel", "arbitrary"),
    )(page_table, *([cache] * P), new)


def _indexer_kernel(qi_ref, wi_ref, ki_ref, sc_ref, thr_ref, key_ref, *, geom, topk):
    tq, tk = geom.tq, geom.tk
    qt, kt = pl.program_id(1), pl.program_id(2)
    qpos0 = geom.past_len + qt * tq
    last = geom.q_tile(qt)

    @pl.when(kt == 0)
    def _():
        sc_ref[0] = jnp.full(sc_ref.shape[1:], NEG, F32)
        key_ref[...] = jnp.full(key_ref.shape, INT32_MIN, jnp.int32)

    @pl.when(kt <= last)
    def _():
        ki = ki_ref[0]
        wi = wi_ref[0]
        acc = jnp.zeros((tq, tk), F32)
        for h in range(H_IDX):
            s = _qk(qi_ref[0, :, h * D_IDX:(h + 1) * D_IDX], ki)
            acc = acc + jnp.maximum(s, 0.0) * wi[:, h:h + 1]
        qpos, kpos = _positions(qpos0, kt * tk, tq, tk)
        score = jnp.where(kpos <= qpos, acc * (D_IDX ** -0.5 * H_IDX ** -0.5), NEG)
        sc_ref[0, kt] = score
        i = lax.bitcast_convert_type(score, jnp.int32)
        key_ref[kt] = i ^ ((i >> 31) & 0x7FFFFFFF)

    @pl.when(kt == geom.nkt - 1)
    def _():
        def body(bit, lo):
            cand = lo + jnp.left_shift(jnp.int32(1), 31 - bit)
            ge = (key_ref[...] >= cand[None]).astype(F32)
            cnt = jnp.sum(jnp.sum(ge, axis=0), axis=-1, keepdims=True)
            return jnp.where(cnt >= topk, cand, lo)

        lo = lax.fori_loop(0, 32, body, jnp.full((tq, 1), INT32_MIN, jnp.int32))
        thr = lax.bitcast_convert_type(lo ^ ((lo >> 31) & 0x7FFFFFFF), F32)
        thr_ref[0] = jnp.broadcast_to(thr, (tq, LANE))


def dsa_indexer(geom, qarr, qi_col, miscarr, misc_col, kiarr, ki_col):
    b, T, tq, tk, nkt = geom.b, geom.T, geom.tq, geom.tk, geom.nkt
    topk = min(TOPK_A, geom.L // 4)
    assert topk <= tk, "the first causal key tile alone must hold topk entries"
    return pl.pallas_call(
        functools.partial(_indexer_kernel, geom=geom, topk=topk),
        grid=(b, geom.nqt, nkt),
        in_specs=[pl.BlockSpec((1, tq, H_IDX * D_IDX), lambda i, q, k: (i, q, qi_col)),
                  pl.BlockSpec((1, tq, LANE), lambda i, q, k: (i, q, misc_col)),
                  pl.BlockSpec((1, tk, D_IDX), lambda i, q, k: (i, jnp.minimum(k, geom.q_tile(q)), ki_col))],
        out_specs=[pl.BlockSpec((1, nkt, tq, tk), lambda i, q, k: (i, 0, q, 0)),
                   pl.BlockSpec((1, tq, LANE), lambda i, q, k: (i, q, 0))],
        out_shape=[jax.ShapeDtypeStruct((b, nkt, T, tk), F32), jax.ShapeDtypeStruct((b, T, LANE), F32)],
        scratch_shapes=[pltpu.VMEM((nkt, tq, tk), jnp.int32)],
        compiler_params=_params("parallel", "parallel", "arbitrary"),
    )(qarr, miscarr, kiarr)


def _dsa_attn_kernel(q_ref, k_ref, v_ref, sc_ref, thr_ref, bias_ref, o_ref, m_ref, l_ref, acc_ref, *, geom, rep):
    tq, tk = geom.tq, geom.tk
    qt, kt = pl.program_id(1), pl.program_id(2)
    qpos0 = geom.past_len + qt * tq
    last = geom.q_tile(qt)

    @pl.when(kt == 0)
    def _():
        _softmax_init(m_ref, l_ref, acc_ref)

    @pl.when(kt <= last)
    def _():
        qpos, kpos = _positions(qpos0, kt * tk, tq, tk)
        mask = (sc_ref[0, 0] >= thr_ref[0][:, :1]) & (kpos <= qpos)

        def step(add_bias):
            for h in range(H_A):
                kv_cols = slice((h // rep) * HEAD_DIM, (h // rep + 1) * HEAD_DIM)
                s = _qk(q_ref[0, :, h * HEAD_DIM:(h + 1) * HEAD_DIM], k_ref[0, :, kv_cols]) * SCALE
                if add_bias:
                    s = s + bias_ref[h, 0]
                _softmax_step(s, mask, v_ref[0, :, kv_cols], m_ref, l_ref, acc_ref, h)

        _near_or_far((last - kt) <= 1, step)

    @pl.when(kt == geom.nkt - 1)
    def _():
        for h in range(H_A):
            o_ref[0, :, h * HEAD_DIM:(h + 1) * HEAD_DIM] = _softmax_out(l_ref, acc_ref, h).astype(o_ref.dtype)


def dsa_attention(geom, qarr, q_col, karr, k_col, v_col, scores, thr, bias):
    b, T, tq, tk, nkt = geom.b, geom.T, geom.tq, geom.tk, geom.nkt

    def kmap(col):
        return lambda i, q, k: (i, jnp.minimum(k, geom.q_tile(q)), col)

    return pl.pallas_call(
        functools.partial(_dsa_attn_kernel, geom=geom, rep=H_A // KV_A),
        grid=(b, geom.nqt, nkt),
        in_specs=[pl.BlockSpec((1, tq, H_A * HEAD_DIM), lambda i, q, k: (i, q, q_col)),
                  pl.BlockSpec((1, tk, KV_A * HEAD_DIM), kmap(k_col)),
                  pl.BlockSpec((1, tk, KV_A * HEAD_DIM), kmap(v_col)),
                  pl.BlockSpec((1, 1, tq, tk), lambda i, q, k: (i, jnp.minimum(k, geom.q_tile(q)), q, 0)),
                  pl.BlockSpec((1, tq, LANE), lambda i, q, k: (i, q, 0)),
                  _bias_spec(geom, H_A)],
        out_specs=pl.BlockSpec((1, tq, H_A * HEAD_DIM), lambda i, q, k: (i, q, 0)),
        out_shape=jax.ShapeDtypeStruct((b, T, H_A * HEAD_DIM), BF16),
        scratch_shapes=[pltpu.VMEM((H_A, tq, 1), F32), pltpu.VMEM((H_A, tq, 1), F32),
                        pltpu.VMEM((H_A, tq, HEAD_DIM), F32)],
        compiler_params=_params("parallel", "parallel", "arbitrary"),
    )(qarr, karr, karr, scores, thr, bias)


def _moba_select_kernel(q_ref, k_ref, avg_ref, sel_ref, *, geom, nblk, ksel):
    qt = pl.program_id(2)
    cur = (geom.past_len + qt * geom.tq) // MOBA_BLOCK
    kmean = jnp.dot(avg_ref[...], k_ref[0], preferred_element_type=F32)
    gs = _qk(q_ref[0], kmean.astype(BF16))
    lane = lax.broadcasted_iota(jnp.int32, gs.shape, 1)
    past = lane < cur
    g = jnp.where(past, gs, NEG)
    cnt = jnp.zeros_like(g)
    for m in range(nblk):
        col = g[:, m:m + 1]
        cnt = cnt + ((col > g) | ((col == g) & (lane > m))).astype(F32)
    sel_ref[0, 0] = (past & (cnt < ksel)).astype(F32)


def moba_select(geom, qarr, q_col, karr, k_col):
    b, T, tq = geom.b, geom.T, geom.tq
    nblk = geom.L // MOBA_BLOCK
    assert 0 < nblk <= LANE and MOBA_BLOCK % tq == 0 and geom.tk % MOBA_BLOCK == 0
    ksel = min(MOBA_TOPK, nblk)
    rows = nblk * MOBA_BLOCK
    avg = np.zeros((LANE, rows), np.float32)
    avg[np.arange(rows) // MOBA_BLOCK, np.arange(rows)] = 1.0 / MOBA_BLOCK
    return pl.pallas_call(
        functools.partial(_moba_select_kernel, geom=geom, nblk=nblk, ksel=ksel),
        grid=(b, H_C, geom.nqt),
        in_specs=[pl.BlockSpec((1, tq, HEAD_DIM), lambda i, h, q: (i, q, q_col + h)),
                  pl.BlockSpec((1, rows, HEAD_DIM), lambda i, h, q: (i, 0, k_col + h)),
                  pl.BlockSpec((LANE, rows), lambda i, h, q: (0, 0))],
        out_specs=pl.BlockSpec((1, 1, tq, LANE), lambda i, h, q: (i, h, q, 0)),
        out_shape=jax.ShapeDtypeStruct((b, H_C, T, LANE), F32),
        compiler_params=_params("parallel", "parallel", "arbitrary"),
    )(qarr, karr, jnp.asarray(avg, BF16))


def _moba_attn_kernel(q_ref, k_ref, v_ref, sel_ref, bias_ref, o_ref, m_ref, l_ref, acc_ref, *, geom):
    tq, tk = geom.tq, geom.tk
    qt, kt = pl.program_id(1), pl.program_id(2)
    qpos0 = geom.past_len + qt * tq
    cur = geom.q_tile(qt)

    @pl.when(kt == 0)
    def _():
        _softmax_init(m_ref, l_ref, acc_ref)

    @pl.when(kt <= cur)
    def _():
        qpos, kpos = _positions(qpos0, kt * tk, tq, tk)
        own_blk = qpos0 // MOBA_BLOCK
        blk = lax.broadcasted_iota(jnp.int32, (LANE, tk), 0)
        tok = lax.broadcasted_iota(jnp.int32, (LANE, tk), 1)
        expand = (blk == kt * (tk // MOBA_BLOCK) + tok // MOBA_BLOCK).astype(BF16)
        own = (kpos // MOBA_BLOCK == own_blk) & (kpos <= qpos)

        def step(add_bias):
            for h in range(H_C):
                cols = slice(h * HEAD_DIM, (h + 1) * HEAD_DIM)
                chosen = jnp.dot(sel_ref[0, h].astype(BF16), expand, preferred_element_type=F32) > 0.5
                s = _qk(q_ref[0, :, cols], k_ref[0, :, cols]) * SCALE
                if add_bias:
                    s = s + bias_ref[h, 0]
                _softmax_step(s, chosen | own, v_ref[0, :, cols], m_ref, l_ref, acc_ref, h)

        _near_or_far((cur - kt) <= 1, step)

    @pl.when(kt == geom.nkt - 1)
    def _():
        for h in range(H_C):
            o_ref[0, :, h * HEAD_DIM:(h + 1) * HEAD_DIM] = _softmax_out(l_ref, acc_ref, h).astype(o_ref.dtype)


def moba_attention(geom, qarr, q_col, karr, k_col, v_col, sel, bias):
    b, T, tq, tk, nkt = geom.b, geom.T, geom.tq, geom.tk, geom.nkt
    width = H_C * HEAD_DIM

    def kmap(col):
        return lambda i, q, k: (i, jnp.minimum(k, geom.q_tile(q)), col)

    return pl.pallas_call(
        functools.partial(_moba_attn_kernel, geom=geom),
        grid=(b, geom.nqt, nkt),
        in_specs=[pl.BlockSpec((1, tq, width), lambda i, q, k: (i, q, q_col)),
                  pl.BlockSpec((1, tk, width), kmap(k_col)),
                  pl.BlockSpec((1, tk, width), kmap(v_col)),
                  pl.BlockSpec((1, H_C, tq, LANE), lambda i, q, k: (i, 0, q, 0)),
                  _bias_spec(geom, H_C)],
        out_specs=pl.BlockSpec((1, tq, width), lambda i, q, k: (i, q, 0)),
        out_shape=jax.ShapeDtypeStruct((b, T, width), BF16),
        scratch_shapes=[pltpu.VMEM((H_C, tq, 1), F32), pltpu.VMEM((H_C, tq, 1), F32),
                        pltpu.VMEM((H_C, tq, HEAD_DIM), F32)],
        compiler_params=_params("parallel", "parallel", "arbitrary"),
    )(qarr, karr, karr, sel, bias)


CMP_ROW = CMP_STRIDE * 2 * KV_B * HEAD_DIM


def _compress_kernel(x_ref, xn_ref, wlo_ref, whi_ref, c_ref, w2_ref, g_ref, o_ref, *, tr):
    x = x_ref[0]
    lo = jnp.dot(x, wlo_ref[...], preferred_element_type=F32)
    hi = jnp.dot(x, whi_ref[...], preferred_element_type=F32)
    hi_next = jnp.dot(xn_ref[0], whi_ref[...], preferred_element_type=F32)
    row = lax.broadcasted_iota(jnp.int32, hi.shape, 0)
    hi = jnp.where(row == tr - 1, hi_next[0:1], pltpu.roll(hi, tr - 1, 0))
    act = jax.nn.gelu(lo + hi + c_ref[...]).astype(BF16)
    for c in range(2 * KV_B):
        cols = slice(c * HEAD_DIM, (c + 1) * HEAD_DIM)
        y = jnp.dot(act[:, cols], w2_ref[c // KV_B], preferred_element_type=F32)
        if c < KV_B:
            y = y * lax.rsqrt(jnp.mean(y * y, axis=-1, keepdims=True) + EPS) * g_ref[...]
        o_ref[0, :, cols] = y.astype(o_ref.dtype)


def nsa_compress(x, wlo, whi, const, w2, gain):
    b, R, _ = x.shape
    tr = _tile(R, 128, BF16_SUBLANE)
    nxt = tr // BF16_SUBLANE
    n_sub16 = R // BF16_SUBLANE
    C = 2 * KV_B * HEAD_DIM
    return pl.pallas_call(
        functools.partial(_compress_kernel, tr=tr),
        grid=(b, R // tr),
        in_specs=[pl.BlockSpec((1, tr, CMP_ROW), lambda i, r: (i, r, 0)),
                  pl.BlockSpec((1, BF16_SUBLANE, CMP_ROW), lambda i, r: (i, jnp.minimum((r + 1) * nxt, n_sub16 - 1), 0)),
                  pl.BlockSpec((CMP_ROW, C), lambda i, r: (0, 0)),
                  pl.BlockSpec((CMP_ROW, C), lambda i, r: (0, 0)),
                  pl.BlockSpec((1, C), lambda i, r: (0, 0)),
                  pl.BlockSpec((2, HEAD_DIM, HEAD_DIM), lambda i, r: (0, 0, 0)),
                  pl.BlockSpec((1, HEAD_DIM), lambda i, r: (0, 0))],
        out_specs=pl.BlockSpec((1, tr, C), lambda i, r: (i, r, 0)),
        out_shape=jax.ShapeDtypeStruct((b, R, C), BF16),
        compiler_params=_params("parallel", "arbitrary"),
    )(x, x, wlo, whi, const, w2, gain)


def _prep_compress(cmp_pos, cmp_w1, cmp_w2, kn_cmp):
    r = CMP_LEN // CMP_STRIDE
    assert r == 2
    w1 = cmp_w1.reshape(2, r, CMP_STRIDE, HEAD_DIM, HEAD_DIM)
    kv_of_col = np.repeat(np.arange(2), KV_B)
    eye = jnp.eye(2 * KV_B, dtype=cmp_w1.dtype)

    def expand(half):
        w = w1[kv_of_col, half]
        return jnp.einsum('crde,cf->rcdfe', w, eye).reshape(CMP_ROW, 2 * KV_B * HEAD_DIM).astype(BF16)

    const = jnp.einsum('krd,krde->ke', cmp_pos.reshape(2, CMP_LEN, HEAD_DIM),
                       cmp_w1.reshape(2, CMP_LEN, HEAD_DIM, HEAD_DIM), precision=lax.Precision.HIGHEST)
    const = const[kv_of_col].reshape(1, -1).astype(F32)
    return dict(wlo=expand(0), whi=expand(1), const=const, w2=cmp_w2.astype(BF16), gain=kn_cmp.reshape(1, -1).astype(F32))


def _nsa_cmp_kernel(q_ref, kc_ref, vc_ref, bias_ref, ov_ref, oc_ref, sel_ref, *, geom, rep, n_slc, nsel):
    tq = geom.tq
    qt = pl.program_id(2)
    qpos0 = geom.past_len + qt * tq
    kc, vc, ov = kc_ref[0], vc_ref[0], ov_ref[...]
    ncp, nsp = ov.shape
    qpos, n_idx = _positions(qpos0, 0, tq, ncp)
    valid = (n_idx * CMP_STRIDE + (CMP_LEN - 1)) <= qpos
    imp = jnp.zeros((tq, nsp), F32)
    for r in range(rep):
        cols = slice(r * HEAD_DIM, (r + 1) * HEAD_DIM)
        s = jnp.where(valid, _qk(q_ref[0, :, cols], kc) * SCALE + bias_ref[r], NEG)
        e = jnp.where(valid, jnp.exp(s - jnp.max(s, axis=-1, keepdims=True)), 0.0)
        p = (e / jnp.maximum(jnp.sum(e, axis=-1, keepdims=True), 1e-30)).astype(BF16)
        oc_ref[0, :, cols] = jnp.dot(p, vc, preferred_element_type=F32)
        imp = imp + jnp.dot(p, ov, preferred_element_type=F32)
    spos, jj = _positions(qpos0, 0, tq, nsp)
    cur = spos // SLC_LEN
    forced = (jj == 0) | (jj == cur) | (jj == cur - 1)
    imp = jnp.where(jj > cur, NEG, jnp.where(forced, -NEG, imp))
    cnt = jnp.zeros_like(imp)
    for s_blk in range(n_slc):
        col = imp[:, s_blk:s_blk + 1]
        cnt = cnt + ((col > imp) | ((col == imp) & (jj > s_blk))).astype(F32)
    sel_ref[0, 0] = ((cnt < nsel) & (jj < n_slc)).astype(F32)


def nsa_compressed(geom, qarr, q_col, cmpkv, bias_cmp):
    b, T, tq = geom.b, geom.T, geom.tq
    rep = H_B // KV_B
    ncp = cmpkv.shape[1]
    n_slc = -(-geom.L // SLC_LEN)
    nsel = min(N_SLC, n_slc)
    nsp = _round_up(n_slc, LANE)
    n_cmp = geom.L // CMP_STRIDE - CMP_LEN // CMP_STRIDE + 1
    cstart = np.arange(ncp) * CMP_STRIDE
    cend = cstart + CMP_LEN - 1
    sstart = np.arange(nsp) * SLC_LEN
    ov = ((cstart[:, None] < sstart[None, :] + SLC_LEN) & (cend[:, None] >= sstart[None, :])
          & (np.arange(ncp)[:, None] < n_cmp) & (np.arange(nsp)[None, :] < n_slc)).astype(np.float32)
    return pl.pallas_call(
        functools.partial(_nsa_cmp_kernel, geom=geom, rep=rep, n_slc=n_slc, nsel=nsel),
        grid=(b, KV_B, geom.nqt),
        in_specs=[pl.BlockSpec((1, tq, rep * HEAD_DIM), lambda i, g, q: (i, q, q_col + g)),
                  pl.BlockSpec((1, ncp, HEAD_DIM), lambda i, g, q: (i, 0, g)),
                  pl.BlockSpec((1, ncp, HEAD_DIM), lambda i, g, q: (i, 0, KV_B + g)),
                  pl.BlockSpec((rep, tq, ncp), lambda i, g, q: (g, q, 0)),
                  pl.BlockSpec((ncp, nsp), lambda i, g, q: (0, 0))],
        out_specs=[pl.BlockSpec((1, tq, rep * HEAD_DIM), lambda i, g, q: (i, q, g)),
                   pl.BlockSpec((1, 1, tq, nsp), lambda i, g, q: (i, g, q, 0))],
        out_shape=[jax.ShapeDtypeStruct((b, T, H_B * HEAD_DIM), F32), jax.ShapeDtypeStruct((b, KV_B, T, nsp), F32)],
        compiler_params=_params("parallel", "parallel", "arbitrary"),
    )(qarr, cmpkv, cmpkv, bias_cmp, jnp.asarray(ov, BF16))


class _Window:
    def __init__(self, arr, k_col, v_col, tw, n_back, tile0, pos0, bias, bias_always):
        self.arr, self.k_col, self.v_col, self.tw, self.n_back = arr, k_col, v_col, tw, n_back
        self.tile0, self.pos0, self.bias, self.bias_always = tile0, pos0, bias, bias_always


def _nsa_attn_kernel(q_ref, ks_ref, vs_ref, kw_ref, vw_ref, sel_ref, bias_ref, wbias_ref, oc_ref, misc_ref, o_ref,
                     m_ref, l_ref, acc_ref, *, geom, rep, win):
    tq, tk = geom.tq, geom.tk
    qt, kt = pl.program_id(1), pl.program_id(2)
    qpos0 = geom.past_len + qt * tq
    last = geom.q_tile(qt)

    @pl.when(kt == 0)
    def _():
        _softmax_init(m_ref, l_ref, acc_ref)

    def branch(g, k_ref, v_ref, mask, slot0, b_ref, add_bias):
        kv_cols = slice(g * HEAD_DIM, (g + 1) * HEAD_DIM)
        for h in range(g * rep, (g + 1) * rep):
            s = _qk(q_ref[0, :, h * HEAD_DIM:(h + 1) * HEAD_DIM], k_ref[0, :, kv_cols]) * SCALE
            if add_bias:
                s = s + b_ref[h, 0]
            _softmax_step(s, mask, v_ref[0, :, kv_cols], m_ref, l_ref, acc_ref, slot0 + h)

    near = (last - kt) <= 1

    @pl.when(kt <= last)
    def _():
        qpos, kpos = _positions(qpos0, kt * tk, tq, tk)
        nsp = sel_ref.shape[-1]
        blk = lax.broadcasted_iota(jnp.int32, (nsp, tk), 0)
        tok = lax.broadcasted_iota(jnp.int32, (nsp, tk), 1)
        expand = (blk == kt * (tk // SLC_LEN) + tok // SLC_LEN).astype(BF16)

        def step(add_bias):
            for g in range(KV_B):
                chosen = jnp.dot(sel_ref[0, g].astype(BF16), expand, preferred_element_type=F32) > 0.5
                branch(g, ks_ref, vs_ref, chosen & (kpos <= qpos), 0, bias_ref, add_bias)

        _near_or_far(near, step)

    @pl.when((kt <= last) & (kt >= last - win.n_back))
    def _():
        qpos, kpos = _positions(qpos0, win.pos0 + (kt - win.tile0) * win.tw, tq, win.tw)
        dist = qpos - kpos

        def step(add_bias):
            for g in range(KV_B):
                branch(g, kw_ref, vw_ref, (dist >= 0) & (dist < WINDOW), H_B, wbias_ref, add_bias)

        if win.bias_always:
            step(True)
        else:
            _near_or_far(near, step)

    @pl.when(kt == geom.nkt - 1)
    def _():
        gates = jax.nn.sigmoid(misc_ref[0])
        for h in range(H_B):
            cols = slice(h * HEAD_DIM, (h + 1) * HEAD_DIM)
            base = H_IDX + h * N_BRANCH
            o = (gates[:, base:base + 1] * oc_ref[0, :, cols]
                 + gates[:, base + 1:base + 2] * _softmax_out(l_ref, acc_ref, h)
                 + gates[:, base + 2:base + 3] * _softmax_out(l_ref, acc_ref, H_B + h))
            o_ref[0, :, cols] = o.astype(o_ref.dtype)


def nsa_attention(geom, qarr, q_col, sarr, ks_col, vs_col, win, sel, bias, oc, miscarr, misc_col):
    b, T, tq, tk, nkt = geom.b, geom.T, geom.tq, geom.tk, geom.nkt
    nsp = sel.shape[-1]
    tw = win.tw
    n_wt = win.arr.shape[1] // tw
    kv_width = KV_B * HEAD_DIM

    def smap(col):
        return lambda i, q, k: (i, jnp.minimum(k, geom.q_tile(q)), col)

    def wtile(q, k):
        last = geom.q_tile(q)
        return jnp.clip(jnp.clip(k, last - win.n_back, last) - win.tile0, 0, n_wt - 1)

    def wmap(col):
        return lambda i, q, k: (i, wtile(q, k), col)

    if win.bias_always:
        assert win.bias.shape[1] == 1 and n_wt == 1
        wbias_spec = pl.BlockSpec((H_B, 1, tq, tw), lambda i, q, k: (0, 0, 0, 0))
    else:
        wbias_spec = _bias_spec(geom, H_B)

    return pl.pallas_call(
        functools.partial(_nsa_attn_kernel, geom=geom, rep=H_B // KV_B, win=win),
        grid=(b, geom.nqt, nkt),
        in_specs=[pl.BlockSpec((1, tq, H_B * HEAD_DIM), lambda i, q, k: (i, q, q_col)),
                  pl.BlockSpec((1, tk, kv_width), smap(ks_col)),
                  pl.BlockSpec((1, tk, kv_width), smap(vs_col)),
                  pl.BlockSpec((1, tw, kv_width), wmap(win.k_col)),
                  pl.BlockSpec((1, tw, kv_width), wmap(win.v_col)),
                  pl.BlockSpec((1, KV_B, tq, nsp), lambda i, q, k: (i, 0, q, 0)),
                  _bias_spec(geom, H_B),
                  wbias_spec,
                  pl.BlockSpec((1, tq, H_B * HEAD_DIM), lambda i, q, k: (i, q, 0)),
                  pl.BlockSpec((1, tq, LANE), lambda i, q, k: (i, q, misc_col))],
        out_specs=pl.BlockSpec((1, tq, H_B * HEAD_DIM), lambda i, q, k: (i, q, 0)),
        out_shape=jax.ShapeDtypeStruct((b, T, H_B * HEAD_DIM), BF16),
        scratch_shapes=[pltpu.VMEM((2 * H_B, tq, 1), F32), pltpu.VMEM((2 * H_B, tq, 1), F32),
                        pltpu.VMEM((2 * H_B, tq, HEAD_DIM), F32)],
        compiler_params=_params("parallel", "parallel", "arbitrary"),
    )(qarr, sarr, sarr, win.arr, win.arr, sel, bias, win.bias, oc, miscarr)


_SEG = {}
_off = 0
for _name, _w in (("qa", H_A * HEAD_DIM), ("qb", H_B * HEAD_DIM), ("qi", H_IDX * D_IDX), ("qc", H_C * HEAD_DIM),
                  ("kvc", 2 * H_C * HEAD_DIM), ("kva", 2 * KV_A * HEAD_DIM), ("cmp", 2 * KV_B * HEAD_DIM),
                  ("slc", 2 * KV_B * HEAD_DIM), ("win", 2 * KV_B * HEAD_DIM), ("ki", D_IDX), ("misc", LANE)):
    _SEG[_name] = (_off, _off + _w)
    _off += _w
IN_COLS_ALIGNED = _off


def _col(name, width=LANE):
    assert _SEG[name][0] % width == 0
    return _SEG[name][0] // width


def _prep_in_proj(w_in, qn_a, kn_a, qn_b, kn_b, qn_c, kn_c):
    sizes = (H_A * HEAD_DIM, KV_A * HEAD_DIM, KV_A * HEAD_DIM, H_IDX * D_IDX, H_IDX, D_IDX,
             H_B * HEAD_DIM, N_BRANCH * H_B) + (KV_B * HEAD_DIM,) * 6 + (H_C * HEAD_DIM,) * 3
    splits = np.cumsum(sizes)[:-1].tolist()
    (qa, ka, va, qi, wi, ki, q_b, g_b, kbc, vbc, kbs, vbs, kbw, vbw, qc, kc, vc) = jnp.split(w_in, splits, axis=-1)
    D = w_in.shape[0]
    misc = jnp.concatenate([wi, g_b, jnp.zeros((D, LANE - H_IDX - N_BRANCH * H_B), w_in.dtype)], axis=-1)
    n_pad = _round_up(IN_COLS_ALIGNED, 512) - IN_COLS_ALIGNED
    w = jnp.concatenate([qa, q_b, qi, qc, kc, vc, ka, va, kbc, vbc, kbs, vbs, kbw, vbw, ki, misc,
                         jnp.zeros((D, n_pad), w_in.dtype)], axis=-1).astype(BF16)
    one = jnp.ones((HEAD_DIM,), F32)

    def rep(g, n):
        return jnp.tile(g.astype(F32), n)

    gains = jnp.concatenate([
        rep(qn_a, H_A), rep(qn_b, H_B), rep(one, H_IDX), rep(qn_c, H_C), rep(kn_c, H_C), rep(one, H_C),
        rep(kn_a, KV_A), rep(one, KV_A), rep(one, 2 * KV_B), rep(kn_b[1], KV_B), rep(one, KV_B),
        rep(kn_b[2], KV_B), rep(one, KV_B), one, one, jnp.ones((n_pad,), F32)]).reshape(1, -1)
    flags = np.concatenate([
        np.ones(H_A), np.ones(H_B), np.zeros(H_IDX), np.ones(H_C), np.ones(H_C), np.zeros(H_C),
        np.ones(KV_A), np.zeros(KV_A), np.zeros(2 * KV_B), np.ones(KV_B), np.zeros(KV_B),
        np.ones(KV_B), np.zeros(KV_B), np.zeros(1), np.zeros(1), np.zeros(n_pad // LANE)]).astype(np.int32)
    return w, gains, jnp.asarray(flags)


def _prep_layer(i, ffn1_norm, ffn1_wg, ffn1_wu, ffn1_wd, mix_norm, w_in, qn_a, kn_a, qn_b, kn_b, cmp_pos, cmp_w1,
                cmp_w2, qn_c, kn_c, wb_a, wb_b, wb_c, w_gate, b_gate, w_out, ffn2_norm, ffn2_wg, ffn2_wu, ffn2_wd,
                ple_norm, ple_wg, ple_wp):
    w_in_p, gains, flags = _prep_in_proj(w_in[i], qn_a[i], kn_a[i], qn_b[i], kn_b[i], qn_c[i], kn_c[i])
    return dict(
        layer=i, f1n=ffn1_norm[i], f1g=ffn1_wg, f1u=ffn1_wu, f1d=ffn1_wd[i].astype(BF16), mn=mix_norm[i],
        w_in=w_in_p, gains=gains, flags=flags,
        cmp=_prep_compress(cmp_pos[i], cmp_w1[i], cmp_w2[i], kn_b[i, 0]),
        wb_a=wb_a[i].astype(BF16), wb_b=wb_b[i].astype(BF16), wb_c=wb_c[i].astype(BF16),
        w_gate=w_gate[i].astype(BF16), b_gate=b_gate[i].reshape(1, -1), w_out=w_out,
        f2n=ffn2_norm[i], f2g=ffn2_wg, f2u=ffn2_wu, f2d=ffn2_wd[i].astype(BF16),
        pn=ple_norm[i], pwg=ple_wg, pwp=ple_wp[i].astype(BF16))


def _layer(x, p, geom, layer, caches, W, bias):
    b, T = geom.b, geom.T
    M = b * T
    h = ffn_down(ffn_up(rmsnorm_bf16(x, W["f1n"]), W["f1g"], W["f1u"], W["layer"]), W["f1d"], x)
    u = rmsnorm_bf16(h, W["mn"])
    proj32, proj16 = in_proj(u, W["w_in"], W["gains"], W["flags"])
    Np = proj32.shape[1]
    p32 = proj32.reshape(b, T, Np)
    p16 = proj16.reshape(b, T, Np)

    def seg32(name):
        lo, hi = _SEG[name]
        return p32[:, :, lo:hi]

    def seg16(name):
        lo, hi = _SEG[name]
        return p16[:, :, lo:hi]

    win_new = seg32("win")
    if caches is None:
        assert T >= WINDOW
        a_kv, a_ki, slc, c_kv = (p16,) * 4
        a_k, ki_c = _col("kva", KV_A * HEAD_DIM), _col("ki")
        s_k = _col("slc", KV_B * HEAD_DIM)
        c_k = _col("kvc", H_C * HEAD_DIM)
        w_k = _col("win", KV_B * HEAD_DIM)
        win = _Window(p16, w_k, w_k + 1, geom.tk, WINDOW // geom.tk, 0, 0, bias["b"], False)
        cmp_rows = seg16("cmp")
        win_state = win_new[:, T - WINDOW:]
    else:
        cache_a_kv, cache_a_kidx, cache_b_cmp, cache_b_slc, win_past, cache_c_kv, page_table = caches
        a_kv = assemble_cache(cache_a_kv, layer, page_table, seg16("kva"), geom.tk)
        a_ki = assemble_cache(cache_a_kidx, layer, page_table, seg16("ki"), geom.tk)
        cmp_rows = assemble_cache(cache_b_cmp, layer, page_table, seg16("cmp"), geom.tk)
        slc = assemble_cache(cache_b_slc, layer, page_table, seg16("slc"), geom.tk)
        c_kv = assemble_cache(cache_c_kv, layer, page_table, seg16("kvc"), geom.tk)
        a_k, ki_c, s_k, c_k = 0, 0, 0, 0
        n_buf = win_past.shape[2]
        win_past = win_past[layer].reshape(b, n_buf, -1)
        warr = jnp.concatenate([win_past.astype(BF16), seg16("win"),
                                jnp.zeros((b, KEY_TILE - T, win_past.shape[-1]), BF16)], axis=1)
        win = _Window(warr, 0, 1, warr.shape[1], 0, geom.q_tile(0), geom.past_len - n_buf, bias["b_win"], True)
        win_state = jnp.concatenate([win_past, win_new], axis=1)[:, -WINDOW:]

    scores, thr = dsa_indexer(geom, p16, _col("qi", H_IDX * D_IDX), p32, _col("misc"), a_ki, ki_c)
    o_a = dsa_attention(geom, p16, _col("qa", H_A * HEAD_DIM), a_kv, a_k, a_k + 1, scores, thr, bias["a"])
    cmpkv = nsa_compress(cmp_rows.reshape(b, -1, CMP_ROW), **W["cmp"])
    qpos = geom.past_len + jnp.arange(T)
    cend = jnp.arange(cmpkv.shape[1]) * CMP_STRIDE + CMP_LEN - 1
    bias_cmp = _bias_lookup(bias["tab_b"], qpos[:, None] - cend[None, :])
    oc, sel_b = nsa_compressed(geom, p16, _col("qb", 4 * HEAD_DIM), cmpkv, bias_cmp)
    o_b = nsa_attention(geom, p16, _col("qb", H_B * HEAD_DIM), slc, s_k, s_k + 1, win, sel_b, bias["b"], oc, p32,
                        _col("misc"))
    sel_c = moba_select(geom, p16, _col("qc"), c_kv, c_k * H_C)
    o_c = moba_attention(geom, p16, _col("qc", H_C * HEAD_DIM), c_kv, c_k, c_k + 1, sel_c, bias["c"])

    merged = gate_merge(u, o_a.reshape(M, -1), o_b.reshape(M, -1), o_c.reshape(M, -1),
                        W["w_gate"], W["b_gate"], W["wb_a"], W["wb_b"], W["wb_c"])
    h = out_proj(merged, W["w_out"], h, W["layer"])
    h = ffn_down(ffn_up(rmsnorm_bf16(h, W["f2n"]), W["f2g"], W["f2u"], W["layer"]), W["f2d"], h)
    h = ple(rmsnorm_bf16(h, W["pn"]), W["pwg"], p.astype(BF16), W["pwp"], h, W["layer"])
    state = (seg32("kva").reshape(b, T, 2, KV_A, HEAD_DIM), seg32("ki"), seg32("cmp").reshape(b, T, 2, KV_B, HEAD_DIM),
             seg32("slc").reshape(b, T, 2, KV_B, HEAD_DIM), win_state.reshape(b, WINDOW, 2, KV_B, HEAD_DIM),
             seg32("kvc").reshape(b, T, 2, H_C, HEAD_DIM))
    return h, state


def kernel(x_prompt, x_sample, cache_a_kv, cache_a_kidx, cache_b_cmp_kv, cache_b_slc_kv, state_b_win_kv, cache_c_kv, page_table, p_prompt, p_sample, rel_bias, ffn1_norm, ffn1_wg, ffn1_wu, ffn1_wd, mix_norm, w_in, qn_a, kn_a, qn_b, kn_b, cmp_pos, cmp_w1, cmp_w2, qn_c, kn_c, wb_a, wb_b, wb_c, w_gate, b_gate, w_out, ffn2_norm, ffn2_wg, ffn2_wu, ffn2_wd, ple_norm, ple_wg, ple_wp):
    depth = ffn1_norm.shape[0]
    bp, Tp, D = x_prompt.shape
    bs, Ts, _ = x_sample.shape
    geom_p = _Geom(bp, Tp, 0)
    geom_s = _Geom(bs, Ts, page_table.shape[1] * PAGE_SIZE)
    n_buf = state_b_win_kv.shape[2]
    assert n_buf >= WINDOW and Ts <= KEY_TILE
    tab_a, tab_b, tab_c = rel_bias[:, :H_A], rel_bias[:, H_A:H_A + H_B], rel_bias[:, H_A + H_B:]

    def tile_bias(geom):
        near = (0, geom.tk)
        return dict(a=_toeplitz_bias(tab_a, geom.tq, geom.tk, near), b=_toeplitz_bias(tab_b, geom.tq, geom.tk, near),
                    c=_toeplitz_bias(tab_c, geom.tq, geom.tk, near), tab_b=tab_b)

    bias_p = tile_bias(geom_p)
    bias_s = tile_bias(geom_s)
    bias_s["b_win"] = _toeplitz_bias(tab_b, geom_s.tq, n_buf + KEY_TILE, (n_buf,))
    caches = (cache_a_kv, cache_a_kidx, cache_b_cmp_kv, cache_b_slc_kv, state_b_win_kv, cache_c_kv, page_table)

    y_p = x_prompt.reshape(bp * Tp, D)
    y_s = x_sample.reshape(bs * Ts, D)
    sp_list, ss_list = [], []
    for i in range(depth):
        W = _prep_layer(i, ffn1_norm, ffn1_wg, ffn1_wu, ffn1_wd, mix_norm, w_in, qn_a, kn_a, qn_b, kn_b, cmp_pos,
                        cmp_w1, cmp_w2, qn_c, kn_c, wb_a, wb_b, wb_c, w_gate, b_gate, w_out, ffn2_norm, ffn2_wg,
                        ffn2_wu, ffn2_wd, ple_norm, ple_wg, ple_wp)
        y_p, sp = _layer(y_p, p_prompt[i].reshape(bp * Tp, -1), geom_p, i, None, W, bias_p)
        y_s, ss = _layer(y_s, p_sample[i].reshape(bs * Ts, -1), geom_s, i, caches, W, bias_s)
        sp_list.append(sp)
        ss_list.append(ss)

    outs = [y_p.reshape(bp, Tp, D), y_s.reshape(bs, Ts, D)]
    for j in range(6):
        outs.append(jnp.stack([s[j] for s in sp_list]))
        outs.append(jnp.stack([s[j] for s in ss_list]))
    return tuple(outs)
```
